```python
import math
import jax, jax.numpy as jnp
from jax import lax
import numpy as np

D_MODEL = 1024
BATCH = 16
SEQ = 2048
DEPTH = 2

CTX_LEN = 256
GRID_W = 64
MIX_WIDTH = D_MODEL
ATTN_HEADS = 4
ATTN_QK_DIM = 64
ATTN_V_DIM = 2 * ATTN_QK_DIM
Q_W = ATTN_HEADS * 2 * ATTN_QK_DIM
ATTN_WIDTH = ATTN_HEADS * ATTN_V_DIM
CONV_WIDTH = MIX_WIDTH - ATTN_WIDTH
CONV_GROUPS = 8
CONV_K = 3
K_OFF = Q_W
V_OFF = 2 * Q_W
B_OFF = V_OFF + ATTN_WIDTH
C_OFF = B_OFF + CONV_WIDTH
X_OFF = C_OFF + CONV_WIDTH
EVEN_IN = X_OFF + CONV_WIDTH
CHUNK = 128
CMLP_GROUPS = 4
CMLP_WIDTH = MIX_WIDTH
N_EXPERTS = 16
EXPERT_HIDDEN = D_MODEL
CAPACITY_FACTOR = 2
ROPE_THETA = 10000.0
NORM_EPS = 1e-6
Q_BLOCK = 128
N_EVEN = (DEPTH + 1) // 2
N_ODD = DEPTH // 2

kernel_name = "hybrid_diffattn_shortconv_chunkmlp_ecmoe_dit"


def _rms(x, g):
    xf = x.astype(jnp.float32)
    y = xf * lax.rsqrt(jnp.mean(xf * xf, axis=-1, keepdims=True) + NORM_EPS)
    return y.astype(x.dtype) * g


def _modulate(h, shift, scale):
    return h * (1 + scale) + shift


def _ctx_needed_after(l):
    return any(j % 2 == 0 for j in range(l + 1, DEPTH))


def _axial_rope_tables(n):
    rows = n // GRID_W
    row = jnp.repeat(jnp.arange(rows), GRID_W).astype(jnp.float32)
    col = jnp.tile(jnp.arange(GRID_W), rows).astype(jnp.float32)
    half = ATTN_QK_DIM // 2
    inv = 1.0 / (ROPE_THETA ** (jnp.arange(0, half, 2, dtype=jnp.float32) / half))
    ang_r = row[:, None] * inv
    ang_c = col[:, None] * inv
    ang = jnp.concatenate([ang_r, ang_r, ang_c, ang_c], axis=-1)
    return jnp.cos(ang), jnp.sin(ang)


def _apply_rope(x, cos, sin):
    quarter = ATTN_QK_DIM // 4
    xs = x.reshape(x.shape[:-1] + (2, 2, quarter))
    rot = jnp.stack([-xs[..., 1, :], xs[..., 0, :]], axis=-2).reshape(x.shape)
    cb = cos[None, :, None, None, :]
    sb = sin[None, :, None, None, :]
    return (x.astype(jnp.float32) * cb + rot.astype(jnp.float32) * sb).astype(x.dtype)


def _diff_attention(q, k, v, lam):
    bsz, n, heads, _, d = q.shape
    nb = n // Q_BLOCK
    qb = q.reshape(bsz, nb, Q_BLOCK, heads, 2, d).transpose(1, 0, 3, 4, 2, 5)
    kt = k.transpose(0, 2, 3, 1, 4)
    vt = v.transpose(0, 2, 1, 3)
    scale = d ** -0.5

    def one_block(q_blk):
        s = jnp.einsum('bhmqd,bhmkd->bhmqk', q_blk, kt).astype(jnp.float32) * scale
        p = jax.nn.softmax(s, axis=-1)
        a = p[:, :, 0] - lam * p[:, :, 1]
        return jnp.einsum('bhqk,bhkv->bhqv', a.astype(vt.dtype), vt)

    o = lax.map(one_block, qb)
    return o.transpose(1, 0, 3, 2, 4).reshape(bsz, n, heads, -1)


def _short_conv(u, w):
    n = u.shape[1]
    pad = CONV_K // 2
    up = jnp.pad(u, ((0, 0), (pad, pad), (0, 0)))
    return sum(up[:, i:i + n] * w[i] for i in range(CONV_K))


def _diff_heads(qp, kp, vp, lam, subln, lam_init, cos=None, sin=None, k_prefix=None, v_prefix=None):
    bsz, n, _ = qp.shape
    q = qp.reshape(bsz, n, ATTN_HEADS, 2, ATTN_QK_DIM)
    k = kp.reshape(bsz, kp.shape[1], ATTN_HEADS, 2, ATTN_QK_DIM)
    v = vp.reshape(bsz, vp.shape[1], ATTN_HEADS, ATTN_V_DIM)
    if cos is not None:
        q = _apply_rope(q, cos, sin)
        k = _apply_rope(k, cos, sin)
    if k_prefix is not None:
        k = jnp.concatenate([k_prefix, k], axis=1)
        v = jnp.concatenate([v_prefix, v], axis=1)
    o = _diff_attention(q, k, v, lam)
    o = _rms(o, subln) * (1.0 - lam_init)
    return o.reshape(bsz, n, ATTN_WIDTH)


def _even_mixer(h, hc, w_in, lam_p, subln, conv_w, cos, sin, lam_init, ctx_out):
    bsz, m = hc.shape[0], hc.shape[1]
    lam = (jnp.exp(jnp.sum(lam_p[0] * lam_p[1]).astype(jnp.float32))
           - jnp.exp(jnp.sum(lam_p[2] * lam_p[3]).astype(jnp.float32)) + lam_init)
    p = h @ w_in
    q, k, v, gb, gc, xs = jnp.split(p, [K_OFF, V_OFF, B_OFF, C_OFF, X_OFF], axis=-1)
    if ctx_out:
        pc = hc @ w_in
        qc, kc, vc, gbc, gcc, xsc = jnp.split(pc, [K_OFF, V_OFF, B_OFF, C_OFF, X_OFF], axis=-1)
    else:
        kvc = hc @ w_in[:, K_OFF:B_OFF]
        kc, vc = jnp.split(kvc, [V_OFF - K_OFF], axis=-1)
    kc_h = kc.reshape(bsz, m, ATTN_HEADS, 2, ATTN_QK_DIM)
    vc_h = vc.reshape(bsz, m, ATTN_HEADS, ATTN_V_DIM)
    attn = _diff_heads(q, k, v, lam, subln, lam_init, cos, sin, kc_h, vc_h)
    conv = gb * _short_conv(gc * xs, conv_w)
    mix = jnp.concatenate([attn, conv], axis=-1)
    mix_c = None
    if ctx_out:
        attn_c = _diff_heads(qc, kc, vc, lam, subln, lam_init)
        conv_c = gbc * _short_conv(gcc * xsc, conv_w)
        mix_c = jnp.concatenate([attn_c, conv_c], axis=-1)
    return mix, mix_c


def _chunk_mlp(h, w_in, v_norm, w_s, b_s):
    bsz, n, _ = h.shape
    p = jax.nn.gelu(h @ w_in, approximate=False)
    u, v = jnp.split(p, 2, axis=-1)
    v = _rms(v, v_norm)
    gw = CMLP_WIDTH // CMLP_GROUPS
    vc = v.reshape(bsz, n // CHUNK, CHUNK, CMLP_GROUPS, gw)
    s = jnp.einsum('gij,bcjgw->bcigw', w_s, vc) + b_s.T[None, None, :, :, None]
    return u * s.reshape(bsz, n, CMLP_WIDTH)


def _expert_choice_moe(h, w_r, w_gate, w_up, w_down):
    bsz, n, d = h.shape
    cap = CAPACITY_FACTOR * n // N_EXPERTS
    probs = jax.nn.softmax((h @ w_r).astype(jnp.float32), axis=-1)
    vals, idx = lax.top_k(jnp.swapaxes(probs, 1, 2), cap)
    xs = jax.vmap(lambda hb, ib: hb[ib])(h, idx)
    a = jnp.einsum('becd,edf->becf', xs, w_gate)
    b = jnp.einsum('becd,edf->becf', xs, w_up)
    y = jnp.einsum('becf,efd->becd', jax.nn.silu(a) * b, w_down) * vals[..., None].astype(h.dtype)
    return jax.vmap(lambda yb, ib: jnp.zeros((n, d), yb.dtype).at[ib.reshape(-1)].add(yb.reshape(-1, d)))(y, idx)


def setup_inputs(seed: int = 0) -> dict:
    key = jax.random.key(seed)
    ks = jax.random.split(key, 24)
    nrm = lambda k, shape, s: jax.random.normal(k, shape, jnp.float32) * s
    D = D_MODEL
    return {
        "x": nrm(ks[0], (BATCH, SEQ, D), 1.0),
        "c": nrm(ks[1], (BATCH, D), 1.0),
        "ctx": nrm(ks[2], (BATCH, CTX_LEN, D), 1.0),
        "c_ctx": nrm(ks[3], (D,), 1.0),
        "w_mod": nrm(ks[4], (DEPTH, D, 6 * D), 0.5 * D ** -0.5),
        "b_mod": nrm(ks[5], (DEPTH, 6 * D), 0.02),
        "norm1": 1.0 + nrm(ks[6], (DEPTH, D), 0.02),
        "norm2": 1.0 + nrm(ks[7], (DEPTH, D), 0.02),
        "even_w_in": nrm(ks[8], (N_EVEN, D, EVEN_IN), D ** -0.5),
        "even_lambda": nrm(ks[9], (N_EVEN, 4, ATTN_QK_DIM), 0.1),
        "even_subln": 1.0 + nrm(ks[10], (N_EVEN, ATTN_V_DIM), 0.02),
        "even_conv_w": nrm(ks[11], (N_EVEN, CONV_K, CONV_WIDTH), CONV_K ** -0.5),
        "odd_w_in": nrm(ks[12], (N_ODD, D, 2 * CMLP_WIDTH), D ** -0.5),
        "odd_v_norm": 1.0 + nrm(ks[13], (N_ODD, CMLP_WIDTH), 0.02),
        "odd_w_s": nrm(ks[14], (N_ODD, CMLP_GROUPS, CHUNK, CHUNK), CHUNK ** -0.5),
        "odd_b_s": 1.0 + nrm(ks[15], (N_ODD, CMLP_GROUPS, CHUNK), 0.01),
        "w_out": nrm(ks[16], (DEPTH, MIX_WIDTH, D), MIX_WIDTH ** -0.5),
        "w_router": nrm(ks[17], (DEPTH, D, N_EXPERTS), D ** -0.5),
        "w_gate": nrm(ks[18], (DEPTH, N_EXPERTS, D, EXPERT_HIDDEN), D ** -0.5),
        "w_up": nrm(ks[19], (DEPTH, N_EXPERTS, D, EXPERT_HIDDEN), D ** -0.5),
        "w_down": nrm(ks[20], (DEPTH, N_EXPERTS, EXPERT_HIDDEN, D), EXPERT_HIDDEN ** -0.5),
        "final_norm": 1.0 + nrm(ks[21], (D,), 0.02),
    }


def reference(x, c, ctx, c_ctx, w_mod, b_mod, norm1, norm2, even_w_in, even_lambda, even_subln,
              even_conv_w, odd_w_in, odd_v_norm, odd_w_s, odd_b_s, w_out, w_router, w_gate, w_up,
              w_down, final_norm):
    n_lat = x.shape[1]
    cos, sin = _axial_rope_tables(n_lat)
    xc = ctx
    sc_c = jax.nn.silu(c)
    sc_ctx = jax.nn.silu(c_ctx)
    for l in range(DEPTH):
        ctx_out = _ctx_needed_after(l)
        need_c = (l % 2 == 0) or ctx_out
        mod = sc_c @ w_mod[l] + b_mod[l]
        sh1, scl1, g1, sh2, scl2, g2 = jnp.split(mod[:, None, :], 6, axis=-1)
        h = _modulate(_rms(x, norm1[l]), sh1, scl1)
        if need_c:
            modc = sc_ctx @ w_mod[l] + b_mod[l]
            csh1, cscl1, cg1, csh2, cscl2, cg2 = jnp.split(modc, 6)
            hc = _modulate(_rms(xc, norm1[l]), csh1, cscl1)
        if l % 2 == 0:
            e = l // 2
            lam_init = 0.8 - 0.6 * math.exp(-0.3 * l)
            mix, mix_c = _even_mixer(h, hc, even_w_in[e], even_lambda[e], even_subln[e],
                                     even_conv_w[e], cos, sin, lam_init, ctx_out)
        else:
            o = l // 2
            mix = _chunk_mlp(h, odd_w_in[o], odd_v_norm[o], odd_w_s[o], odd_b_s[o])
            mix_c = _chunk_mlp(hc, odd_w_in[o], odd_v_norm[o], odd_w_s[o], odd_b_s[o]) if ctx_out else None
        x = x + g1 * (mix @ w_out[l])
        h2 = _modulate(_rms(x, norm2[l]), sh2, scl2)
        x = x + g2 * _expert_choice_moe(h2, w_router[l], w_gate[l], w_up[l], w_down[l])
        if ctx_out:
            xc = xc + cg1 * (mix_c @ w_out[l])
            hc2 = _modulate(_rms(xc, norm2[l]), csh2, cscl2)
            xc = xc + cg2 * _expert_choice_moe(hc2, w_router[l], w_gate[l], w_up[l], w_down[l])
    return _rms(x, final_norm)
```

```python
import functools
import math

import numpy as np
import jax
import jax.numpy as jnp
from jax import lax
from jax.experimental import pallas as pl
from jax.experimental.pallas import tpu as pltpu

F32 = jnp.float32
BF16 = jnp.bfloat16

D_MODEL = 1024
ATTN_HEADS = 4
QK_DIM = 64
V_DIM = 128
Q_W = ATTN_HEADS * 2 * QK_DIM
CONV_W = 512
EVEN_IN = 3072
N_EXPERTS = 16
CAPACITY_FACTOR = 2
GRID_W = 64
CHUNK = 128
CMLP_GROUPS = 4
ROPE_THETA = 10000.0
NORM_EPS = 1e-6
LANES = 128
HALO_ROWS = 8
VMEM_LIMIT = 56 * 1024 * 1024

NT_DIMS = (((1,), (1,)), ((), ()))
TN_DIMS = (((0,), (0,)), ((), ()))


def _dot(a, b):
    return jnp.dot(a, b, preferred_element_type=F32)


def _params(*sem):
    return pltpu.CompilerParams(dimension_semantics=sem, vmem_limit_bytes=VMEM_LIMIT)


def _rms_mod(x, g, shift, scale):
    y = x * lax.rsqrt(jnp.mean(x * x, axis=-1, keepdims=True) + NORM_EPS)
    return (y * g) * (1.0 + scale) + shift


def _split_bf16(x):
    hi = x.astype(BF16)
    lo = (x - hi.astype(F32)).astype(BF16)
    return hi, lo


def _mod_kernel(c_ref, w_ref, b_ref, o_ref):
    c = c_ref[...]
    a_hi, a_lo = _split_bf16(c * jax.nn.sigmoid(c))
    w_hi, w_lo = _split_bf16(w_ref[0])
    o_ref[0] = _dot(a_hi, w_hi) + _dot(a_lo, w_hi) + _dot(a_hi, w_lo) + b_ref[0]


def _modulation(cc, w_mod, b_mod):
    depth, d, six_d = w_mod.shape
    rows = cc.shape[0]
    tn = 1536
    return pl.pallas_call(
        _mod_kernel,
        grid=(depth, six_d // tn),
        in_specs=[
            pl.BlockSpec((rows, d), lambda l, j: (0, 0)),
            pl.BlockSpec((1, d, tn), lambda l, j: (l, 0, j)),
            pl.BlockSpec((1, 1, tn), lambda l, j: (l, 0, j)),
        ],
        out_specs=pl.BlockSpec((1, rows, tn), lambda l, j: (l, 0, j)),
        out_shape=jax.ShapeDtypeStruct((depth, rows, six_d), F32),
        compiler_params=_params("arbitrary", "arbitrary"),
        name="modulation",
    )(cc, w_mod, b_mod.reshape(depth, 1, six_d))


def _inproj_kernel(x_ref, xp_ref, xn_ref, mod_ref, n1_ref, w_ref, cos_ref, sa_ref, sb_ref, cw_ref,
                   q_ref, k_ref, v_ref, c_ref, *, tm, n_tiles):
    i = pl.program_id(1)
    m = mod_ref[0]
    shift, scale = m[0:1], m[1:2]
    g = n1_ref[...]
    h = _rms_mod(x_ref[0], g, shift, scale).astype(BF16)

    cos, sa, sb = cos_ref[...], sa_ref[...], sb_ref[...]
    for col0, out_ref, qscale in ((0, q_ref, QK_DIM ** -0.5), (Q_W, k_ref, 1.0)):
        p = _dot(h, w_ref[:, col0:col0 + Q_W])
        for j in range(Q_W // LANES):
            pj = p[:, j * LANES:(j + 1) * LANES]
            r = pj * cos + pltpu.roll(pj, LANES - 16, 1) * sa + pltpu.roll(pj, 16, 1) * sb
            out_ref[0, :, j * LANES:(j + 1) * LANES] = (r * qscale).astype(BF16)

    v_ref[0] = _dot(h, w_ref[:, 2 * Q_W:2 * Q_W + 512]).astype(BF16)

    pc = _dot(h, w_ref[:, 2 * Q_W + 512:])
    gb = pc[:, :CONV_W]
    u = pc[:, CONV_W:2 * CONV_W] * pc[:, 2 * CONV_W:]
    xh = jnp.concatenate([xp_ref[0], xn_ref[0]], axis=0)
    hh = _rms_mod(xh, g, shift, scale).astype(BF16)
    ph = _dot(hh, w_ref[:, 2 * Q_W + 512 + CONV_W:])
    uh = ph[:, :CONV_W] * ph[:, CONV_W:]
    u_before = jnp.where(i > 0, uh[HALO_ROWS - 1:HALO_ROWS], 0.0)
    u_after = jnp.where(i < n_tiles - 1, uh[HALO_ROWS:HALO_ROWS + 1], 0.0)
    row = lax.broadcasted_iota(jnp.int32, (tm, 1), 0)
    u_prev = jnp.where(row == 0, u_before, pltpu.roll(u, 1, 0))
    u_next = jnp.where(row == tm - 1, u_after, pltpu.roll(u, tm - 1, 0))
    cw = cw_ref[...]
    conv = cw[0:1] * u_prev + cw[1:2] * u + cw[2:3] * u_next
    c_ref[0] = (gb * conv).astype(BF16)


def _rope_tables(n):
    rows = n // GRID_W
    row = jnp.repeat(jnp.arange(rows), GRID_W).astype(F32)
    col = jnp.tile(jnp.arange(GRID_W), rows).astype(F32)
    half = QK_DIM // 2
    inv = 1.0 / (ROPE_THETA ** (jnp.arange(0, half, 2, dtype=F32) / half))
    ang_r = row[:, None] * inv
    ang_c = col[:, None] * inv
    ang = jnp.concatenate([ang_r, ang_r, ang_c, ang_c], axis=-1)
    cos, sin = jnp.cos(ang), jnp.sin(ang)
    first_half = (jnp.arange(QK_DIM) % 32) < 16
    sin_a = jnp.where(first_half, -sin, 0.0)
    sin_b = jnp.where(first_half, 0.0, sin)
    tile2 = lambda t: jnp.concatenate([t, t], axis=-1)
    return tile2(cos), tile2(sin_a), tile2(sin_b)


def _in_projection(x, mod0, norm1, w_in_bf, conv_w, tm):
    b, n, d = x.shape
    n_tiles = n // tm
    hb = tm // HALO_ROWS
    n_hblocks = n // HALO_ROWS
    cos, sa, sb = _rope_tables(n)
    out = jax.ShapeDtypeStruct((b, n, Q_W), BF16)
    row_spec = pl.BlockSpec((1, tm, Q_W), lambda bi, i: (bi, i, 0))
    tab_spec = pl.BlockSpec((tm, LANES), lambda bi, i: (i, 0))
    return pl.pallas_call(
        functools.partial(_inproj_kernel, tm=tm, n_tiles=n_tiles),
        grid=(b, n_tiles),
        in_specs=[
            pl.BlockSpec((1, tm, d), lambda bi, i: (bi, i, 0)),
            pl.BlockSpec((1, HALO_ROWS, d), lambda bi, i: (bi, jnp.maximum(i * hb - 1, 0), 0)),
            pl.BlockSpec((1, HALO_ROWS, d), lambda bi, i: (bi, jnp.minimum((i + 1) * hb, n_hblocks - 1), 0)),
            pl.BlockSpec((1, 6, d), lambda bi, i: (bi, 0, 0)),
            pl.BlockSpec((1, d), lambda bi, i: (0, 0)),
            pl.BlockSpec((d, EVEN_IN), lambda bi, i: (0, 0)),
            tab_spec, tab_spec, tab_spec,
            pl.BlockSpec((3, CONV_W), lambda bi, i: (0, 0)),
        ],
        out_specs=[row_spec, row_spec, row_spec, row_spec],
        out_shape=[out, out, out, out],
        compiler_params=_params("arbitrary", "arbitrary"),
        name="in_projection",
    )(x, x, x, mod0, norm1, w_in_bf, cos, sa, sb, conv_w)


def _ctxkv_kernel(x_ref, mod_ref, n1_ref, wk_ref, wv_ref, k_ref, v_ref):
    m = mod_ref[0]
    h = _rms_mod(x_ref[0], n1_ref[...], m[0:1], m[1:2]).astype(BF16)
    k_ref[0] = _dot(h, wk_ref[...]).astype(BF16)
    v_ref[0] = _dot(h, wv_ref[...]).astype(BF16)


def _context_kv(ctx, modc, norm1, w_in_bf):
    b, m, d = ctx.shape
    out = jax.ShapeDtypeStruct((b, m, Q_W), BF16)
    return pl.pallas_call(
        _ctxkv_kernel,
        grid=(b,),
        in_specs=[
            pl.BlockSpec((1, m, d), lambda bi: (bi, 0, 0)),
            pl.BlockSpec((1, 6, d), lambda bi: (0, 0, 0)),
            pl.BlockSpec((1, d), lambda bi: (0, 0)),
            pl.BlockSpec((d, Q_W), lambda bi: (0, 1)),
            pl.BlockSpec((d, Q_W), lambda bi: (0, 2)),
        ],
        out_specs=[pl.BlockSpec((1, m, Q_W), lambda bi: (bi, 0, 0))] * 2,
        out_shape=[out, out],
        compiler_params=_params("arbitrary"),
        name="context_kv",
    )(ctx, modc, norm1, w_in_bf, w_in_bf)


def _attn_kernel(lam_ref, q_ref, kc_ref, k_ref, vc_ref, v_ref, sub_ref, o_ref, *, lam_init, m_ctx):
    lp = lam_ref[...]
    lam = (jnp.exp(jnp.sum(lp[0:1] * lp[1:2], keepdims=True))
           - jnp.exp(jnp.sum(lp[2:3] * lp[3:4], keepdims=True)) + lam_init)
    q = q_ref[0]
    lane = lax.broadcasted_iota(jnp.int32, q.shape, 1)
    zero = jnp.zeros_like(q)
    kc, kl = kc_ref[0], k_ref[0]

    def softmax_parts(qm):
        s = jnp.concatenate(
            [lax.dot_general(qm, kc, NT_DIMS, preferred_element_type=F32),
             lax.dot_general(qm, kl, NT_DIMS, preferred_element_type=F32)], axis=1)
        e = jnp.exp(s - jnp.max(s, axis=1, keepdims=True))
        return e, jnp.sum(e, axis=1, keepdims=True)

    e1, l1 = softmax_parts(jnp.where(lane < QK_DIM, q, zero))
    e2, l2 = softmax_parts(jnp.where(lane >= QK_DIM, q, zero))
    a = (e1 * (1.0 / l1) - e2 * (lam / l2)).astype(BF16)
    o = _dot(a[:, :m_ctx], vc_ref[0]) + _dot(a[:, m_ctx:], v_ref[0])
    o = o * lax.rsqrt(jnp.mean(o * o, axis=-1, keepdims=True) + NORM_EPS)
    o_ref[0] = (o * sub_ref[...] * (1.0 - lam_init)).astype(BF16)


def _attention(q, kc, k, vc, v, lam_p, subln, lam_init, tq):
    b, n, _ = q.shape
    m_ctx = kc.shape[1]
    head_rows = lambda rows: pl.BlockSpec((1, rows, V_DIM), lambda bi, h, i: (bi, 0, h))
    return pl.pallas_call(
        functools.partial(_attn_kernel, lam_init=lam_init, m_ctx=m_ctx),
        grid=(b, ATTN_HEADS, n // tq),
        in_specs=[
            pl.BlockSpec((4, QK_DIM), lambda bi, h, i: (0, 0)),
            pl.BlockSpec((1, tq, V_DIM), lambda bi, h, i: (bi, i, h)),
            head_rows(m_ctx), head_rows(n), head_rows(m_ctx), head_rows(n),
            pl.BlockSpec((1, V_DIM), lambda bi, h, i: (0, 0)),
        ],
        out_specs=pl.BlockSpec((1, tq, V_DIM), lambda bi, h, i: (bi, i, h)),
        out_shape=jax.ShapeDtypeStruct((b, n, Q_W), BF16),
        compiler_params=_params("arbitrary", "arbitrary", "arbitrary"),
        name="diff_attention",
    )(lam_p, q, kc, k, vc, v, subln)


def _tail(y, x, m, n2, wrh_ref, wrl_ref, x1_ref, h2_ref, pt_ref):
    x1 = x + m[2:3] * y
    x1_ref[0] = x1
    h2 = _rms_mod(x1, n2, m[3:4], m[4:5])
    h_hi, h_lo = _split_bf16(h2)
    h2_ref[0] = h_hi
    wrh, wrl = wrh_ref[...], wrl_ref[...]
    logits = (lax.dot_general(wrh, h_hi, NT_DIMS, preferred_element_type=F32)
              + lax.dot_general(wrh, h_lo, NT_DIMS, preferred_element_type=F32)
              + lax.dot_general(wrl, h_hi, NT_DIMS, preferred_element_type=F32))
    ex = jnp.exp(logits - jnp.max(logits, axis=0, keepdims=True))
    pt_ref[0] = ex / jnp.sum(ex, axis=0, keepdims=True)


def _tail_specs(b, n, d, tm):
    in_specs = [
        pl.BlockSpec((1, d), lambda bi, i: (0, 0)),
        pl.BlockSpec((N_EXPERTS, d), lambda bi, i: (0, 0)),
        pl.BlockSpec((N_EXPERTS, d), lambda bi, i: (0, 0)),
    ]
    out_specs = [
        pl.BlockSpec((1, tm, d), lambda bi, i: (bi, i, 0)),
        pl.BlockSpec((1, tm, d), lambda bi, i: (bi, i, 0)),
        pl.BlockSpec((1, N_EXPERTS, tm), lambda bi, i: (bi, 0, i)),
    ]
    out_shape = [
        jax.ShapeDtypeStruct((b, n, d), F32),
        jax.ShapeDtypeStruct((b, n, d), BF16),
        jax.ShapeDtypeStruct((b, N_EXPERTS, n), F32),
    ]
    return in_specs, out_specs, out_shape


def _outproj_kernel(a_ref, c_ref, x_ref, mod_ref, wo_ref, n2_ref, wrh_ref, wrl_ref,
                    x1_ref, h2_ref, pt_ref):
    y = _dot(a_ref[0], wo_ref[:Q_W, :]) + _dot(c_ref[0], wo_ref[Q_W:, :])
    _tail(y, x_ref[0], mod_ref[0], n2_ref[...], wrh_ref, wrl_ref, x1_ref, h2_ref, pt_ref)


def _out_projection(attn, conv, x, mod0, w_out_bf, norm2, wr_hi, wr_lo, tm):
    b, n, d = x.shape
    tail_in, out_specs, out_shape = _tail_specs(b, n, d, tm)
    return pl.pallas_call(
        _outproj_kernel,
        grid=(b, n // tm),
        in_specs=[
            pl.BlockSpec((1, tm, Q_W), lambda bi, i: (bi, i, 0)),
            pl.BlockSpec((1, tm, CONV_W), lambda bi, i: (bi, i, 0)),
            pl.BlockSpec((1, tm, d), lambda bi, i: (bi, i, 0)),
            pl.BlockSpec((1, 6, d), lambda bi, i: (bi, 0, 0)),
            pl.BlockSpec((d, d), lambda bi, i: (0, 0)),
        ] + tail_in,
        out_specs=out_specs,
        out_shape=out_shape,
        compiler_params=_params("arbitrary", "arbitrary"),
        name="out_projection",
    )(attn, conv, x, mod0, w_out_bf, norm2, wr_hi, wr_lo)


def _gelu(x):
    return 0.5 * x * (1.0 + lax.erf(x * np.float32(1.0 / math.sqrt(2.0))))


def _cmlp_kernel(x_ref, mod_ref, n1_ref, w1_ref, vn_ref, ws_ref, bs_ref, wo_ref, n2_ref, wrh_ref, wrl_ref,
                 x1_ref, h2_ref, pt_ref, mix_ref, *, tm):
    m = mod_ref[0]
    x = x_ref[0]
    d = x.shape[-1]
    h = _rms_mod(x, n1_ref[...], m[0:1], m[1:2]).astype(BF16)
    u = _gelu(_dot(h, w1_ref[:, :d]))
    v = _gelu(_dot(h, w1_ref[:, d:]))
    v = v * lax.rsqrt(jnp.mean(v * v, axis=-1, keepdims=True) + NORM_EPS) * vn_ref[...]
    vb = v.astype(BF16)
    gw = d // CMLP_GROUPS
    bs = bs_ref[...]
    for c in range(tm // CHUNK):
        r0 = c * CHUNK
        for g in range(CMLP_GROUPS):
            s = _dot(ws_ref[g], vb[r0:r0 + CHUNK, g * gw:(g + 1) * gw]) + bs[:, g:g + 1]
            mix_ref[r0:r0 + CHUNK, g * gw:(g + 1) * gw] = (u[r0:r0 + CHUNK, g * gw:(g + 1) * gw] * s).astype(BF16)
    y = _dot(mix_ref[...], wo_ref[...])
    _tail(y, x, m, n2_ref[...], wrh_ref, wrl_ref, x1_ref, h2_ref, pt_ref)


def _chunk_mlp_layer(x, mod1, norm1, w1_bf, v_norm, ws_bf, bs_t, w_out_bf, norm2, wr_hi, wr_lo, tm):
    b, n, d = x.shape
    tail_in, out_specs, out_shape = _tail_specs(b, n, d, tm)
    return pl.pallas_call(
        functools.partial(_cmlp_kernel, tm=tm),
        grid=(b, n // tm),
        in_specs=[
            pl.BlockSpec((1, tm, d), lambda bi, i: (bi, i, 0)),
            pl.BlockSpec((1, 6, d), lambda bi, i: (bi, 0, 0)),
            pl.BlockSpec((1, d), lambda bi, i: (0, 0)),
            pl.BlockSpec((d, 2 * d), lambda bi, i: (0, 0)),
            pl.BlockSpec((1, d), lambda bi, i: (0, 0)),
            pl.BlockSpec((CMLP_GROUPS, CHUNK, CHUNK), lambda bi, i: (0, 0, 0)),
            pl.BlockSpec((CHUNK, CMLP_GROUPS), lambda bi, i: (0, 0)),
            pl.BlockSpec((d, d), lambda bi, i: (0, 0)),
        ] + tail_in,
        out_specs=out_specs,
        out_shape=out_shape,
        scratch_shapes=[pltpu.VMEM((tm, d), BF16)],
        compiler_params=_params("arbitrary", "arbitrary"),
        name="chunk_mlp",
    )(x, mod1, norm1, w1_bf, v_norm, ws_bf, bs_t, w_out_bf, norm2, wr_hi, wr_lo)


def _route_kernel(p_ref, sel_ref, *, cap, pchunk):
    rows, n = p_ref.shape

    def bit_step(it, t):
        cand = t | jnp.left_shift(jnp.int32(1), 30 - it)
        cf = lax.bitcast_convert_type(cand, F32)
        cnt = jnp.sum(jnp.where(p_ref[...] >= cf, 1.0, 0.0), axis=1, keepdims=True)
        return jnp.where(cnt >= cap, cand, t)

    t = lax.fori_loop(0, 31, bit_step, jnp.zeros((rows, 1), jnp.int32))
    tf = lax.bitcast_convert_type(t, F32)
    p = p_ref[...]
    gt = p > tf
    eq = jnp.logical_and(p >= tf, jnp.logical_not(gt))
    need = cap - jnp.sum(jnp.where(gt, 1.0, 0.0), axis=1, keepdims=True)

    ri = lax.broadcasted_iota(jnp.int32, (pchunk, pchunk), 0)
    ci = lax.broadcasted_iota(jnp.int32, (pchunk, pchunk), 1)
    upper = jnp.where(ri < ci, 1.0, 0.0).astype(BF16)

    def prefix(mask):
        mf = jnp.where(mask, 1.0, 0.0)
        carry = jnp.zeros((rows, 1), F32)
        parts = []
        for c in range(n // pchunk):
            blk = mf[:, c * pchunk:(c + 1) * pchunk]
            parts.append(_dot(blk.astype(BF16), upper) + carry)
            carry = carry + jnp.sum(blk, axis=1, keepdims=True)
        return jnp.concatenate(parts, axis=1)

    chosen = jnp.logical_or(gt, jnp.logical_and(eq, prefix(eq) < need))
    sel_ref[...] = jnp.where(chosen, prefix(chosen), -1.0)


def _route(pt, cap):
    b, e, n = pt.shape
    rows = b * e
    pchunk = min(256, n)
    sel = pl.pallas_call(
        functools.partial(_route_kernel, cap=cap, pchunk=pchunk),
        grid=(1,),
        in_specs=[pl.BlockSpec((rows, n), lambda i: (0, 0))],
        out_specs=pl.BlockSpec((rows, n), lambda i: (0, 0)),
        out_shape=jax.ShapeDtypeStruct((rows, n), F32),
        compiler_params=_params("arbitrary"),
        name="route",
    )(pt.reshape(rows, n))
    return sel.reshape(b, e, n)


def _expert_kernel(sel_ref, pt_ref, h_ref, wg_ref, wu_ref, wd_ref, y_ref, wg_s, wu_s, wd_s, *, cap):
    e = pl.program_id(0)

    @pl.when(pl.program_id(1) == 0)
    def _():
        wg_s[...] = wg_ref[0, 0].astype(BF16)
        wu_s[...] = wu_ref[0, 0].astype(BF16)
        wd_s[...] = wd_ref[0, 0].astype(BF16)

    sel = sel_ref[0, pl.ds(e, 1), :]
    prob = pt_ref[0, pl.ds(e, 1), :]
    n = sel.shape[-1]
    rank = lax.broadcasted_iota(jnp.int32, (cap, n), 0).astype(F32)
    hit = sel == rank
    onehot = jnp.where(hit, 1.0, 0.0).astype(BF16)
    vals = jnp.sum(jnp.where(hit, prob, 0.0), axis=1, keepdims=True)
    xs = _dot(onehot, h_ref[0]).astype(BF16)
    a = _dot(xs, wg_s[...])
    bm = _dot(xs, wu_s[...])
    hid = (a * jax.nn.sigmoid(a) * bm).astype(BF16)
    y_ref[0, 0] = (_dot(hid, wd_s[...]) * vals).astype(BF16)


def _expert_ffn(sel, pt, h2, w_gate, w_up, w_down, layer, cap):
    b, n, d = h2.shape
    hdim = w_gate.shape[-1]
    w_spec = lambda rows, cols: pl.BlockSpec((1, 1, rows, cols), lambda e, bi: (layer, e, 0, 0))
    return pl.pallas_call(
        functools.partial(_expert_kernel, cap=cap),
        grid=(N_EXPERTS, b),
        in_specs=[
            pl.BlockSpec((1, N_EXPERTS, n), lambda e, bi: (bi, 0, 0)),
            pl.BlockSpec((1, N_EXPERTS, n), lambda e, bi: (bi, 0, 0)),
            pl.BlockSpec((1, n, d), lambda e, bi: (bi, 0, 0)),
            w_spec(d, hdim), w_spec(d, hdim), w_spec(hdim, d),
        ],
        out_specs=pl.BlockSpec((1, 1, cap, d), lambda e, bi: (bi, e, 0, 0)),
        out_shape=jax.ShapeDtypeStruct((b, N_EXPERTS, cap, d), BF16),
        scratch_shapes=[pltpu.VMEM((d, hdim), BF16), pltpu.VMEM((d, hdim), BF16), pltpu.VMEM((hdim, d), BF16)],
        compiler_params=_params("arbitrary", "arbitrary"),
        name="expert_ffn",
    )(sel, pt, h2, w_gate, w_up, w_down)


def _combine_kernel(sel_ref, y_ref, x_ref, mod_ref, fn_ref, o_ref, *, cap, final):
    sel = sel_ref[0]
    tm = sel.shape[-1]
    rank = lax.broadcasted_iota(jnp.int32, (cap, tm), 0).astype(F32)
    acc = jnp.zeros(x_ref.shape[1:], F32)
    for e in range(N_EXPERTS):
        onehot = jnp.where(sel[e:e + 1, :] == rank, 1.0, 0.0).astype(BF16)
        acc = acc + lax.dot_general(onehot, y_ref[0, e], TN_DIMS, preferred_element_type=F32)
    x2 = x_ref[0] + mod_ref[0][5:6] * acc
    if final:
        x2 = x2 * lax.rsqrt(jnp.mean(x2 * x2, axis=-1, keepdims=True) + NORM_EPS) * fn_ref[...]
    o_ref[0] = x2


def _combine(sel, y, x1, mod_l, final_norm, cap, tm, final):
    b, n, d = x1.shape
    return pl.pallas_call(
        functools.partial(_combine_kernel, cap=cap, final=final),
        grid=(b, n // tm),
        in_specs=[
            pl.BlockSpec((1, N_EXPERTS, tm), lambda bi, i: (bi, 0, i)),
            pl.BlockSpec((1, N_EXPERTS, cap, d), lambda bi, i: (bi, 0, 0, 0)),
            pl.BlockSpec((1, tm, d), lambda bi, i: (bi, i, 0)),
            pl.BlockSpec((1, 6, d), lambda bi, i: (bi, 0, 0)),
            pl.BlockSpec((1, d), lambda bi, i: (0, 0)),
        ],
        out_specs=pl.BlockSpec((1, tm, d), lambda bi, i: (bi, i, 0)),
        out_shape=jax.ShapeDtypeStruct((b, n, d), F32),
        compiler_params=_params("arbitrary", "arbitrary"),
        name="moe_combine",
    )(sel, y, x1, mod_l, final_norm)


def _moe(x1, h2, pt, mod_l, w_gate, w_up, w_down, final_norm, layer, tm, final):
    n = x1.shape[1]
    cap = CAPACITY_FACTOR * n // N_EXPERTS
    sel = _route(pt, cap)
    y = _expert_ffn(sel, pt, h2, w_gate, w_up, w_down, layer, cap)
    return _combine(sel, y, x1, mod_l, final_norm, cap, tm, final)


def kernel(x, c, ctx, c_ctx, w_mod, b_mod, norm1, norm2, even_w_in, even_lambda, even_subln, even_conv_w,
           odd_w_in, odd_v_norm, odd_w_s, odd_b_s, w_out, w_router, w_gate, w_up, w_down, final_norm):
    b, n, d = x.shape
    depth = w_mod.shape[0]
    assert d == D_MODEL and depth == 2 and n % 256 == 0
    tm = min(512, n)
    tq = min(256, n)
    tc = min(256, n)

    cc = jnp.concatenate([c, c_ctx[None, :]], axis=0)
    mod = _modulation(cc, w_mod, b_mod).reshape(depth, b + 1, 6, d)
    row = lambda a, l: a[l].reshape(1, -1)
    wr_t = jnp.swapaxes(w_router, 1, 2)
    wr_hi = wr_t.astype(BF16)
    wr_lo = (wr_t - wr_hi.astype(F32)).astype(BF16)
    w_out_bf = w_out.astype(BF16)
    fnorm = final_norm.reshape(1, d)

    mod0, modc = mod[0, :b], mod[0, b:]
    w_in_bf = even_w_in[0].astype(BF16)
    q, k, v, conv = _in_projection(x, mod0, row(norm1, 0), w_in_bf, even_conv_w[0], tm)
    kc, vc = _context_kv(ctx, modc, row(norm1, 0), w_in_bf)
    lam_init = 0.8 - 0.6 * math.exp(-0.3 * 0)
    attn = _attention(q, kc, k, vc, v, even_lambda[0], even_subln[0].reshape(1, -1), lam_init, tq)
    x1, h2, pt = _out_projection(attn, conv, x, mod0, w_out_bf[0], row(norm2, 0), wr_hi[0], wr_lo[0], tm)
    x = _moe(x1, h2, pt, mod0, w_gate, w_up, w_down, fnorm, 0, tm, final=False)

    mod1 = mod[1, :b]
    x1, h2, pt = _chunk_mlp_layer(x, mod1, row(norm1, 1), odd_w_in[0].astype(BF16), row(odd_v_norm, 0),
                                  odd_w_s[0].astype(BF16), odd_b_s[0].T, w_out_bf[1], row(norm2, 1),
                                  wr_hi[1], wr_lo[1], tc)
    return _moe(x1, h2, pt, mod1, w_gate, w_up, w_down, fnorm, 1, tm, final=True)
```

```python
import functools
import math

import numpy as np
import jax
import jax.numpy as jnp
from jax import lax
from jax.experimental import pallas as pl
from jax.experimental.pallas import tpu as pltpu

F32 = jnp.float32
BF16 = jnp.bfloat16

D_MODEL = 1024
ATTN_HEADS = 4
QK_DIM = 64
V_DIM = 128
Q_W = ATTN_HEADS * 2 * QK_DIM
CONV_W = 512
EVEN_IN = 3072
N_EXPERTS = 16
CAPACITY_FACTOR = 2
GRID_W = 64
CHUNK = 128
CMLP_GROUPS = 4
ROPE_THETA = 10000.0
NORM_EPS = 1e-6
LANES = 128
HALO_ROWS = 8
VMEM_LIMIT = 56 * 1024 * 1024
PV_ROWS = V_DIM + 16
SAFE_SHIFT = 60.0

NT_DIMS = (((1,), (1,)), ((), ()))
TN_DIMS = (((0,), (0,)), ((), ()))


def _dot(a, b):
    return jnp.dot(a, b, preferred_element_type=F32)


def _params(*sem, flags=None):
    return pltpu.CompilerParams(dimension_semantics=sem, vmem_limit_bytes=VMEM_LIMIT, flags=flags)


def _rms_mod(x, g, shift, scale):
    y = x * lax.rsqrt(jnp.mean(x * x, axis=-1, keepdims=True) + NORM_EPS)
    return (y * g) * (1.0 + scale) + shift


def _split_bf16(x):
    hi = x.astype(BF16)
    lo = (x - hi.astype(F32)).astype(BF16)
    return hi, lo


def _mod_kernel(c_ref, w_ref, b_ref, o_ref):
    c = c_ref[...]
    a_hi, a_lo = _split_bf16(c * jax.nn.sigmoid(c))
    w_hi, w_lo = _split_bf16(w_ref[0])
    o_ref[0] = _dot(a_hi, w_hi) + _dot(a_lo, w_hi) + _dot(a_hi, w_lo) + b_ref[0]


def _modulation(cc, w_mod, b_mod):
    depth, d, six_d = w_mod.shape
    rows = cc.shape[0]
    tn = 1536
    return pl.pallas_call(
        _mod_kernel,
        grid=(depth, six_d // tn),
        in_specs=[
            pl.BlockSpec((rows, d), lambda l, j: (0, 0)),
            pl.BlockSpec((1, d, tn), lambda l, j: (l, 0, j)),
            pl.BlockSpec((1, 1, tn), lambda l, j: (l, 0, j)),
        ],
        out_specs=pl.BlockSpec((1, rows, tn), lambda l, j: (l, 0, j)),
        out_shape=jax.ShapeDtypeStruct((depth, rows, six_d), F32),
        compiler_params=_params("arbitrary", "arbitrary"),
        name="modulation",
    )(cc, w_mod, b_mod.reshape(depth, 1, six_d))


def _inproj_kernel(x_ref, xp_ref, xn_ref, mod_ref, n1_ref, w_ref, cos_ref, sa_ref, sb_ref, cw_ref,
                   q_ref, k_ref, vt_ref, c_ref, *, tm, n_tiles):
    i = pl.program_id(1)
    m = mod_ref[0]
    shift, scale = m[0:1], m[1:2]
    g = n1_ref[...]
    h = _rms_mod(x_ref[0], g, shift, scale).astype(BF16)

    cos, sa, sb = cos_ref[...], sa_ref[...], sb_ref[...]
    for col0, out_ref, qscale in ((0, q_ref, QK_DIM ** -0.5 * math.log2(math.e)), (Q_W, k_ref, 1.0)):
        p = _dot(h, w_ref[:, col0:col0 + Q_W])
        for j in range(Q_W // LANES):
            pj = p[:, j * LANES:(j + 1) * LANES]
            r = pj * cos + pltpu.roll(pj, LANES - 16, 1) * sa + pltpu.roll(pj, 16, 1) * sb
            out_ref[0, :, j * LANES:(j + 1) * LANES] = (r * qscale).astype(BF16)

    vt_ref[0] = _dot(h, w_ref[:, 2 * Q_W:2 * Q_W + 512]).T.astype(BF16)

    pc = _dot(h, w_ref[:, 2 * Q_W + 512:])
    gb = pc[:, :CONV_W]
    u = pc[:, CONV_W:2 * CONV_W] * pc[:, 2 * CONV_W:]
    xh = jnp.concatenate([xp_ref[0], xn_ref[0]], axis=0)
    hh = _rms_mod(xh, g, shift, scale).astype(BF16)
    ph = _dot(hh, w_ref[:, 2 * Q_W + 512 + CONV_W:])
    uh = ph[:, :CONV_W] * ph[:, CONV_W:]
    u_before = jnp.where(i > 0, uh[HALO_ROWS - 1:HALO_ROWS], 0.0)
    u_after = jnp.where(i < n_tiles - 1, uh[HALO_ROWS:HALO_ROWS + 1], 0.0)
    row = lax.broadcasted_iota(jnp.int32, (tm, 1), 0)
    u_prev = jnp.where(row == 0, u_before, pltpu.roll(u, 1, 0))
    u_next = jnp.where(row == tm - 1, u_after, pltpu.roll(u, tm - 1, 0))
    cw = cw_ref[...]
    conv = cw[0:1] * u_prev + cw[1:2] * u + cw[2:3] * u_next
    c_ref[0] = (gb * conv).astype(BF16)


def _rope_tables(n):
    rows = n // GRID_W
    row = jnp.repeat(jnp.arange(rows), GRID_W).astype(F32)
    col = jnp.tile(jnp.arange(GRID_W), rows).astype(F32)
    half = QK_DIM // 2
    inv = 1.0 / (ROPE_THETA ** (jnp.arange(0, half, 2, dtype=F32) / half))
    ang_r = row[:, None] * inv
    ang_c = col[:, None] * inv
    ang = jnp.concatenate([ang_r, ang_r, ang_c, ang_c], axis=-1)
    cos, sin = jnp.cos(ang), jnp.sin(ang)
    first_half = (jnp.arange(QK_DIM) % 32) < 16
    sin_a = jnp.where(first_half, -sin, 0.0)
    sin_b = jnp.where(first_half, 0.0, sin)
    tile2 = lambda t: jnp.concatenate([t, t], axis=-1)
    return tile2(cos), tile2(sin_a), tile2(sin_b)


def _in_projection(x, mod0, norm1, w_in_bf, conv_w, tm):
    b, n, d = x.shape
    n_tiles = n // tm
    hb = tm // HALO_ROWS
    n_hblocks = n // HALO_ROWS
    cos, sa, sb = _rope_tables(n)
    out = jax.ShapeDtypeStruct((b, n, Q_W), BF16)
    row_spec = pl.BlockSpec((1, tm, Q_W), lambda bi, i: (bi, i, 0))
    tab_spec = pl.BlockSpec((tm, LANES), lambda bi, i: (i, 0))
    return pl.pallas_call(
        functools.partial(_inproj_kernel, tm=tm, n_tiles=n_tiles),
        grid=(b, n_tiles),
        in_specs=[
            pl.BlockSpec((1, tm, d), lambda bi, i: (bi, i, 0)),
            pl.BlockSpec((1, HALO_ROWS, d), lambda bi, i: (bi, jnp.maximum(i * hb - 1, 0), 0)),
            pl.BlockSpec((1, HALO_ROWS, d), lambda bi, i: (bi, jnp.minimum((i + 1) * hb, n_hblocks - 1), 0)),
            pl.BlockSpec((1, 6, d), lambda bi, i: (bi, 0, 0)),
            pl.BlockSpec((1, d), lambda bi, i: (0, 0)),
            pl.BlockSpec((d, EVEN_IN), lambda bi, i: (0, 0)),
            tab_spec, tab_spec, tab_spec,
            pl.BlockSpec((3, CONV_W), lambda bi, i: (0, 0)),
        ],
        out_specs=[row_spec, row_spec, pl.BlockSpec((1, Q_W, tm), lambda bi, i: (bi, 0, i)), row_spec],
        out_shape=[out, out, jax.ShapeDtypeStruct((b, Q_W, n), BF16), out],
        compiler_params=_params("arbitrary", "arbitrary"),
        name="in_projection",
    )(x, x, x, mod0, norm1, w_in_bf, cos, sa, sb, conv_w)


def _ctxkv_kernel(x_ref, mod_ref, n1_ref, wk_ref, wv_ref, k_ref, vt_ref):
    m = mod_ref[0]
    h = _rms_mod(x_ref[0], n1_ref[...], m[0:1], m[1:2]).astype(BF16)
    k_ref[0] = _dot(h, wk_ref[...]).astype(BF16)
    vt_ref[0] = _dot(h, wv_ref[...]).T.astype(BF16)


def _context_kv(ctx, modc, norm1, w_in_bf):
    b, m, d = ctx.shape
    out = jax.ShapeDtypeStruct((b, m, Q_W), BF16)
    return pl.pallas_call(
        _ctxkv_kernel,
        grid=(b,),
        in_specs=[
            pl.BlockSpec((1, m, d), lambda bi: (bi, 0, 0)),
            pl.BlockSpec((1, 6, d), lambda bi: (0, 0, 0)),
            pl.BlockSpec((1, d), lambda bi: (0, 0)),
            pl.BlockSpec((d, Q_W), lambda bi: (0, 1)),
            pl.BlockSpec((d, Q_W), lambda bi: (0, 2)),
        ],
        out_specs=[pl.BlockSpec((1, m, Q_W), lambda bi: (bi, 0, 0)), pl.BlockSpec((1, Q_W, m), lambda bi: (bi, 0, 0))],
        out_shape=[out, jax.ShapeDtypeStruct((b, Q_W, m), BF16)],
        compiler_params=_params("arbitrary"),
        name="context_kv",
    )(ctx, modc, norm1, w_in_bf, w_in_bf)


def _attn_kernel(lam_ref, q_ref, kc_ref, k_ref, vct_ref, vt_ref, sub_ref, o_ref,
                 kx_ref, vxt_ref, kn_ref, s_ref, e_ref, ox_ref, *, lam_init, m_ctx):
    nk = kx_ref.shape[0]

    @pl.when(pl.program_id(2) == 0)
    def _():
        kx_ref[:m_ctx, :] = kc_ref[0]
        kx_ref[m_ctx:, :] = k_ref[0]
        vxt_ref[:V_DIM, :m_ctx] = vct_ref[0]
        vxt_ref[:V_DIM, m_ctx:] = vt_ref[0]
        row = lax.broadcasted_iota(jnp.int32, (PV_ROWS - V_DIM, nk), 0)
        vxt_ref[V_DIM:, :] = jnp.where(row == 0, 1.0, 0.0).astype(BF16)
        kf = kx_ref[...].astype(F32)
        ksq = kf * kf
        klane = lax.broadcasted_iota(jnp.int32, ksq.shape, 1)
        for mi, in_map in enumerate((klane < QK_DIM, klane >= QK_DIM)):
            norms = jnp.sum(jnp.where(in_map, ksq, 0.0), axis=1, keepdims=True)
            kn_ref[mi:mi + 1, :] = jnp.broadcast_to(jnp.max(norms, axis=0, keepdims=True), (1, LANES))

    lp = lam_ref[...]
    lam = (jnp.exp(jnp.sum(lp[0:1] * lp[1:2], keepdims=True))
           - jnp.exp(jnp.sum(lp[2:3] * lp[3:4], keepdims=True)) + lam_init)
    q = q_ref[0]
    lane = lax.broadcasted_iota(jnp.int32, q.shape, 1)
    zero = jnp.zeros_like(q)
    in_maps = (lane < QK_DIM, lane >= QK_DIM)
    qms = [jnp.where(in_map, q, zero) for in_map in in_maps]

    qf = q.astype(F32)
    qsq = (qf * qf).astype(BF16)
    ones_rows = jnp.ones((HALO_ROWS, LANES), BF16)
    shifts = []
    for mi, in_map in enumerate(in_maps):
        qq = lax.dot_general(ones_rows, jnp.where(in_map, qsq, zero), NT_DIMS,
                             preferred_element_type=F32)[0:1]
        shifts.append(jnp.sqrt(qq * kn_ref[mi:mi + 1, 0:1]) * 1.05)
    bound_is_safe = jnp.maximum(jnp.max(shifts[0]), jnp.max(shifts[1])) <= SAFE_SHIFT

    @pl.when(bound_is_safe)
    def _():
        for mi in range(2):
            st = lax.dot_general(kx_ref[...], qms[mi], NT_DIMS, preferred_element_type=F32)
            e_ref[mi] = jnp.exp2(st - shifts[mi]).astype(BF16)
            ox_ref[mi] = _dot(vxt_ref[...], e_ref[mi])

    @pl.when(jnp.logical_not(bound_is_safe))
    def _():
        for mi in range(2):
            s_ref[mi] = lax.dot_general(kx_ref[...], qms[mi], NT_DIMS, preferred_element_type=F32)
        for mi in range(2):
            e_ref[mi] = jnp.exp2(s_ref[mi] - jnp.max(s_ref[mi], axis=0, keepdims=True)).astype(BF16)
            ox_ref[mi] = _dot(vxt_ref[...], e_ref[mi])

    o1, l1 = ox_ref[0, :V_DIM, :], ox_ref[0, V_DIM:V_DIM + 1, :]
    o2, l2 = ox_ref[1, :V_DIM, :], ox_ref[1, V_DIM:V_DIM + 1, :]
    ot = o1 * (1.0 / l1) - o2 * (lam / l2)
    ot = ot * lax.rsqrt(jnp.mean(ot * ot, axis=0, keepdims=True) + NORM_EPS)
    o_ref[0] = (ot * sub_ref[...] * (1.0 - lam_init)).T.astype(BF16)


def _attention(q, kc, k, vct, vt, lam_p, subln, lam_init, tq):
    b, n, _ = q.shape
    m_ctx = kc.shape[1]
    head_rows = lambda rows: pl.BlockSpec((1, rows, V_DIM), lambda bi, h, i: (bi, 0, h))
    head_cols = lambda cols: pl.BlockSpec((1, V_DIM, cols), lambda bi, h, i: (bi, h, 0))
    return pl.pallas_call(
        functools.partial(_attn_kernel, lam_init=lam_init, m_ctx=m_ctx),
        grid=(b, ATTN_HEADS, n // tq),
        in_specs=[
            pl.BlockSpec((4, QK_DIM), lambda bi, h, i: (0, 0)),
            pl.BlockSpec((1, tq, V_DIM), lambda bi, h, i: (bi, i, h)),
            head_rows(m_ctx), head_rows(n), head_cols(m_ctx), head_cols(n),
            pl.BlockSpec((V_DIM, 1), lambda bi, h, i: (0, 0)),
        ],
        out_specs=pl.BlockSpec((1, tq, V_DIM), lambda bi, h, i: (bi, i, h)),
        out_shape=jax.ShapeDtypeStruct((b, n, Q_W), BF16),
        scratch_shapes=[pltpu.VMEM((m_ctx + n, V_DIM), BF16), pltpu.VMEM((PV_ROWS, m_ctx + n), BF16),
                        pltpu.VMEM((HALO_ROWS, LANES), F32),
                        pltpu.VMEM((2, m_ctx + n, tq), F32), pltpu.VMEM((2, m_ctx + n, tq), BF16),
                        pltpu.VMEM((2, PV_ROWS, tq), F32)],
        compiler_params=_params("arbitrary", "arbitrary", "arbitrary"),
        name="diff_attention",
    )(lam_p, q, kc, k, vct, vt, subln)


def _tail(y, x, m, n2, wrh_ref, wrl_ref, x1_ref, h2_ref, pt_ref):
    x1 = x + m[2:3] * y
    x1_ref[0] = x1
    h2 = _rms_mod(x1, n2, m[3:4], m[4:5])
    h_hi, h_lo = _split_bf16(h2)
    h2_ref[0] = h_hi
    wrh, wrl = wrh_ref[...], wrl_ref[...]
    logits = (lax.dot_general(wrh, h_hi, NT_DIMS, preferred_element_type=F32)
              + lax.dot_general(wrh, h_lo, NT_DIMS, preferred_element_type=F32)
              + lax.dot_general(wrl, h_hi, NT_DIMS, preferred_element_type=F32))
    ex = jnp.exp(logits - jnp.max(logits, axis=0, keepdims=True))
    pt_ref[0] = ex / jnp.sum(ex, axis=0, keepdims=True)


def _tail_specs(b, n, d, tm):
    in_specs = [
        pl.BlockSpec((1, d), lambda bi, i: (0, 0)),
        pl.BlockSpec((N_EXPERTS, d), lambda bi, i: (0, 0)),
        pl.BlockSpec((N_EXPERTS, d), lambda bi, i: (0, 0)),
    ]
    out_specs = [
        pl.BlockSpec((1, tm, d), lambda bi, i: (bi, i, 0)),
        pl.BlockSpec((1, tm, d), lambda bi, i: (bi, i, 0)),
        pl.BlockSpec((1, N_EXPERTS, tm), lambda bi, i: (bi, 0, i)),
    ]
    out_shape = [
        jax.ShapeDtypeStruct((b, n, d), F32),
        jax.ShapeDtypeStruct((b, n, d), BF16),
        jax.ShapeDtypeStruct((b, N_EXPERTS, n), F32),
    ]
    return in_specs, out_specs, out_shape


def _outproj_kernel(a_ref, c_ref, x_ref, mod_ref, wo_ref, n2_ref, wrh_ref, wrl_ref,
                    x1_ref, h2_ref, pt_ref):
    y = _dot(a_ref[0], wo_ref[:Q_W, :]) + _dot(c_ref[0], wo_ref[Q_W:, :])
    _tail(y, x_ref[0], mod_ref[0], n2_ref[...], wrh_ref, wrl_ref, x1_ref, h2_ref, pt_ref)


def _out_projection(attn, conv, x, mod0, w_out_bf, norm2, wr_hi, wr_lo, tm):
    b, n, d = x.shape
    tail_in, out_specs, out_shape = _tail_specs(b, n, d, tm)
    return pl.pallas_call(
        _outproj_kernel,
        grid=(b, n // tm),
        in_specs=[
            pl.BlockSpec((1, tm, Q_W), lambda bi, i: (bi, i, 0)),
            pl.BlockSpec((1, tm, CONV_W), lambda bi, i: (bi, i, 0)),
            pl.BlockSpec((1, tm, d), lambda bi, i: (bi, i, 0)),
            pl.BlockSpec((1, 6, d), lambda bi, i: (bi, 0, 0)),
            pl.BlockSpec((d, d), lambda bi, i: (0, 0)),
        ] + tail_in,
        out_specs=out_specs,
        out_shape=out_shape,
        compiler_params=_params("arbitrary", "arbitrary"),
        name="out_projection",
    )(attn, conv, x, mod0, w_out_bf, norm2, wr_hi, wr_lo)


def _gelu(x):
    return 0.5 * x * (1.0 + lax.erf(x * np.float32(1.0 / math.sqrt(2.0))))


def _cmlp_kernel(x_ref, mod_ref, n1_ref, w1_ref, vn_ref, ws_ref, bs_ref, wo_ref, n2_ref, wrh_ref, wrl_ref,
                 x1_ref, h2_ref, pt_ref, mix_ref, *, tm):
    m = mod_ref[0]
    x = x_ref[0]
    d = x.shape[-1]
    h = _rms_mod(x, n1_ref[...], m[0:1], m[1:2]).astype(BF16)
    u = _gelu(_dot(h, w1_ref[:, :d]))
    v = _gelu(_dot(h, w1_ref[:, d:]))
    v = v * lax.rsqrt(jnp.mean(v * v, axis=-1, keepdims=True) + NORM_EPS) * vn_ref[...]
    vb = v.astype(BF16)
    gw = d // CMLP_GROUPS
    bs = bs_ref[...]
    for c in range(tm // CHUNK):
        r0 = c * CHUNK
        for g in range(CMLP_GROUPS):
            s = _dot(ws_ref[g], vb[r0:r0 + CHUNK, g * gw:(g + 1) * gw]) + bs[:, g:g + 1]
            mix_ref[r0:r0 + CHUNK, g * gw:(g + 1) * gw] = (u[r0:r0 + CHUNK, g * gw:(g + 1) * gw] * s).astype(BF16)
    y = _dot(mix_ref[...], wo_ref[...])
    _tail(y, x, m, n2_ref[...], wrh_ref, wrl_ref, x1_ref, h2_ref, pt_ref)


def _chunk_mlp_layer(x, mod1, norm1, w1_bf, v_norm, ws_bf, bs_t, w_out_bf, norm2, wr_hi, wr_lo, tm):
    b, n, d = x.shape
    tail_in, out_specs, out_shape = _tail_specs(b, n, d, tm)
    return pl.pallas_call(
        functools.partial(_cmlp_kernel, tm=tm),
        grid=(b, n // tm),
        in_specs=[
            pl.BlockSpec((1, tm, d), lambda bi, i: (bi, i, 0)),
            pl.BlockSpec((1, 6, d), lambda bi, i: (bi, 0, 0)),
            pl.BlockSpec((1, d), lambda bi, i: (0, 0)),
            pl.BlockSpec((d, 2 * d), lambda bi, i: (0, 0)),
            pl.BlockSpec((1, d), lambda bi, i: (0, 0)),
            pl.BlockSpec((CMLP_GROUPS, CHUNK, CHUNK), lambda bi, i: (0, 0, 0)),
            pl.BlockSpec((CHUNK, CMLP_GROUPS), lambda bi, i: (0, 0)),
            pl.BlockSpec((d, d), lambda bi, i: (0, 0)),
        ] + tail_in,
        out_specs=out_specs,
        out_shape=out_shape,
        scratch_shapes=[pltpu.VMEM((tm, d), BF16)],
        compiler_params=_params("arbitrary", "arbitrary"),
        name="chunk_mlp",
    )(x, mod1, norm1, w1_bf, v_norm, ws_bf, bs_t, w_out_bf, norm2, wr_hi, wr_lo)


def _route_kernel(p_ref, sel_ref, *, cap, pchunk):
    rows, n = p_ref.shape

    def bit_step(it, t):
        cand = t | jnp.left_shift(jnp.int32(1), 30 - it)
        cf = lax.bitcast_convert_type(cand, F32)
        cnt = jnp.sum(jnp.where(p_ref[...] >= cf, 1.0, 0.0), axis=1, keepdims=True)
        return jnp.where(cnt >= cap, cand, t)

    t = lax.fori_loop(0, 31, bit_step, jnp.zeros((rows, 1), jnp.int32))
    tf = lax.bitcast_convert_type(t, F32)
    p = p_ref[...]
    gt = p > tf
    eq = jnp.logical_and(p >= tf, jnp.logical_not(gt))
    need = cap - jnp.sum(jnp.where(gt, 1.0, 0.0), axis=1, keepdims=True)

    ri = lax.broadcasted_iota(jnp.int32, (pchunk, pchunk), 0)
    ci = lax.broadcasted_iota(jnp.int32, (pchunk, pchunk), 1)
    upper = jnp.where(ri < ci, 1.0, 0.0).astype(BF16)

    def prefix(mask):
        mf = jnp.where(mask, 1.0, 0.0)
        carry = jnp.zeros((rows, 1), F32)
        parts = []
        for c in range(n // pchunk):
            blk = mf[:, c * pchunk:(c + 1) * pchunk]
            parts.append(_dot(blk.astype(BF16), upper) + carry)
            carry = carry + jnp.sum(blk, axis=1, keepdims=True)
        return jnp.concatenate(parts, axis=1)

    chosen = jnp.logical_or(gt, jnp.logical_and(eq, prefix(eq) < need))
    sel_ref[...] = jnp.where(chosen, prefix(chosen), -1.0)


def _route(pt, cap):
    b, e, n = pt.shape
    rows = b * e
    pchunk = min(256, n)
    sel = pl.pallas_call(
        functools.partial(_route_kernel, cap=cap, pchunk=pchunk),
        grid=(1,),
        in_specs=[pl.BlockSpec((rows, n), lambda i: (0, 0))],
        out_specs=pl.BlockSpec((rows, n), lambda i: (0, 0)),
        out_shape=jax.ShapeDtypeStruct((rows, n), F32),
        compiler_params=_params("arbitrary"),
        name="route",
    )(pt.reshape(rows, n))
    return sel.reshape(b, e, n)


def _expert_kernel(sel_ref, pt_ref, h_ref, wg_ref, wu_ref, wd_ref, y_ref, wg_s, wu_s, wd_s, *, cap):
    e = pl.program_id(0)

    @pl.when(pl.program_id(1) == 0)
    def _():
        wg_s[...] = wg_ref[0, 0].astype(BF16)
        wu_s[...] = wu_ref[0, 0].astype(BF16)
        wd_s[...] = wd_ref[0, 0].astype(BF16)

    sel = sel_ref[0, pl.ds(e, 1), :]
    prob = pt_ref[0, pl.ds(e, 1), :]
    n = sel.shape[-1]
    rank = lax.broadcasted_iota(jnp.int32, (cap, n), 0).astype(F32)
    hit = sel == rank
    onehot = jnp.where(hit, 1.0, 0.0).astype(BF16)
    vals = jnp.sum(jnp.where(hit, prob, 0.0), axis=1, keepdims=True)
    xs = _dot(onehot, h_ref[0]).astype(BF16)
    a = _dot(xs, wg_s[...])
    bm = _dot(xs, wu_s[...])
    hid = (a * jax.nn.sigmoid(a) * bm).astype(BF16)
    y_ref[0, 0] = (_dot(hid, wd_s[...]) * vals).astype(BF16)


def _expert_ffn(sel, pt, h2, w_gate, w_up, w_down, layer, cap):
    b, n, d = h2.shape
    hdim = w_gate.shape[-1]
    w_spec = lambda rows, cols: pl.BlockSpec((1, 1, rows, cols), lambda e, bi: (layer, e, 0, 0))
    return pl.pallas_call(
        functools.partial(_expert_kernel, cap=cap),
        grid=(N_EXPERTS, b),
        in_specs=[
            pl.BlockSpec((1, N_EXPERTS, n), lambda e, bi: (bi, 0, 0)),
            pl.BlockSpec((1, N_EXPERTS, n), lambda e, bi: (bi, 0, 0)),
            pl.BlockSpec((1, n, d), lambda e, bi: (bi, 0, 0)),
            w_spec(d, hdim), w_spec(d, hdim), w_spec(hdim, d),
        ],
        out_specs=pl.BlockSpec((1, 1, cap, d), lambda e, bi: (bi, e, 0, 0)),
        out_shape=jax.ShapeDtypeStruct((b, N_EXPERTS, cap, d), BF16),
        scratch_shapes=[pltpu.VMEM((d, hdim), BF16), pltpu.VMEM((d, hdim), BF16), pltpu.VMEM((hdim, d), BF16)],
        compiler_params=_params("arbitrary", "arbitrary"),
        name="expert_ffn",
    )(sel, pt, h2, w_gate, w_up, w_down)


def _combine_kernel(sel_ref, y_ref, x_ref, mod_ref, fn_ref, o_ref, *, cap, final):
    sel = sel_ref[0]
    tm = sel.shape[-1]
    rank = lax.broadcasted_iota(jnp.int32, (cap, tm), 0).astype(F32)
    acc = jnp.zeros(x_ref.shape[1:], F32)
    for e in range(N_EXPERTS):
        onehot = jnp.where(sel[e:e + 1, :] == rank, 1.0, 0.0).astype(BF16)
        acc = acc + lax.dot_general(onehot, y_ref[0, e], TN_DIMS, preferred_element_type=F32)
    x2 = x_ref[0] + mod_ref[0][5:6] * acc
    if final:
        x2 = x2 * lax.rsqrt(jnp.mean(x2 * x2, axis=-1, keepdims=True) + NORM_EPS) * fn_ref[...]
    o_ref[0] = x2


def _combine(sel, y, x1, mod_l, final_norm, cap, tm, final):
    b, n, d = x1.shape
    return pl.pallas_call(
        functools.partial(_combine_kernel, cap=cap, final=final),
        grid=(b, n // tm),
        in_specs=[
            pl.BlockSpec((1, N_EXPERTS, tm), lambda bi, i: (bi, 0, i)),
            pl.BlockSpec((1, N_EXPERTS, cap, d), lambda bi, i: (bi, 0, 0, 0)),
            pl.BlockSpec((1, tm, d), lambda bi, i: (bi, i, 0)),
            pl.BlockSpec((1, 6, d), lambda bi, i: (bi, 0, 0)),
            pl.BlockSpec((1, d), lambda bi, i: (0, 0)),
        ],
        out_specs=pl.BlockSpec((1, tm, d), lambda bi, i: (bi, i, 0)),
        out_shape=jax.ShapeDtypeStruct((b, n, d), F32),
        compiler_params=_params("arbitrary", "arbitrary"),
        name="moe_combine",
    )(sel, y, x1, mod_l, final_norm)


def _moe(x1, h2, pt, mod_l, w_gate, w_up, w_down, final_norm, layer, tm, final):
    n = x1.shape[1]
    cap = CAPACITY_FACTOR * n // N_EXPERTS
    sel = _route(pt, cap)
    y = _expert_ffn(sel, pt, h2, w_gate, w_up, w_down, layer, cap)
    return _combine(sel, y, x1, mod_l, final_norm, cap, tm, final)


def kernel(x, c, ctx, c_ctx, w_mod, b_mod, norm1, norm2, even_w_in, even_lambda, even_subln, even_conv_w,
           odd_w_in, odd_v_norm, odd_w_s, odd_b_s, w_out, w_router, w_gate, w_up, w_down, final_norm):
    b, n, d = x.shape
    depth = w_mod.shape[0]
    assert d == D_MODEL and depth == 2 and n % 256 == 0
    tm = min(512, n)
    tq = min(512, n)
    tc = min(256, n)

    cc = jnp.concatenate([c, c_ctx[None, :]], axis=0)
    mod = _modulation(cc, w_mod, b_mod).reshape(depth, b + 1, 6, d)
    row = lambda a, l: a[l].reshape(1, -1)
    wr_t = jnp.swapaxes(w_router, 1, 2)
    wr_hi = wr_t.astype(BF16)
    wr_lo = (wr_t - wr_hi.astype(F32)).astype(BF16)
    w_out_bf = w_out.astype(BF16)
    fnorm = final_norm.reshape(1, d)

    mod0, modc = mod[0, :b], mod[0, b:]
    w_in_bf = even_w_in[0].astype(BF16)
    q, k, vt, conv = _in_projection(x, mod0, row(norm1, 0), w_in_bf, even_conv_w[0], tm)
    kc, vct = _context_kv(ctx, modc, row(norm1, 0), w_in_bf)
    lam_init = 0.8 - 0.6 * math.exp(-0.3 * 0)
    attn = _attention(q, kc, k, vct, vt, even_lambda[0], even_subln[0].reshape(-1, 1), lam_init, tq)
    x1, h2, pt = _out_projection(attn, conv, x, mod0, w_out_bf[0], row(norm2, 0), wr_hi[0], wr_lo[0], tm)
    x = _moe(x1, h2, pt, mod0, w_gate, w_up, w_down, fnorm, 0, tm, final=False)

    mod1 = mod[1, :b]
    x1, h2, pt = _chunk_mlp_layer(x, mod1, row(norm1, 1), odd_w_in[0].astype(BF16), row(odd_v_norm, 0),
                                  odd_w_s[0].astype(BF16), odd_b_s[0].T, w_out_bf[1], row(norm2, 1),
                                  wr_hi[1], wr_lo[1], tc)
    return _moe(x1, h2, pt, mod1, w_gate, w_up, w_down, fnorm, 1, tm, final=True)
```

```python
import functools
import math

import numpy as np
import jax
import jax.numpy as jnp
from jax import lax
from jax.experimental import pallas as pl
from jax.experimental.pallas import tpu as pltpu

F32 = jnp.float32
BF16 = jnp.bfloat16

D_MODEL = 1024
ATTN_HEADS = 4
QK_DIM = 64
V_DIM = 128
Q_W = ATTN_HEADS * 2 * QK_DIM
CONV_W = 512
EVEN_IN = 3072
N_EXPERTS = 16
CAPACITY_FACTOR = 2
GRID_W = 64
CHUNK = 128
CMLP_GROUPS = 4
ROPE_THETA = 10000.0
NORM_EPS = 1e-6
LANES = 128
HALO_ROWS = 8
VMEM_LIMIT = 56 * 1024 * 1024
PV_ROWS = V_DIM + 16
SAFE_SHIFT = 60.0
ALIGN_ROWS = 16
SUB_TOKENS = 256
SLOT_WINDOW = 64

NT_DIMS = (((1,), (1,)), ((), ()))
TN_DIMS = (((0,), (0,)), ((), ()))


def _dot(a, b):
    return jnp.dot(a, b, preferred_element_type=F32)


def _params(*sem, flags=None):
    return pltpu.CompilerParams(dimension_semantics=sem, vmem_limit_bytes=VMEM_LIMIT, flags=flags)


def _rms_mod(x, g, shift, scale):
    y = x * lax.rsqrt(jnp.mean(x * x, axis=-1, keepdims=True) + NORM_EPS)
    return (y * g) * (1.0 + scale) + shift


def _split_bf16(x):
    hi = x.astype(BF16)
    lo = (x - hi.astype(F32)).astype(BF16)
    return hi, lo


def _mod_kernel(c_ref, w_ref, b_ref, o_ref):
    c = c_ref[...]
    a_hi, a_lo = _split_bf16(c * jax.nn.sigmoid(c))
    w_hi, w_lo = _split_bf16(w_ref[0])
    o_ref[0] = _dot(a_hi, w_hi) + _dot(a_lo, w_hi) + _dot(a_hi, w_lo) + b_ref[0]


def _modulation(cc, w_mod, b_mod):
    depth, d, six_d = w_mod.shape
    rows = cc.shape[0]
    tn = 1536
    return pl.pallas_call(
        _mod_kernel,
        grid=(depth, six_d // tn),
        in_specs=[
            pl.BlockSpec((rows, d), lambda l, j: (0, 0)),
            pl.BlockSpec((1, d, tn), lambda l, j: (l, 0, j)),
            pl.BlockSpec((1, 1, tn), lambda l, j: (l, 0, j)),
        ],
        out_specs=pl.BlockSpec((1, rows, tn), lambda l, j: (l, 0, j)),
        out_shape=jax.ShapeDtypeStruct((depth, rows, six_d), F32),
        compiler_params=_params("arbitrary", "arbitrary"),
        name="modulation",
    )(cc, w_mod, b_mod.reshape(depth, 1, six_d))


def _inproj_kernel(x_ref, xp_ref, xn_ref, mod_ref, n1_ref, w_ref, cos_ref, sa_ref, sb_ref, cw_ref,
                   q_ref, k_ref, vt_ref, c_ref, *, tm, n_tiles):
    i = pl.program_id(1)
    m = mod_ref[0]
    shift, scale = m[0:1], m[1:2]
    g = n1_ref[...]
    h = _rms_mod(x_ref[0], g, shift, scale).astype(BF16)

    cos, sa, sb = cos_ref[...], sa_ref[...], sb_ref[...]
    for col0, out_ref, qscale in ((0, q_ref, QK_DIM ** -0.5 * math.log2(math.e)), (Q_W, k_ref, 1.0)):
        p = _dot(h, w_ref[:, col0:col0 + Q_W])
        for j in range(Q_W // LANES):
            pj = p[:, j * LANES:(j + 1) * LANES]
            r = pj * cos + pltpu.roll(pj, LANES - 16, 1) * sa + pltpu.roll(pj, 16, 1) * sb
            out_ref[0, :, j * LANES:(j + 1) * LANES] = (r * qscale).astype(BF16)

    vt_ref[0] = _dot(h, w_ref[:, 2 * Q_W:2 * Q_W + 512]).T.astype(BF16)

    pc = _dot(h, w_ref[:, 2 * Q_W + 512:])
    gb = pc[:, :CONV_W]
    u = pc[:, CONV_W:2 * CONV_W] * pc[:, 2 * CONV_W:]
    xh = jnp.concatenate([xp_ref[0], xn_ref[0]], axis=0)
    hh = _rms_mod(xh, g, shift, scale).astype(BF16)
    ph = _dot(hh, w_ref[:, 2 * Q_W + 512 + CONV_W:])
    uh = ph[:, :CONV_W] * ph[:, CONV_W:]
    u_before = jnp.where(i > 0, uh[HALO_ROWS - 1:HALO_ROWS], 0.0)
    u_after = jnp.where(i < n_tiles - 1, uh[HALO_ROWS:HALO_ROWS + 1], 0.0)
    row = lax.broadcasted_iota(jnp.int32, (tm, 1), 0)
    u_prev = jnp.where(row == 0, u_before, pltpu.roll(u, 1, 0))
    u_next = jnp.where(row == tm - 1, u_after, pltpu.roll(u, tm - 1, 0))
    cw = cw_ref[...]
    conv = cw[0:1] * u_prev + cw[1:2] * u + cw[2:3] * u_next
    c_ref[0] = (gb * conv).astype(BF16)


def _rope_tables(n):
    rows = n // GRID_W
    row = jnp.repeat(jnp.arange(rows), GRID_W).astype(F32)
    col = jnp.tile(jnp.arange(GRID_W), rows).astype(F32)
    half = QK_DIM // 2
    inv = 1.0 / (ROPE_THETA ** (jnp.arange(0, half, 2, dtype=F32) / half))
    ang_r = row[:, None] * inv
    ang_c = col[:, None] * inv
    ang = jnp.concatenate([ang_r, ang_r, ang_c, ang_c], axis=-1)
    cos, sin = jnp.cos(ang), jnp.sin(ang)
    first_half = (jnp.arange(QK_DIM) % 32) < 16
    sin_a = jnp.where(first_half, -sin, 0.0)
    sin_b = jnp.where(first_half, 0.0, sin)
    tile2 = lambda t: jnp.concatenate([t, t], axis=-1)
    return tile2(cos), tile2(sin_a), tile2(sin_b)


def _in_projection(x, mod0, norm1, w_in_bf, conv_w, tm):
    b, n, d = x.shape
    n_tiles = n // tm
    hb = tm // HALO_ROWS
    n_hblocks = n // HALO_ROWS
    cos, sa, sb = _rope_tables(n)
    out = jax.ShapeDtypeStruct((b, n, Q_W), BF16)
    row_spec = pl.BlockSpec((1, tm, Q_W), lambda bi, i: (bi, i, 0))
    tab_spec = pl.BlockSpec((tm, LANES), lambda bi, i: (i, 0))
    return pl.pallas_call(
        functools.partial(_inproj_kernel, tm=tm, n_tiles=n_tiles),
        grid=(b, n_tiles),
        in_specs=[
            pl.BlockSpec((1, tm, d), lambda bi, i: (bi, i, 0)),
            pl.BlockSpec((1, HALO_ROWS, d), lambda bi, i: (bi, jnp.maximum(i * hb - 1, 0), 0)),
            pl.BlockSpec((1, HALO_ROWS, d), lambda bi, i: (bi, jnp.minimum((i + 1) * hb, n_hblocks - 1), 0)),
            pl.BlockSpec((1, 6, d), lambda bi, i: (bi, 0, 0)),
            pl.BlockSpec((1, d), lambda bi, i: (0, 0)),
            pl.BlockSpec((d, EVEN_IN), lambda bi, i: (0, 0)),
            tab_spec, tab_spec, tab_spec,
            pl.BlockSpec((3, CONV_W), lambda bi, i: (0, 0)),
        ],
        out_specs=[row_spec, row_spec, pl.BlockSpec((1, Q_W, tm), lambda bi, i: (bi, 0, i)), row_spec],
        out_shape=[out, out, jax.ShapeDtypeStruct((b, Q_W, n), BF16), out],
        compiler_params=_params("arbitrary", "arbitrary"),
        name="in_projection",
    )(x, x, x, mod0, norm1, w_in_bf, cos, sa, sb, conv_w)


def _ctxkv_kernel(x_ref, mod_ref, n1_ref, wk_ref, wv_ref, k_ref, vt_ref):
    m = mod_ref[0]
    h = _rms_mod(x_ref[0], n1_ref[...], m[0:1], m[1:2]).astype(BF16)
    k_ref[0] = _dot(h, wk_ref[...]).astype(BF16)
    vt_ref[0] = _dot(h, wv_ref[...]).T.astype(BF16)


def _context_kv(ctx, modc, norm1, w_in_bf):
    b, m, d = ctx.shape
    out = jax.ShapeDtypeStruct((b, m, Q_W), BF16)
    return pl.pallas_call(
        _ctxkv_kernel,
        grid=(b,),
        in_specs=[
            pl.BlockSpec((1, m, d), lambda bi: (bi, 0, 0)),
            pl.BlockSpec((1, 6, d), lambda bi: (0, 0, 0)),
            pl.BlockSpec((1, d), lambda bi: (0, 0)),
            pl.BlockSpec((d, Q_W), lambda bi: (0, 1)),
            pl.BlockSpec((d, Q_W), lambda bi: (0, 2)),
        ],
        out_specs=[pl.BlockSpec((1, m, Q_W), lambda bi: (bi, 0, 0)), pl.BlockSpec((1, Q_W, m), lambda bi: (bi, 0, 0))],
        out_shape=[out, jax.ShapeDtypeStruct((b, Q_W, m), BF16)],
        compiler_params=_params("arbitrary"),
        name="context_kv",
    )(ctx, modc, norm1, w_in_bf, w_in_bf)


def _attn_kernel(lam_ref, q_ref, kc_ref, k_ref, vct_ref, vt_ref, sub_ref, o_ref,
                 kx_ref, vxt_ref, kn_ref, s_ref, e_ref, ox_ref, *, lam_init, m_ctx):
    nk = kx_ref.shape[0]

    @pl.when(pl.program_id(2) == 0)
    def _():
        kx_ref[:m_ctx, :] = kc_ref[0]
        kx_ref[m_ctx:, :] = k_ref[0]
        vxt_ref[:V_DIM, :m_ctx] = vct_ref[0]
        vxt_ref[:V_DIM, m_ctx:] = vt_ref[0]
        row = lax.broadcasted_iota(jnp.int32, (PV_ROWS - V_DIM, nk), 0)
        vxt_ref[V_DIM:, :] = jnp.where(row == 0, 1.0, 0.0).astype(BF16)
        kf = kx_ref[...].astype(F32)
        ksq = kf * kf
        klane = lax.broadcasted_iota(jnp.int32, ksq.shape, 1)
        for mi, in_map in enumerate((klane < QK_DIM, klane >= QK_DIM)):
            norms = jnp.sum(jnp.where(in_map, ksq, 0.0), axis=1, keepdims=True)
            kn_ref[mi:mi + 1, :] = jnp.broadcast_to(jnp.max(norms, axis=0, keepdims=True), (1, LANES))

    lp = lam_ref[...]
    lam = (jnp.exp(jnp.sum(lp[0:1] * lp[1:2], keepdims=True))
           - jnp.exp(jnp.sum(lp[2:3] * lp[3:4], keepdims=True)) + lam_init)
    q = q_ref[0]
    lane = lax.broadcasted_iota(jnp.int32, q.shape, 1)
    zero = jnp.zeros_like(q)
    in_maps = (lane < QK_DIM, lane >= QK_DIM)
    qms = [jnp.where(in_map, q, zero) for in_map in in_maps]

    qf = q.astype(F32)
    qsq = (qf * qf).astype(BF16)
    ones_rows = jnp.ones((HALO_ROWS, LANES), BF16)
    shifts = []
    for mi, in_map in enumerate(in_maps):
        qq = lax.dot_general(ones_rows, jnp.where(in_map, qsq, zero), NT_DIMS,
                             preferred_element_type=F32)[0:1]
        shifts.append(jnp.sqrt(qq * kn_ref[mi:mi + 1, 0:1]) * 1.05)
    bound_is_safe = jnp.maximum(jnp.max(shifts[0]), jnp.max(shifts[1])) <= SAFE_SHIFT

    @pl.when(bound_is_safe)
    def _():
        for mi in range(2):
            st = lax.dot_general(kx_ref[...], qms[mi], NT_DIMS, preferred_element_type=F32)
            e_ref[mi] = jnp.exp2(st - shifts[mi]).astype(BF16)
            ox_ref[mi] = _dot(vxt_ref[...], e_ref[mi])

    @pl.when(jnp.logical_not(bound_is_safe))
    def _():
        for mi in range(2):
            s_ref[mi] = lax.dot_general(kx_ref[...], qms[mi], NT_DIMS, preferred_element_type=F32)
        for mi in range(2):
            e_ref[mi] = jnp.exp2(s_ref[mi] - jnp.max(s_ref[mi], axis=0, keepdims=True)).astype(BF16)
            ox_ref[mi] = _dot(vxt_ref[...], e_ref[mi])

    o1, l1 = ox_ref[0, :V_DIM, :], ox_ref[0, V_DIM:V_DIM + 1, :]
    o2, l2 = ox_ref[1, :V_DIM, :], ox_ref[1, V_DIM:V_DIM + 1, :]
    ot = o1 * (1.0 / l1) - o2 * (lam / l2)
    ot = ot * lax.rsqrt(jnp.mean(ot * ot, axis=0, keepdims=True) + NORM_EPS)
    o_ref[0] = (ot * sub_ref[...] * (1.0 - lam_init)).T.astype(BF16)


def _attention(q, kc, k, vct, vt, lam_p, subln, lam_init, tq):
    b, n, _ = q.shape
    m_ctx = kc.shape[1]
    head_rows = lambda rows: pl.BlockSpec((1, rows, V_DIM), lambda bi, h, i: (bi, 0, h))
    head_cols = lambda cols: pl.BlockSpec((1, V_DIM, cols), lambda bi, h, i: (bi, h, 0))
    return pl.pallas_call(
        functools.partial(_attn_kernel, lam_init=lam_init, m_ctx=m_ctx),
        grid=(b, ATTN_HEADS, n // tq),
        in_specs=[
            pl.BlockSpec((4, QK_DIM), lambda bi, h, i: (0, 0)),
            pl.BlockSpec((1, tq, V_DIM), lambda bi, h, i: (bi, i, h)),
            head_rows(m_ctx), head_rows(n), head_cols(m_ctx), head_cols(n),
            pl.BlockSpec((V_DIM, 1), lambda bi, h, i: (0, 0)),
        ],
        out_specs=pl.BlockSpec((1, tq, V_DIM), lambda bi, h, i: (bi, i, h)),
        out_shape=jax.ShapeDtypeStruct((b, n, Q_W), BF16),
        scratch_shapes=[pltpu.VMEM((m_ctx + n, V_DIM), BF16), pltpu.VMEM((PV_ROWS, m_ctx + n), BF16),
                        pltpu.VMEM((HALO_ROWS, LANES), F32),
                        pltpu.VMEM((2, m_ctx + n, tq), F32), pltpu.VMEM((2, m_ctx + n, tq), BF16),
                        pltpu.VMEM((2, PV_ROWS, tq), F32)],
        compiler_params=_params("arbitrary", "arbitrary", "arbitrary"),
        name="diff_attention",
    )(lam_p, q, kc, k, vct, vt, subln)


def _tail(y, x, m, n2, wr_ref, x1_ref, h2_ref, pt_ref):
    x1 = x + m[2:3] * y
    x1_ref[0] = x1
    h2 = _rms_mod(x1, n2, m[3:4], m[4:5])
    h_hi, h_lo = _split_bf16(h2)
    h2_ref[0] = h_hi
    hcat = jnp.concatenate([h_hi, h_lo], axis=1)
    half = hcat.shape[0] // 2
    parts = [_dot(hcat[r0:r0 + half], wr_ref[...]) for r0 in (0, half)]
    prod = jnp.concatenate(parts, axis=0)
    logits = (prod[:, :LANES] + prod[:, LANES:]).T[:N_EXPERTS]
    ex = jnp.exp(logits - jnp.max(logits, axis=0, keepdims=True))
    pt_ref[0] = ex / jnp.sum(ex, axis=0, keepdims=True)


def _tail_specs(b, n, d, tm):
    in_specs = [
        pl.BlockSpec((1, d), lambda bi, i: (0, 0)),
        pl.BlockSpec((2 * d, 2 * LANES), lambda bi, i: (0, 0)),
    ]
    out_specs = [
        pl.BlockSpec((1, tm, d), lambda bi, i: (bi, i, 0)),
        pl.BlockSpec((1, tm, d), lambda bi, i: (bi, i, 0)),
        pl.BlockSpec((1, N_EXPERTS, tm), lambda bi, i: (bi, 0, i)),
    ]
    out_shape = [
        jax.ShapeDtypeStruct((b, n, d), F32),
        jax.ShapeDtypeStruct((b, n, d), BF16),
        jax.ShapeDtypeStruct((b, N_EXPERTS, n), F32),
    ]
    return in_specs, out_specs, out_shape


def _outproj_kernel(a_ref, c_ref, x_ref, mod_ref, wo_ref, n2_ref, wr_ref,
                    x1_ref, h2_ref, pt_ref, mix_ref):
    mix_ref[:, :Q_W] = a_ref[0]
    mix_ref[:, Q_W:] = c_ref[0]
    y = _dot(mix_ref[...], wo_ref[...])
    _tail(y, x_ref[0], mod_ref[0], n2_ref[...], wr_ref, x1_ref, h2_ref, pt_ref)


def _out_projection(attn, conv, x, mod0, w_out_bf, norm2, wr_cat, tm):
    b, n, d = x.shape
    tail_in, out_specs, out_shape = _tail_specs(b, n, d, tm)
    return pl.pallas_call(
        _outproj_kernel,
        grid=(b, n // tm),
        in_specs=[
            pl.BlockSpec((1, tm, Q_W), lambda bi, i: (bi, i, 0)),
            pl.BlockSpec((1, tm, CONV_W), lambda bi, i: (bi, i, 0)),
            pl.BlockSpec((1, tm, d), lambda bi, i: (bi, i, 0)),
            pl.BlockSpec((1, 6, d), lambda bi, i: (bi, 0, 0)),
            pl.BlockSpec((d, d), lambda bi, i: (0, 0)),
        ] + tail_in,
        out_specs=out_specs,
        out_shape=out_shape,
        scratch_shapes=[pltpu.VMEM((tm, d), BF16)],
        compiler_params=_params("arbitrary", "arbitrary"),
        name="out_projection",
    )(attn, conv, x, mod0, w_out_bf, norm2, wr_cat)


def _gelu(x):
    return 0.5 * x * (1.0 + lax.erf(x * np.float32(1.0 / math.sqrt(2.0))))


def _cmlp_kernel(x_ref, mod_ref, n1_ref, w1_ref, vn_ref, ws_ref, bs_ref, wo_ref, n2_ref, wr_ref,
                 x1_ref, h2_ref, pt_ref, mix_ref, *, tm):
    m = mod_ref[0]
    x = x_ref[0]
    d = x.shape[-1]
    h = _rms_mod(x, n1_ref[...], m[0:1], m[1:2]).astype(BF16)
    p = _gelu(_dot(h, w1_ref[...]))
    u, v = p[:, :d], p[:, d:]
    v = v * lax.rsqrt(jnp.mean(v * v, axis=-1, keepdims=True) + NORM_EPS) * vn_ref[...]
    vb = v.astype(BF16)
    gw = d // CMLP_GROUPS
    bs = bs_ref[...]
    for c in range(tm // CHUNK):
        r0 = c * CHUNK
        for g in range(CMLP_GROUPS):
            s = _dot(ws_ref[g], vb[r0:r0 + CHUNK, g * gw:(g + 1) * gw]) + bs[:, g:g + 1]
            mix_ref[r0:r0 + CHUNK, g * gw:(g + 1) * gw] = (u[r0:r0 + CHUNK, g * gw:(g + 1) * gw] * s).astype(BF16)
    y = _dot(mix_ref[...], wo_ref[...])
    _tail(y, x, m, n2_ref[...], wr_ref, x1_ref, h2_ref, pt_ref)


def _chunk_mlp_layer(x, mod1, norm1, w1_bf, v_norm, ws_bf, bs_t, w_out_bf, norm2, wr_cat, tm):
    b, n, d = x.shape
    tail_in, out_specs, out_shape = _tail_specs(b, n, d, tm)
    return pl.pallas_call(
        functools.partial(_cmlp_kernel, tm=tm),
        grid=(b, n // tm),
        in_specs=[
            pl.BlockSpec((1, tm, d), lambda bi, i: (bi, i, 0)),
            pl.BlockSpec((1, 6, d), lambda bi, i: (bi, 0, 0)),
            pl.BlockSpec((1, d), lambda bi, i: (0, 0)),
            pl.BlockSpec((d, 2 * d), lambda bi, i: (0, 0)),
            pl.BlockSpec((1, d), lambda bi, i: (0, 0)),
            pl.BlockSpec((CMLP_GROUPS, CHUNK, CHUNK), lambda bi, i: (0, 0, 0)),
            pl.BlockSpec((CHUNK, CMLP_GROUPS), lambda bi, i: (0, 0)),
            pl.BlockSpec((d, d), lambda bi, i: (0, 0)),
        ] + tail_in,
        out_specs=out_specs,
        out_shape=out_shape,
        scratch_shapes=[pltpu.VMEM((tm, d), BF16)],
        compiler_params=_params("arbitrary", "arbitrary"),
        name="chunk_mlp",
    )(x, mod1, norm1, w1_bf, v_norm, ws_bf, bs_t, w_out_bf, norm2, wr_cat)


def _route_kernel(p_ref, sel_ref, start_ref, fits_ref, *, cap, pchunk, window):
    rows, n = p_ref.shape

    def bit_step(it, t):
        cand = t | jnp.left_shift(jnp.int32(1), 30 - it)
        cf = lax.bitcast_convert_type(cand, F32)
        cnt = jnp.sum(jnp.where(p_ref[...] >= cf, 1.0, 0.0), axis=1, keepdims=True)
        return jnp.where(cnt >= cap, cand, t)

    t = lax.fori_loop(0, 31, bit_step, jnp.zeros((rows, 1), jnp.int32))
    tf = lax.bitcast_convert_type(t, F32)
    p = p_ref[...]
    gt = p > tf
    eq = jnp.logical_and(p >= tf, jnp.logical_not(gt))
    need = cap - jnp.sum(jnp.where(gt, 1.0, 0.0), axis=1, keepdims=True)

    ri = lax.broadcasted_iota(jnp.int32, (pchunk, pchunk), 0)
    ci = lax.broadcasted_iota(jnp.int32, (pchunk, pchunk), 1)
    upper = jnp.where(ri < ci, 1.0, 0.0).astype(BF16)

    def prefix(mask):
        mf = jnp.where(mask, 1.0, 0.0)
        carry = jnp.zeros((rows, 1), F32)
        parts, before, inside = [], [], []
        for c in range(n // pchunk):
            blk = mf[:, c * pchunk:(c + 1) * pchunk]
            parts.append(_dot(blk.astype(BF16), upper) + carry)
            total = jnp.sum(blk, axis=1, keepdims=True)
            before.append(carry)
            inside.append(total)
            carry = carry + total
        return jnp.concatenate(parts, axis=1), jnp.concatenate(before, axis=1), jnp.concatenate(inside, axis=1)

    chosen = jnp.logical_or(gt, jnp.logical_and(eq, prefix(eq)[0] < need))
    rank, before, inside = prefix(chosen)
    sel_ref[...] = jnp.where(chosen, rank, -1.0)
    start = jnp.minimum(jnp.floor(before * (1.0 / ALIGN_ROWS)) * ALIGN_ROWS, float(cap - window))
    start_ref[...] = start.astype(jnp.int32)
    fits_ref[...] = jnp.where(before + inside <= start + window, 1, 0).astype(jnp.int32)


def _route(pt, cap, window):
    b, e, n = pt.shape
    rows = b * e
    pchunk = min(SUB_TOKENS, n)
    n_chunks = n // pchunk
    full = lambda cols: pl.BlockSpec((rows, cols), lambda i: (0, 0))
    sel, start, fits = pl.pallas_call(
        functools.partial(_route_kernel, cap=cap, pchunk=pchunk, window=window),
        grid=(1,),
        in_specs=[full(n)],
        out_specs=[full(n), full(n_chunks), full(n_chunks)],
        out_shape=[jax.ShapeDtypeStruct((rows, n), F32), jax.ShapeDtypeStruct((rows, n_chunks), jnp.int32),
                   jax.ShapeDtypeStruct((rows, n_chunks), jnp.int32)],
        compiler_params=_params("arbitrary"),
        name="route",
    )(pt.reshape(rows, n))
    return sel.reshape(b, e, n), start.reshape(-1), fits.reshape(-1)


def _expert_kernel(sel_ref, pt_ref, h_ref, wg_ref, wu_ref, wd_ref, y_ref, wg_s, wu_s, wd_s, *, cap):
    e = pl.program_id(0)

    @pl.when(pl.program_id(1) == 0)
    def _():
        wg_s[...] = wg_ref[0, 0].astype(BF16)
        wu_s[...] = wu_ref[0, 0].astype(BF16)
        wd_s[...] = wd_ref[0, 0].astype(BF16)

    sel = sel_ref[0, pl.ds(e, 1), :]
    prob = pt_ref[0, pl.ds(e, 1), :]
    n = sel.shape[-1]
    rank = lax.broadcasted_iota(jnp.int32, (cap, n), 0).astype(F32)
    hit = sel == rank
    onehot = jnp.where(hit, 1.0, 0.0).astype(BF16)
    vals = jnp.sum(jnp.where(hit, prob, 0.0), axis=1, keepdims=True)
    xs = _dot(onehot, h_ref[0]).astype(BF16)
    a = _dot(xs, wg_s[...])
    bm = _dot(xs, wu_s[...])
    hid = (a * jax.nn.sigmoid(a) * bm).astype(BF16)
    y_ref[0, 0] = (_dot(hid, wd_s[...]) * vals).astype(BF16)


def _expert_ffn(sel, pt, h2, w_gate, w_up, w_down, layer, cap):
    b, n, d = h2.shape
    hdim = w_gate.shape[-1]
    w_spec = lambda rows, cols: pl.BlockSpec((1, 1, rows, cols), lambda e, bi: (layer, e, 0, 0))
    return pl.pallas_call(
        functools.partial(_expert_kernel, cap=cap),
        grid=(N_EXPERTS, b),
        in_specs=[
            pl.BlockSpec((1, N_EXPERTS, n), lambda e, bi: (bi, 0, 0)),
            pl.BlockSpec((1, N_EXPERTS, n), lambda e, bi: (bi, 0, 0)),
            pl.BlockSpec((1, n, d), lambda e, bi: (bi, 0, 0)),
            w_spec(d, hdim), w_spec(d, hdim), w_spec(hdim, d),
        ],
        out_specs=pl.BlockSpec((1, 1, cap, d), lambda e, bi: (bi, e, 0, 0)),
        out_shape=jax.ShapeDtypeStruct((b, N_EXPERTS, cap, d), BF16),
        scratch_shapes=[pltpu.VMEM((d, hdim), BF16), pltpu.VMEM((d, hdim), BF16), pltpu.VMEM((hdim, d), BF16)],
        compiler_params=_params("arbitrary", "arbitrary"),
        name="expert_ffn",
    )(sel, pt, h2, w_gate, w_up, w_down)


def _combine_kernel(start_ref, fits_ref, sel_ref, y_ref, x_ref, mod_ref, fn_ref, o_ref, acc_ref,
                    *, cap, window, sub, n_chunks, final):
    bi, i = pl.program_id(0), pl.program_id(1)
    tm = sel_ref.shape[-1]
    n_sub = tm // sub
    per_pass = sub // window
    entry = lambda e, j: (bi * N_EXPERTS + e) * n_chunks + i * n_sub + j

    all_fit = fits_ref[entry(0, 0)]
    for e in range(N_EXPERTS):
        for j in range(n_sub):
            all_fit = jnp.minimum(all_fit, fits_ref[entry(e, j)])

    @pl.when(all_fit == 1)
    def _():
        offs = lax.broadcasted_iota(jnp.int32, (window, sub), 0)
        for j in range(n_sub):
            sel = sel_ref[0, :, j * sub:(j + 1) * sub]
            acc = jnp.zeros((sub, acc_ref.shape[1]), F32)
            for g in range(N_EXPERTS // per_pass):
                hots, ys = [], []
                for e in range(g * per_pass, (g + 1) * per_pass):
                    a = pl.multiple_of(start_ref[entry(e, j)], ALIGN_ROWS)
                    hots.append(jnp.where(sel[e:e + 1, :] == (offs + a).astype(F32), 1.0, 0.0).astype(BF16))
                    ys.append(y_ref[0, e, pl.ds(a, window), :])
                acc = acc + lax.dot_general(jnp.concatenate(hots, axis=0), jnp.concatenate(ys, axis=0),
                                            TN_DIMS, preferred_element_type=F32)
            acc_ref[j * sub:(j + 1) * sub, :] = acc

    @pl.when(all_fit != 1)
    def _():
        sel = sel_ref[0]
        rank = lax.broadcasted_iota(jnp.int32, (cap, tm), 0).astype(F32)
        acc = jnp.zeros(acc_ref.shape, F32)
        for e in range(N_EXPERTS):
            onehot = jnp.where(sel[e:e + 1, :] == rank, 1.0, 0.0).astype(BF16)
            acc = acc + lax.dot_general(onehot, y_ref[0, e], TN_DIMS, preferred_element_type=F32)
        acc_ref[...] = acc

    x2 = x_ref[0] + mod_ref[0][5:6] * acc_ref[...]
    if final:
        x2 = x2 * lax.rsqrt(jnp.mean(x2 * x2, axis=-1, keepdims=True) + NORM_EPS) * fn_ref[...]
    o_ref[0] = x2


def _combine(start, fits, sel, y, x1, mod_l, final_norm, cap, window, tm, final):
    b, n, d = x1.shape
    sub = min(SUB_TOKENS, n)
    kern = functools.partial(_combine_kernel, cap=cap, window=window, sub=sub, n_chunks=n // sub, final=final)
    return pl.pallas_call(
        kern,
        grid_spec=pltpu.PrefetchScalarGridSpec(
            num_scalar_prefetch=2,
            grid=(b, n // tm),
            in_specs=[
                pl.BlockSpec((1, N_EXPERTS, tm), lambda bi, i, *_: (bi, 0, i)),
                pl.BlockSpec((1, N_EXPERTS, cap, d), lambda bi, i, *_: (bi, 0, 0, 0)),
                pl.BlockSpec((1, tm, d), lambda bi, i, *_: (bi, i, 0)),
                pl.BlockSpec((1, 6, d), lambda bi, i, *_: (bi, 0, 0)),
                pl.BlockSpec((1, d), lambda bi, i, *_: (0, 0)),
            ],
            out_specs=pl.BlockSpec((1, tm, d), lambda bi, i, *_: (bi, i, 0)),
            scratch_shapes=[pltpu.VMEM((tm, d), F32)],
        ),
        out_shape=jax.ShapeDtypeStruct((b, n, d), F32),
        compiler_params=_params("arbitrary", "arbitrary"),
        name="moe_combine",
    )(start, fits, sel, y, x1, mod_l, final_norm)


def _moe(x1, h2, pt, mod_l, w_gate, w_up, w_down, final_norm, layer, tm, final):
    n = x1.shape[1]
    cap = CAPACITY_FACTOR * n // N_EXPERTS
    window = min(SLOT_WINDOW, cap)
    sel, start, fits = _route(pt, cap, window)
    y = _expert_ffn(sel, pt, h2, w_gate, w_up, w_down, layer, cap)
    return _combine(start, fits, sel, y, x1, mod_l, final_norm, cap, window, tm, final)


def kernel(x, c, ctx, c_ctx, w_mod, b_mod, norm1, norm2, even_w_in, even_lambda, even_subln, even_conv_w,
           odd_w_in, odd_v_norm, odd_w_s, odd_b_s, w_out, w_router, w_gate, w_up, w_down, final_norm):
    b, n, d = x.shape
    depth = w_mod.shape[0]
    assert d == D_MODEL and depth == 2 and n % 256 == 0
    tm = min(512, n)
    tq = min(512, n)
    tc = min(256, n)

    cc = jnp.concatenate([c, c_ctx[None, :]], axis=0)
    mod = _modulation(cc, w_mod, b_mod).reshape(depth, b + 1, 6, d)
    row = lambda a, l: a[l].reshape(1, -1)
    wr_pad = jnp.pad(w_router, ((0, 0), (0, 0), (0, LANES - N_EXPERTS)))
    wr_hi = wr_pad.astype(BF16)
    wr_lo = (wr_pad - wr_hi.astype(F32)).astype(BF16)
    wr_cat = jnp.concatenate([jnp.concatenate([wr_hi, wr_lo], axis=2),
                              jnp.concatenate([wr_hi, jnp.zeros_like(wr_lo)], axis=2)], axis=1)
    w_out_bf = w_out.astype(BF16)
    fnorm = final_norm.reshape(1, d)

    mod0, modc = mod[0, :b], mod[0, b:]
    w_in_bf = even_w_in[0].astype(BF16)
    q, k, vt, conv = _in_projection(x, mod0, row(norm1, 0), w_in_bf, even_conv_w[0], tm)
    kc, vct = _context_kv(ctx, modc, row(norm1, 0), w_in_bf)
    lam_init = 0.8 - 0.6 * math.exp(-0.3 * 0)
    attn = _attention(q, kc, k, vct, vt, even_lambda[0], even_subln[0].reshape(-1, 1), lam_init, tq)
    x1, h2, pt = _out_projection(attn, conv, x, mod0, w_out_bf[0], row(norm2, 0), wr_cat[0], tm)
    x = _moe(x1, h2, pt, mod0, w_gate, w_up, w_down, fnorm, 0, tm, final=False)

    mod1 = mod[1, :b]
    x1, h2, pt = _chunk_mlp_layer(x, mod1, row(norm1, 1), odd_w_in[0].astype(BF16), row(odd_v_norm, 0),
                                  odd_w_s[0].astype(BF16), odd_b_s[0].T, w_out_bf[1], row(norm2, 1),
                                  wr_cat[1], tc)
    return _moe(x1, h2, pt, mod1, w_gate, w_up, w_down, fnorm, 1, tm, final=True)
```

```python
import functools
import math

import numpy as np
import jax
import jax.numpy as jnp
from jax import lax
from jax.experimental import pallas as pl
from jax.experimental.pallas import tpu as pltpu

F32 = jnp.float32
BF16 = jnp.bfloat16

D_MODEL = 1024
ATTN_HEADS = 4
QK_DIM = 64
V_DIM = 128
Q_W = ATTN_HEADS * 2 * QK_DIM
CONV_W = 512
EVEN_IN = 3072
N_EXPERTS = 16
CAPACITY_FACTOR = 2
GRID_W = 64
CHUNK = 128
CMLP_GROUPS = 4
ROPE_THETA = 10000.0
NORM_EPS = 1e-6
LANES = 128
HALO_ROWS = 8
VMEM_LIMIT = 56 * 1024 * 1024
PV_ROWS = V_DIM + 16
SAFE_SHIFT = 60.0
ALIGN_ROWS = 16
SUB_TOKENS = 256
SLOT_WINDOW = 64

NT_DIMS = (((1,), (1,)), ((), ()))
TN_DIMS = (((0,), (0,)), ((), ()))


def _dot(a, b):
    return jnp.dot(a, b, preferred_element_type=F32)


def _params(*sem, flags=None):
    return pltpu.CompilerParams(dimension_semantics=sem, vmem_limit_bytes=VMEM_LIMIT, flags=flags)


def _rms_mod(x, g, shift, scale):
    y = x * lax.rsqrt(jnp.mean(x * x, axis=-1, keepdims=True) + NORM_EPS)
    return (y * g) * (1.0 + scale) + shift


def _split_bf16(x):
    hi = x.astype(BF16)
    lo = (x - hi.astype(F32)).astype(BF16)
    return hi, lo


def _mod_kernel(c_ref, w_ref, b_ref, o_ref):
    c = c_ref[...]
    a_hi, a_lo = _split_bf16(c * jax.nn.sigmoid(c))
    w_hi, w_lo = _split_bf16(w_ref[0])
    o_ref[0] = _dot(a_hi, w_hi) + _dot(a_lo, w_hi) + _dot(a_hi, w_lo) + b_ref[0]


def _modulation(cc, w_mod, b_mod):
    depth, d, six_d = w_mod.shape
    rows = cc.shape[0]
    tn = 1536
    return pl.pallas_call(
        _mod_kernel,
        grid=(depth, six_d // tn),
        in_specs=[
            pl.BlockSpec((rows, d), lambda l, j: (0, 0)),
            pl.BlockSpec((1, d, tn), lambda l, j: (l, 0, j)),
            pl.BlockSpec((1, 1, tn), lambda l, j: (l, 0, j)),
        ],
        out_specs=pl.BlockSpec((1, rows, tn), lambda l, j: (l, 0, j)),
        out_shape=jax.ShapeDtypeStruct((depth, rows, six_d), F32),
        compiler_params=_params("arbitrary", "arbitrary"),
        name="modulation",
    )(cc, w_mod, b_mod.reshape(depth, 1, six_d))


def _inproj_kernel(x_ref, xp_ref, xn_ref, mod_ref, n1_ref, w_ref, cos_ref, sa_ref, sb_ref, cw_ref,
                   q_ref, k_ref, vt_ref, c_ref, *, tm, n_tiles):
    i = pl.program_id(1)
    m = mod_ref[0]
    shift, scale = m[0:1], m[1:2]
    g = n1_ref[...]
    h = _rms_mod(x_ref[0], g, shift, scale).astype(BF16)

    cos, sa, sb = cos_ref[...], sa_ref[...], sb_ref[...]
    for col0, out_ref, qscale in ((0, q_ref, QK_DIM ** -0.5 * math.log2(math.e)), (Q_W, k_ref, 1.0)):
        p = _dot(h, w_ref[:, col0:col0 + Q_W])
        for j in range(Q_W // LANES):
            pj = p[:, j * LANES:(j + 1) * LANES]
            r = pj * cos + pltpu.roll(pj, LANES - 16, 1) * sa + pltpu.roll(pj, 16, 1) * sb
            out_ref[0, :, j * LANES:(j + 1) * LANES] = (r * qscale).astype(BF16)

    vt_ref[0] = _dot(h, w_ref[:, 2 * Q_W:2 * Q_W + 512]).T.astype(BF16)

    pc = _dot(h, w_ref[:, 2 * Q_W + 512:])
    gb = pc[:, :CONV_W]
    u = pc[:, CONV_W:2 * CONV_W] * pc[:, 2 * CONV_W:]
    xh = jnp.concatenate([xp_ref[0], xn_ref[0]], axis=0)
    hh = _rms_mod(xh, g, shift, scale).astype(BF16)
    ph = _dot(hh, w_ref[:, 2 * Q_W + 512 + CONV_W:])
    uh = ph[:, :CONV_W] * ph[:, CONV_W:]
    u_before = jnp.where(i > 0, uh[HALO_ROWS - 1:HALO_ROWS], 0.0)
    u_after = jnp.where(i < n_tiles - 1, uh[HALO_ROWS:HALO_ROWS + 1], 0.0)
    row = lax.broadcasted_iota(jnp.int32, (tm, 1), 0)
    u_prev = jnp.where(row == 0, u_before, pltpu.roll(u, 1, 0))
    u_next = jnp.where(row == tm - 1, u_after, pltpu.roll(u, tm - 1, 0))
    cw = cw_ref[...]
    conv = cw[0:1] * u_prev + cw[1:2] * u + cw[2:3] * u_next
    c_ref[0] = (gb * conv).astype(BF16)


def _rope_tables(n):
    rows = n // GRID_W
    row = jnp.repeat(jnp.arange(rows), GRID_W).astype(F32)
    col = jnp.tile(jnp.arange(GRID_W), rows).astype(F32)
    half = QK_DIM // 2
    inv = 1.0 / (ROPE_THETA ** (jnp.arange(0, half, 2, dtype=F32) / half))
    ang_r = row[:, None] * inv
    ang_c = col[:, None] * inv
    ang = jnp.concatenate([ang_r, ang_r, ang_c, ang_c], axis=-1)
    cos, sin = jnp.cos(ang), jnp.sin(ang)
    first_half = (jnp.arange(QK_DIM) % 32) < 16
    sin_a = jnp.where(first_half, -sin, 0.0)
    sin_b = jnp.where(first_half, 0.0, sin)
    tile2 = lambda t: jnp.concatenate([t, t], axis=-1)
    return tile2(cos), tile2(sin_a), tile2(sin_b)


def _in_projection(x, mod0, norm1, w_in_bf, conv_w, tm):
    b, n, d = x.shape
    n_tiles = n // tm
    hb = tm // HALO_ROWS
    n_hblocks = n // HALO_ROWS
    cos, sa, sb = _rope_tables(n)
    out = jax.ShapeDtypeStruct((b, n, Q_W), BF16)
    row_spec = pl.BlockSpec((1, tm, Q_W), lambda bi, i: (bi, i, 0))
    tab_spec = pl.BlockSpec((tm, LANES), lambda bi, i: (i, 0))
    return pl.pallas_call(
        functools.partial(_inproj_kernel, tm=tm, n_tiles=n_tiles),
        grid=(b, n_tiles),
        in_specs=[
            pl.BlockSpec((1, tm, d), lambda bi, i: (bi, i, 0)),
            pl.BlockSpec((1, HALO_ROWS, d), lambda bi, i: (bi, jnp.maximum(i * hb - 1, 0), 0)),
            pl.BlockSpec((1, HALO_ROWS, d), lambda bi, i: (bi, jnp.minimum((i + 1) * hb, n_hblocks - 1), 0)),
            pl.BlockSpec((1, 6, d), lambda bi, i: (bi, 0, 0)),
            pl.BlockSpec((1, d), lambda bi, i: (0, 0)),
            pl.BlockSpec((d, EVEN_IN), lambda bi, i: (0, 0)),
            tab_spec, tab_spec, tab_spec,
            pl.BlockSpec((3, CONV_W), lambda bi, i: (0, 0)),
        ],
        out_specs=[row_spec, row_spec, pl.BlockSpec((1, Q_W, tm), lambda bi, i: (bi, 0, i)), row_spec],
        out_shape=[out, out, jax.ShapeDtypeStruct((b, Q_W, n), BF16), out],
        compiler_params=_params("arbitrary", "arbitrary"),
        name="in_projection",
    )(x, x, x, mod0, norm1, w_in_bf, cos, sa, sb, conv_w)


def _ctxkv_kernel(x_ref, mod_ref, n1_ref, wk_ref, wv_ref, k_ref, vt_ref):
    m = mod_ref[0]
    h = _rms_mod(x_ref[0], n1_ref[...], m[0:1], m[1:2]).astype(BF16)
    k_ref[0] = _dot(h, wk_ref[...]).astype(BF16)
    vt_ref[0] = _dot(h, wv_ref[...]).T.astype(BF16)


def _context_kv(ctx, modc, norm1, w_in_bf):
    b, m, d = ctx.shape
    out = jax.ShapeDtypeStruct((b, m, Q_W), BF16)
    return pl.pallas_call(
        _ctxkv_kernel,
        grid=(b,),
        in_specs=[
            pl.BlockSpec((1, m, d), lambda bi: (bi, 0, 0)),
            pl.BlockSpec((1, 6, d), lambda bi: (0, 0, 0)),
            pl.BlockSpec((1, d), lambda bi: (0, 0)),
            pl.BlockSpec((d, Q_W), lambda bi: (0, 1)),
            pl.BlockSpec((d, Q_W), lambda bi: (0, 2)),
        ],
        out_specs=[pl.BlockSpec((1, m, Q_W), lambda bi: (bi, 0, 0)), pl.BlockSpec((1, Q_W, m), lambda bi: (bi, 0, 0))],
        out_shape=[out, jax.ShapeDtypeStruct((b, Q_W, m), BF16)],
        compiler_params=_params("arbitrary"),
        name="context_kv",
    )(ctx, modc, norm1, w_in_bf, w_in_bf)


def _attn_kernel(lam_ref, q_ref, kc_ref, k_ref, vct_ref, vt_ref, sub_ref, o_ref,
                 kx_ref, vxt_ref, kn_ref, s_ref, e_ref, ox_ref, *, lam_init, m_ctx):
    nk = kx_ref.shape[0]

    @pl.when(pl.program_id(2) == 0)
    def _():
        kx_ref[:m_ctx, :] = kc_ref[0]
        kx_ref[m_ctx:, :] = k_ref[0]
        vxt_ref[:V_DIM, :m_ctx] = vct_ref[0]
        vxt_ref[:V_DIM, m_ctx:] = vt_ref[0]
        row = lax.broadcasted_iota(jnp.int32, (PV_ROWS - V_DIM, nk), 0)
        vxt_ref[V_DIM:, :] = jnp.where(row == 0, 1.0, 0.0).astype(BF16)
        kf = kx_ref[...].astype(F32)
        ksq = kf * kf
        klane = lax.broadcasted_iota(jnp.int32, ksq.shape, 1)
        for mi, in_map in enumerate((klane < QK_DIM, klane >= QK_DIM)):
            norms = jnp.sum(jnp.where(in_map, ksq, 0.0), axis=1, keepdims=True)
            kn_ref[mi:mi + 1, :] = jnp.broadcast_to(jnp.max(norms, axis=0, keepdims=True), (1, LANES))

    lp = lam_ref[...]
    lam = (jnp.exp(jnp.sum(lp[0:1] * lp[1:2], keepdims=True))
           - jnp.exp(jnp.sum(lp[2:3] * lp[3:4], keepdims=True)) + lam_init)
    q = q_ref[0]
    lane = lax.broadcasted_iota(jnp.int32, q.shape, 1)
    zero = jnp.zeros_like(q)
    in_maps = (lane < QK_DIM, lane >= QK_DIM)
    qms = [jnp.where(in_map, q, zero) for in_map in in_maps]

    qf = q.astype(F32)
    qsq = (qf * qf).astype(BF16)
    ones_rows = jnp.ones((HALO_ROWS, LANES), BF16)
    shifts = []
    for mi, in_map in enumerate(in_maps):
        qq = lax.dot_general(ones_rows, jnp.where(in_map, qsq, zero), NT_DIMS,
                             preferred_element_type=F32)[0:1]
        shifts.append(jnp.sqrt(qq * kn_ref[mi:mi + 1, 0:1]) * 1.05)
    bound_is_safe = jnp.maximum(jnp.max(shifts[0]), jnp.max(shifts[1])) <= SAFE_SHIFT

    @pl.when(bound_is_safe)
    def _():
        for mi in range(2):
            st = lax.dot_general(kx_ref[...], qms[mi], NT_DIMS, preferred_element_type=F32)
            e_ref[mi] = jnp.exp2(st - shifts[mi]).astype(BF16)
            ox_ref[mi] = _dot(vxt_ref[...], e_ref[mi])

    @pl.when(jnp.logical_not(bound_is_safe))
    def _():
        for mi in range(2):
            s_ref[mi] = lax.dot_general(kx_ref[...], qms[mi], NT_DIMS, preferred_element_type=F32)
        for mi in range(2):
            e_ref[mi] = jnp.exp2(s_ref[mi] - jnp.max(s_ref[mi], axis=0, keepdims=True)).astype(BF16)
            ox_ref[mi] = _dot(vxt_ref[...], e_ref[mi])

    o1, l1 = ox_ref[0, :V_DIM, :], ox_ref[0, V_DIM:V_DIM + 1, :]
    o2, l2 = ox_ref[1, :V_DIM, :], ox_ref[1, V_DIM:V_DIM + 1, :]
    ot = o1 * (1.0 / l1) - o2 * (lam / l2)
    ot = ot * lax.rsqrt(jnp.mean(ot * ot, axis=0, keepdims=True) + NORM_EPS)
    o_ref[0] = (ot * sub_ref[...] * (1.0 - lam_init)).T.astype(BF16)


def _attention(q, kc, k, vct, vt, lam_p, subln, lam_init, tq):
    b, n, _ = q.shape
    m_ctx = kc.shape[1]
    head_rows = lambda rows: pl.BlockSpec((1, rows, V_DIM), lambda bi, h, i: (bi, 0, h))
    head_cols = lambda cols: pl.BlockSpec((1, V_DIM, cols), lambda bi, h, i: (bi, h, 0))
    return pl.pallas_call(
        functools.partial(_attn_kernel, lam_init=lam_init, m_ctx=m_ctx),
        grid=(b, ATTN_HEADS, n // tq),
        in_specs=[
            pl.BlockSpec((4, QK_DIM), lambda bi, h, i: (0, 0)),
            pl.BlockSpec((1, tq, V_DIM), lambda bi, h, i: (bi, i, h)),
            head_rows(m_ctx), head_rows(n), head_cols(m_ctx), head_cols(n),
            pl.BlockSpec((V_DIM, 1), lambda bi, h, i: (0, 0)),
        ],
        out_specs=pl.BlockSpec((1, tq, V_DIM), lambda bi, h, i: (bi, i, h)),
        out_shape=jax.ShapeDtypeStruct((b, n, Q_W), BF16),
        scratch_shapes=[pltpu.VMEM((m_ctx + n, V_DIM), BF16), pltpu.VMEM((PV_ROWS, m_ctx + n), BF16),
                        pltpu.VMEM((HALO_ROWS, LANES), F32),
                        pltpu.VMEM((2, m_ctx + n, tq), F32), pltpu.VMEM((2, m_ctx + n, tq), BF16),
                        pltpu.VMEM((2, PV_ROWS, tq), F32)],
        compiler_params=_params("arbitrary", "arbitrary", "arbitrary"),
        name="diff_attention",
    )(lam_p, q, kc, k, vct, vt, subln)


def _tail(y, x, m, n2, wr_ref, x1_ref, h2_ref, pt_ref):
    x1 = x + m[2:3] * y
    x1_ref[0] = x1
    h2 = _rms_mod(x1, n2, m[3:4], m[4:5])
    h_hi, h_lo = _split_bf16(h2)
    h2_ref[0] = h_hi
    hcat = jnp.concatenate([h_hi, h_lo], axis=1)
    half = hcat.shape[0] // 2
    parts = [_dot(hcat[r0:r0 + half], wr_ref[...]) for r0 in (0, half)]
    prod = jnp.concatenate(parts, axis=0)
    logits = (prod[:, :LANES] + prod[:, LANES:]).T[:N_EXPERTS]
    ex = jnp.exp(logits - jnp.max(logits, axis=0, keepdims=True))
    pt_ref[0] = ex / jnp.sum(ex, axis=0, keepdims=True)


def _tail_specs(b, n, d, tm):
    in_specs = [
        pl.BlockSpec((1, d), lambda bi, i: (0, 0)),
        pl.BlockSpec((2 * d, 2 * LANES), lambda bi, i: (0, 0)),
    ]
    out_specs = [
        pl.BlockSpec((1, tm, d), lambda bi, i: (bi, i, 0)),
        pl.BlockSpec((1, tm, d), lambda bi, i: (bi, i, 0)),
        pl.BlockSpec((1, N_EXPERTS, tm), lambda bi, i: (bi, 0, i)),
    ]
    out_shape = [
        jax.ShapeDtypeStruct((b, n, d), F32),
        jax.ShapeDtypeStruct((b, n, d), BF16),
        jax.ShapeDtypeStruct((b, N_EXPERTS, n), F32),
    ]
    return in_specs, out_specs, out_shape


def _outproj_kernel(a_ref, c_ref, x_ref, mod_ref, wo_ref, n2_ref, wr_ref,
                    x1_ref, h2_ref, pt_ref, mix_ref):
    mix_ref[:, :Q_W] = a_ref[0]
    mix_ref[:, Q_W:] = c_ref[0]
    y = _dot(mix_ref[...], wo_ref[...])
    _tail(y, x_ref[0], mod_ref[0], n2_ref[...], wr_ref, x1_ref, h2_ref, pt_ref)


def _out_projection(attn, conv, x, mod0, w_out_bf, norm2, wr_cat, tm):
    b, n, d = x.shape
    tail_in, out_specs, out_shape = _tail_specs(b, n, d, tm)
    return pl.pallas_call(
        _outproj_kernel,
        grid=(b, n // tm),
        in_specs=[
            pl.BlockSpec((1, tm, Q_W), lambda bi, i: (bi, i, 0)),
            pl.BlockSpec((1, tm, CONV_W), lambda bi, i: (bi, i, 0)),
            pl.BlockSpec((1, tm, d), lambda bi, i: (bi, i, 0)),
            pl.BlockSpec((1, 6, d), lambda bi, i: (bi, 0, 0)),
            pl.BlockSpec((d, d), lambda bi, i: (0, 0)),
        ] + tail_in,
        out_specs=out_specs,
        out_shape=out_shape,
        scratch_shapes=[pltpu.VMEM((tm, d), BF16)],
        compiler_params=_params("arbitrary", "arbitrary"),
        name="out_projection",
    )(attn, conv, x, mod0, w_out_bf, norm2, wr_cat)


def _gelu(x):
    return 0.5 * x * (1.0 + lax.erf(x * np.float32(1.0 / math.sqrt(2.0))))


def _cmlp_kernel(x_ref, mod_ref, n1_ref, w1_ref, vn_ref, ws_ref, bs_ref, wo_ref, n2_ref, wr_ref,
                 x1_ref, h2_ref, pt_ref, mix_ref, *, tm):
    m = mod_ref[0]
    x = x_ref[0]
    d = x.shape[-1]
    h = _rms_mod(x, n1_ref[...], m[0:1], m[1:2]).astype(BF16)
    p = _gelu(_dot(h, w1_ref[...]))
    u, v = p[:, :d], p[:, d:]
    v = v * lax.rsqrt(jnp.mean(v * v, axis=-1, keepdims=True) + NORM_EPS) * vn_ref[...]
    vb = v.astype(BF16)
    gw = d // CMLP_GROUPS
    bs = bs_ref[...]
    for c in range(tm // CHUNK):
        r0 = c * CHUNK
        for g in range(CMLP_GROUPS):
            s = _dot(ws_ref[g], vb[r0:r0 + CHUNK, g * gw:(g + 1) * gw]) + bs[:, g:g + 1]
            mix_ref[r0:r0 + CHUNK, g * gw:(g + 1) * gw] = (u[r0:r0 + CHUNK, g * gw:(g + 1) * gw] * s).astype(BF16)
    y = _dot(mix_ref[...], wo_ref[...])
    _tail(y, x, m, n2_ref[...], wr_ref, x1_ref, h2_ref, pt_ref)


def _chunk_mlp_layer(x, mod1, norm1, w1_bf, v_norm, ws_bf, bs_t, w_out_bf, norm2, wr_cat, tm):
    b, n, d = x.shape
    tail_in, out_specs, out_shape = _tail_specs(b, n, d, tm)
    return pl.pallas_call(
        functools.partial(_cmlp_kernel, tm=tm),
        grid=(b, n // tm),
        in_specs=[
            pl.BlockSpec((1, tm, d), lambda bi, i: (bi, i, 0)),
            pl.BlockSpec((1, 6, d), lambda bi, i: (bi, 0, 0)),
            pl.BlockSpec((1, d), lambda bi, i: (0, 0)),
            pl.BlockSpec((d, 2 * d), lambda bi, i: (0, 0)),
            pl.BlockSpec((1, d), lambda bi, i: (0, 0)),
            pl.BlockSpec((CMLP_GROUPS, CHUNK, CHUNK), lambda bi, i: (0, 0, 0)),
            pl.BlockSpec((CHUNK, CMLP_GROUPS), lambda bi, i: (0, 0)),
            pl.BlockSpec((d, d), lambda bi, i: (0, 0)),
        ] + tail_in,
        out_specs=out_specs,
        out_shape=out_shape,
        scratch_shapes=[pltpu.VMEM((tm, d), BF16)],
        compiler_params=_params("arbitrary", "arbitrary"),
        name="chunk_mlp",
    )(x, mod1, norm1, w1_bf, v_norm, ws_bf, bs_t, w_out_bf, norm2, wr_cat)


def _route_kernel(p_ref, sel_ref, start_ref, fits_ref, *, cap, pchunk, window):
    rows, n = p_ref.shape

    def bit_step(it, t):
        cand = t | jnp.left_shift(jnp.int32(1), 30 - it)
        cf = lax.bitcast_convert_type(cand, F32)
        cnt = jnp.sum(jnp.where(p_ref[...] >= cf, 1.0, 0.0), axis=1, keepdims=True)
        return jnp.where(cnt >= cap, cand, t)

    t = lax.fori_loop(0, 31, bit_step, jnp.zeros((rows, 1), jnp.int32))
    tf = lax.bitcast_convert_type(t, F32)
    p = p_ref[...]
    gt = p > tf
    eq = jnp.logical_and(p >= tf, jnp.logical_not(gt))
    need = cap - jnp.sum(jnp.where(gt, 1.0, 0.0), axis=1, keepdims=True)

    ri = lax.broadcasted_iota(jnp.int32, (pchunk, pchunk), 0)
    ci = lax.broadcasted_iota(jnp.int32, (pchunk, pchunk), 1)
    upper = jnp.where(ri < ci, 1.0, 0.0).astype(BF16)

    def prefix(mask):
        mf = jnp.where(mask, 1.0, 0.0)
        carry = jnp.zeros((rows, 1), F32)
        parts, before, inside = [], [], []
        for c in range(n // pchunk):
            blk = mf[:, c * pchunk:(c + 1) * pchunk]
            parts.append(_dot(blk.astype(BF16), upper) + carry)
            total = jnp.sum(blk, axis=1, keepdims=True)
            before.append(carry)
            inside.append(total)
            carry = carry + total
        return jnp.concatenate(parts, axis=1), jnp.concatenate(before, axis=1), jnp.concatenate(inside, axis=1)

    chosen = jnp.logical_or(gt, jnp.logical_and(eq, prefix(eq)[0] < need))
    rank, before, inside = prefix(chosen)
    sel_ref[...] = jnp.where(chosen, rank, -1.0)
    start = jnp.minimum(jnp.floor(before * (1.0 / ALIGN_ROWS)) * ALIGN_ROWS, float(cap - window))
    start_ref[...] = start.astype(jnp.int32)
    fits_ref[...] = jnp.where(before + inside <= start + window, 1, 0).astype(jnp.int32)


def _route(pt, cap, window):
    b, e, n = pt.shape
    rows = b * e
    pchunk = min(SUB_TOKENS, n)
    n_chunks = n // pchunk
    full = lambda cols: pl.BlockSpec((rows, cols), lambda i: (0, 0))
    sel, start, fits = pl.pallas_call(
        functools.partial(_route_kernel, cap=cap, pchunk=pchunk, window=window),
        grid=(1,),
        in_specs=[full(n)],
        out_specs=[full(n), full(n_chunks), full(n_chunks)],
        out_shape=[jax.ShapeDtypeStruct((rows, n), F32), jax.ShapeDtypeStruct((rows, n_chunks), jnp.int32),
                   jax.ShapeDtypeStruct((rows, n_chunks), jnp.int32)],
        compiler_params=_params("arbitrary"),
        name="route",
    )(pt.reshape(rows, n))
    return sel.reshape(b, e, n), start.reshape(-1), fits.reshape(-1)


def _gather_kernel(start_ref, fits_ref, sel_ref, h_ref, xs_ref, *, cap, window, sub, n_chunks):
    bi = pl.program_id(0)
    per_pass = sub // window
    entry = lambda e, c: (bi * N_EXPERTS + e) * n_chunks + c

    all_fit = fits_ref[entry(0, 0)]
    for e in range(N_EXPERTS):
        for c in range(n_chunks):
            all_fit = jnp.minimum(all_fit, fits_ref[entry(e, c)])

    @pl.when(all_fit == 1)
    def _():
        xs_ref[...] = jnp.zeros(xs_ref.shape, BF16)
        offs = lax.broadcasted_iota(jnp.int32, (window, sub), 0)
        for c in range(n_chunks):
            sel = sel_ref[0, :, c * sub:(c + 1) * sub]
            hc = h_ref[0, c * sub:(c + 1) * sub, :]
            for g in range(N_EXPERTS // per_pass):
                group = range(g * per_pass, (g + 1) * per_pass)
                starts = [pl.multiple_of(start_ref[entry(e, c)], ALIGN_ROWS) for e in group]
                hots = [jnp.where(sel[e:e + 1, :] == (offs + a).astype(F32), 1.0, 0.0).astype(BF16)
                        for e, a in zip(group, starts)]
                z = _dot(jnp.concatenate(hots, axis=0), hc).astype(BF16)
                for k, (e, a) in enumerate(zip(group, starts)):
                    xs_ref[0, e, pl.ds(a, window), :] += z[k * window:(k + 1) * window]

    @pl.when(all_fit != 1)
    def _():
        n = sel_ref.shape[-1]
        rank = lax.broadcasted_iota(jnp.int32, (cap, n), 0).astype(F32)
        for e in range(N_EXPERTS):
            onehot = jnp.where(sel_ref[0, e:e + 1, :] == rank, 1.0, 0.0).astype(BF16)
            xs_ref[0, e] = _dot(onehot, h_ref[0]).astype(BF16)


def _gather(start, fits, sel, h2, cap, window):
    b, n, d = h2.shape
    sub = min(SUB_TOKENS, n)
    kern = functools.partial(_gather_kernel, cap=cap, window=window, sub=sub, n_chunks=n // sub)
    return pl.pallas_call(
        kern,
        grid_spec=pltpu.PrefetchScalarGridSpec(
            num_scalar_prefetch=2,
            grid=(b,),
            in_specs=[
                pl.BlockSpec((1, N_EXPERTS, n), lambda bi, *_: (bi, 0, 0)),
                pl.BlockSpec((1, n, d), lambda bi, *_: (bi, 0, 0)),
            ],
            out_specs=pl.BlockSpec((1, N_EXPERTS, cap, d), lambda bi, *_: (bi, 0, 0, 0)),
        ),
        out_shape=jax.ShapeDtypeStruct((b, N_EXPERTS, cap, d), BF16),
        compiler_params=_params("arbitrary"),
        name="moe_gather",
    )(start, fits, sel, h2)


def _expert_kernel(sel_ref, pt_ref, xs_ref, wg_ref, wu_ref, wd_ref, y_ref, wg_s, wu_s, wd_s, *, cap):
    e = pl.program_id(0)

    @pl.when(pl.program_id(1) == 0)
    def _():
        wg_s[...] = wg_ref[0, 0].astype(BF16)
        wu_s[...] = wu_ref[0, 0].astype(BF16)
        wd_s[...] = wd_ref[0, 0].astype(BF16)

    bb, _, n = sel_ref.shape
    d = xs_ref.shape[-1]
    rank = lax.broadcasted_iota(jnp.int32, (cap, n), 0).astype(F32)
    vals = []
    for j in range(bb):
        sel = sel_ref[j, pl.ds(e, 1), :]
        prob = pt_ref[j, pl.ds(e, 1), :]
        vals.append(jnp.sum(jnp.where(sel == rank, prob, 0.0), axis=1, keepdims=True))
    xs = xs_ref[...].reshape(bb * cap, d)
    a = _dot(xs, wg_s[...])
    bm = _dot(xs, wu_s[...])
    hid = (a * jax.nn.sigmoid(a) * bm).astype(BF16)
    y = _dot(hid, wd_s[...]) * jnp.concatenate(vals, axis=0)
    y_ref[...] = y.astype(BF16).reshape(y_ref.shape)


def _expert_ffn(sel, pt, xs, w_gate, w_up, w_down, layer, cap):
    b, _, _, d = xs.shape
    n = sel.shape[-1]
    hdim = w_gate.shape[-1]
    bb = next(k for k in (4, 2, 1) if b % k == 0)
    w_spec = lambda rows, cols: pl.BlockSpec((1, 1, rows, cols), lambda e, bi: (layer, e, 0, 0))
    return pl.pallas_call(
        functools.partial(_expert_kernel, cap=cap),
        grid=(N_EXPERTS, b // bb),
        in_specs=[
            pl.BlockSpec((bb, N_EXPERTS, n), lambda e, bi: (bi, 0, 0)),
            pl.BlockSpec((bb, N_EXPERTS, n), lambda e, bi: (bi, 0, 0)),
            pl.BlockSpec((bb, 1, cap, d), lambda e, bi: (bi, e, 0, 0)),
            w_spec(d, hdim), w_spec(d, hdim), w_spec(hdim, d),
        ],
        out_specs=pl.BlockSpec((bb, 1, cap, d), lambda e, bi: (bi, e, 0, 0)),
        out_shape=jax.ShapeDtypeStruct((b, N_EXPERTS, cap, d), BF16),
        scratch_shapes=[pltpu.VMEM((d, hdim), BF16), pltpu.VMEM((d, hdim), BF16), pltpu.VMEM((hdim, d), BF16)],
        compiler_params=_params("arbitrary", "arbitrary"),
        name="expert_ffn",
    )(sel, pt, xs, w_gate, w_up, w_down)


def _combine_kernel(start_ref, fits_ref, sel_ref, y_ref, x_ref, mod_ref, fn_ref, o_ref, acc_ref,
                    *, cap, window, sub, n_chunks, final):
    bi, i = pl.program_id(0), pl.program_id(1)
    tm = sel_ref.shape[-1]
    n_sub = tm // sub
    per_pass = sub // window
    entry = lambda e, j: (bi * N_EXPERTS + e) * n_chunks + i * n_sub + j

    all_fit = fits_ref[entry(0, 0)]
    for e in range(N_EXPERTS):
        for j in range(n_sub):
            all_fit = jnp.minimum(all_fit, fits_ref[entry(e, j)])

    @pl.when(all_fit == 1)
    def _():
        offs = lax.broadcasted_iota(jnp.int32, (window, sub), 0)
        for j in range(n_sub):
            sel = sel_ref[0, :, j * sub:(j + 1) * sub]
            acc = jnp.zeros((sub, acc_ref.shape[1]), F32)
            for g in range(N_EXPERTS // per_pass):
                hots, ys = [], []
                for e in range(g * per_pass, (g + 1) * per_pass):
                    a = pl.multiple_of(start_ref[entry(e, j)], ALIGN_ROWS)
                    hots.append(jnp.where(sel[e:e + 1, :] == (offs + a).astype(F32), 1.0, 0.0).astype(BF16))
                    ys.append(y_ref[0, e, pl.ds(a, window), :])
                acc = acc + lax.dot_general(jnp.concatenate(hots, axis=0), jnp.concatenate(ys, axis=0),
                                            TN_DIMS, preferred_element_type=F32)
            acc_ref[j * sub:(j + 1) * sub, :] = acc

    @pl.when(all_fit != 1)
    def _():
        sel = sel_ref[0]
        rank = lax.broadcasted_iota(jnp.int32, (cap, tm), 0).astype(F32)
        acc = jnp.zeros(acc_ref.shape, F32)
        for e in range(N_EXPERTS):
            onehot = jnp.where(sel[e:e + 1, :] == rank, 1.0, 0.0).astype(BF16)
            acc = acc + lax.dot_general(onehot, y_ref[0, e], TN_DIMS, preferred_element_type=F32)
        acc_ref[...] = acc

    x2 = x_ref[0] + mod_ref[0][5:6] * acc_ref[...]
    if final:
        x2 = x2 * lax.rsqrt(jnp.mean(x2 * x2, axis=-1, keepdims=True) + NORM_EPS) * fn_ref[...]
    o_ref[0] = x2


def _combine(start, fits, sel, y, x1, mod_l, final_norm, cap, window, tm, final):
    b, n, d = x1.shape
    sub = min(SUB_TOKENS, n)
    kern = functools.partial(_combine_kernel, cap=cap, window=window, sub=sub, n_chunks=n // sub, final=final)
    return pl.pallas_call(
        kern,
        grid_spec=pltpu.PrefetchScalarGridSpec(
            num_scalar_prefetch=2,
            grid=(b, n // tm),
            in_specs=[
                pl.BlockSpec((1, N_EXPERTS, tm), lambda bi, i, *_: (bi, 0, i)),
                pl.BlockSpec((1, N_EXPERTS, cap, d), lambda bi, i, *_: (bi, 0, 0, 0)),
                pl.BlockSpec((1, tm, d), lambda bi, i, *_: (bi, i, 0)),
                pl.BlockSpec((1, 6, d), lambda bi, i, *_: (bi, 0, 0)),
                pl.BlockSpec((1, d), lambda bi, i, *_: (0, 0)),
            ],
            out_specs=pl.BlockSpec((1, tm, d), lambda bi, i, *_: (bi, i, 0)),
            scratch_shapes=[pltpu.VMEM((tm, d), F32)],
        ),
        out_shape=jax.ShapeDtypeStruct((b, n, d), F32),
        compiler_params=_params("arbitrary", "arbitrary"),
        name="moe_combine",
    )(start, fits, sel, y, x1, mod_l, final_norm)


def _moe(x1, h2, pt, mod_l, w_gate, w_up, w_down, final_norm, layer, tm, final):
    n = x1.shape[1]
    cap = CAPACITY_FACTOR * n // N_EXPERTS
    window = min(SLOT_WINDOW, cap)
    sel, start, fits = _route(pt, cap, window)
    xs = _gather(start, fits, sel, h2, cap, window)
    y = _expert_ffn(sel, pt, xs, w_gate, w_up, w_down, layer, cap)
    return _combine(start, fits, sel, y, x1, mod_l, final_norm, cap, window, tm, final)


def kernel(x, c, ctx, c_ctx, w_mod, b_mod, norm1, norm2, even_w_in, even_lambda, even_subln, even_conv_w,
           odd_w_in, odd_v_norm, odd_w_s, odd_b_s, w_out, w_router, w_gate, w_up, w_down, final_norm):
    b, n, d = x.shape
    depth = w_mod.shape[0]
    assert d == D_MODEL and depth == 2 and n % 256 == 0
    tm = min(512, n)
    tq = min(512, n)
    tc = min(256, n)

    cc = jnp.concatenate([c, c_ctx[None, :]], axis=0)
    mod = _modulation(cc, w_mod, b_mod).reshape(depth, b + 1, 6, d)
    row = lambda a, l: a[l].reshape(1, -1)
    wr_pad = jnp.pad(w_router, ((0, 0), (0, 0), (0, LANES - N_EXPERTS)))
    wr_hi = wr_pad.astype(BF16)
    wr_lo = (wr_pad - wr_hi.astype(F32)).astype(BF16)
    wr_cat = jnp.concatenate([jnp.concatenate([wr_hi, wr_lo], axis=2),
                              jnp.concatenate([wr_hi, jnp.zeros_like(wr_lo)], axis=2)], axis=1)
    w_out_bf = w_out.astype(BF16)
    fnorm = final_norm.reshape(1, d)

    mod0, modc = mod[0, :b], mod[0, b:]
    w_in_bf = even_w_in[0].astype(BF16)
    q, k, vt, conv = _in_projection(x, mod0, row(norm1, 0), w_in_bf, even_conv_w[0], tm)
    kc, vct = _context_kv(ctx, modc, row(norm1, 0), w_in_bf)
    lam_init = 0.8 - 0.6 * math.exp(-0.3 * 0)
    attn = _attention(q, kc, k, vct, vt, even_lambda[0], even_subln[0].reshape(-1, 1), lam_init, tq)
    x1, h2, pt = _out_projection(attn, conv, x, mod0, w_out_bf[0], row(norm2, 0), wr_cat[0], tm)
    x = _moe(x1, h2, pt, mod0, w_gate, w_up, w_down, fnorm, 0, tm, final=False)

    mod1 = mod[1, :b]
    x1, h2, pt = _chunk_mlp_layer(x, mod1, row(norm1, 1), odd_w_in[0].astype(BF16), row(odd_v_norm, 0),
                                  odd_w_s[0].astype(BF16), odd_b_s[0].T, w_out_bf[1], row(norm2, 1),
                                  wr_cat[1], tc)
    return _moe(x1, h2, pt, mod1, w_gate, w_up, w_down, fnorm, 1, tm, final=True)
```

```python
import functools
import math
from typing import NamedTuple

import numpy as np
import jax
import jax.numpy as jnp
from jax import lax
from jax.experimental import pallas as pl
from jax.experimental.pallas import tpu as pltpu

F32 = jnp.float32
BF16 = jnp.bfloat16

D_MODEL = 1024
ATTN_HEADS = 4
QK_DIM = 64
V_DIM = 128
Q_W = ATTN_HEADS * 2 * QK_DIM
CONV_W = 512
EVEN_IN = 3072
N_EXPERTS = 16
CAPACITY_FACTOR = 2
GRID_W = 64
CHUNK = 128
CMLP_GROUPS = 4
ROPE_THETA = 10000.0
NORM_EPS = 1e-6
LANES = 128
HALO_ROWS = 8
VMEM_LIMIT = 56 * 1024 * 1024
PV_ROWS = V_DIM + 16
SAFE_SHIFT = 60.0
ALIGN_ROWS = 16
SUB_TOKENS = 256
SLOT_WINDOW = 64

NT_DIMS = (((1,), (1,)), ((), ()))
TN_DIMS = (((0,), (0,)), ((), ()))


def _dot(a, b):
    return jnp.dot(a, b, preferred_element_type=F32)


def _params(*sem, flags=None):
    return pltpu.CompilerParams(dimension_semantics=sem, vmem_limit_bytes=VMEM_LIMIT, flags=flags)


def _rms_mod(x, g, shift, scale):
    y = x * lax.rsqrt(jnp.mean(x * x, axis=-1, keepdims=True) + NORM_EPS)
    return (y * g) * (1.0 + scale) + shift


def _split_bf16(x):
    hi = x.astype(BF16)
    lo = (x - hi.astype(F32)).astype(BF16)
    return hi, lo


def _mod_kernel(c_ref, w_ref, b_ref, o_ref):
    c = c_ref[...]
    a_hi, a_lo = _split_bf16(c * jax.nn.sigmoid(c))
    w_hi, w_lo = _split_bf16(w_ref[0])
    o_ref[0] = _dot(a_hi, w_hi) + _dot(a_lo, w_hi) + _dot(a_hi, w_lo) + b_ref[0]


def _modulation(cc, w_mod, b_mod):
    depth, d, six_d = w_mod.shape
    rows = cc.shape[0]
    tn = 1536
    return pl.pallas_call(
        _mod_kernel,
        grid=(depth, six_d // tn),
        in_specs=[
            pl.BlockSpec((rows, d), lambda l, j: (0, 0)),
            pl.BlockSpec((1, d, tn), lambda l, j: (l, 0, j)),
            pl.BlockSpec((1, 1, tn), lambda l, j: (l, 0, j)),
        ],
        out_specs=pl.BlockSpec((1, rows, tn), lambda l, j: (l, 0, j)),
        out_shape=jax.ShapeDtypeStruct((depth, rows, six_d), F32),
        compiler_params=_params("arbitrary", "arbitrary"),
        name="modulation",
    )(cc, w_mod, b_mod.reshape(depth, 1, six_d))


def _inproj_kernel(x_ref, xp_ref, xn_ref, mod_ref, n1_ref, w_ref, cos_ref, sa_ref, sb_ref, cw_ref,
                   q_ref, k_ref, vt_ref, c_ref, *, tm, n_tiles):
    i = pl.program_id(1)
    m = mod_ref[0]
    shift, scale = m[0:1], m[1:2]
    g = n1_ref[...]
    h = _rms_mod(x_ref[0], g, shift, scale).astype(BF16)

    cos, sa, sb = cos_ref[...], sa_ref[...], sb_ref[...]
    for col0, out_ref, qscale in ((0, q_ref, QK_DIM ** -0.5 * math.log2(math.e)), (Q_W, k_ref, 1.0)):
        p = _dot(h, w_ref[:, col0:col0 + Q_W])
        for j in range(Q_W // LANES):
            pj = p[:, j * LANES:(j + 1) * LANES]
            r = pj * cos + pltpu.roll(pj, LANES - 16, 1) * sa + pltpu.roll(pj, 16, 1) * sb
            out_ref[0, :, j * LANES:(j + 1) * LANES] = (r * qscale).astype(BF16)

    vt_ref[0] = _dot(h, w_ref[:, 2 * Q_W:2 * Q_W + 512]).T.astype(BF16)

    pc = _dot(h, w_ref[:, 2 * Q_W + 512:])
    gb = pc[:, :CONV_W]
    u = pc[:, CONV_W:2 * CONV_W] * pc[:, 2 * CONV_W:]
    xh = jnp.concatenate([xp_ref[0], xn_ref[0]], axis=0)
    hh = _rms_mod(xh, g, shift, scale).astype(BF16)
    ph = _dot(hh, w_ref[:, 2 * Q_W + 512 + CONV_W:])
    uh = ph[:, :CONV_W] * ph[:, CONV_W:]
    u_before = jnp.where(i > 0, uh[HALO_ROWS - 1:HALO_ROWS], 0.0)
    u_after = jnp.where(i < n_tiles - 1, uh[HALO_ROWS:HALO_ROWS + 1], 0.0)
    row = lax.broadcasted_iota(jnp.int32, (tm, 1), 0)
    u_prev = jnp.where(row == 0, u_before, pltpu.roll(u, 1, 0))
    u_next = jnp.where(row == tm - 1, u_after, pltpu.roll(u, tm - 1, 0))
    cw = cw_ref[...]
    conv = cw[0:1] * u_prev + cw[1:2] * u + cw[2:3] * u_next
    c_ref[0] = (gb * conv).astype(BF16)


def _rope_tables(n):
    rows = n // GRID_W
    row = jnp.repeat(jnp.arange(rows), GRID_W).astype(F32)
    col = jnp.tile(jnp.arange(GRID_W), rows).astype(F32)
    half = QK_DIM // 2
    inv = 1.0 / (ROPE_THETA ** (jnp.arange(0, half, 2, dtype=F32) / half))
    ang_r = row[:, None] * inv
    ang_c = col[:, None] * inv
    ang = jnp.concatenate([ang_r, ang_r, ang_c, ang_c], axis=-1)
    cos, sin = jnp.cos(ang), jnp.sin(ang)
    first_half = (jnp.arange(QK_DIM) % 32) < 16
    sin_a = jnp.where(first_half, -sin, 0.0)
    sin_b = jnp.where(first_half, 0.0, sin)
    tile2 = lambda t: jnp.concatenate([t, t], axis=-1)
    return tile2(cos), tile2(sin_a), tile2(sin_b)


def _in_projection(x, mod0, norm1, w_in_bf, conv_w, tm):
    b, n, d = x.shape
    n_tiles = n // tm
    hb = tm // HALO_ROWS
    n_hblocks = n // HALO_ROWS
    cos, sa, sb = _rope_tables(n)
    out = jax.ShapeDtypeStruct((b, n, Q_W), BF16)
    row_spec = pl.BlockSpec((1, tm, Q_W), lambda bi, i: (bi, i, 0))
    tab_spec = pl.BlockSpec((tm, LANES), lambda bi, i: (i, 0))
    return pl.pallas_call(
        functools.partial(_inproj_kernel, tm=tm, n_tiles=n_tiles),
        grid=(b, n_tiles),
        in_specs=[
            pl.BlockSpec((1, tm, d), lambda bi, i: (bi, i, 0)),
            pl.BlockSpec((1, HALO_ROWS, d), lambda bi, i: (bi, jnp.maximum(i * hb - 1, 0), 0)),
            pl.BlockSpec((1, HALO_ROWS, d), lambda bi, i: (bi, jnp.minimum((i + 1) * hb, n_hblocks - 1), 0)),
            pl.BlockSpec((1, 6, d), lambda bi, i: (bi, 0, 0)),
            pl.BlockSpec((1, d), lambda bi, i: (0, 0)),
            pl.BlockSpec((d, EVEN_IN), lambda bi, i: (0, 0)),
            tab_spec, tab_spec, tab_spec,
            pl.BlockSpec((3, CONV_W), lambda bi, i: (0, 0)),
        ],
        out_specs=[row_spec, row_spec, pl.BlockSpec((1, Q_W, tm), lambda bi, i: (bi, 0, i)), row_spec],
        out_shape=[out, out, jax.ShapeDtypeStruct((b, Q_W, n), BF16), out],
        compiler_params=_params("arbitrary", "arbitrary"),
        name="in_projection",
    )(x, x, x, mod0, norm1, w_in_bf, cos, sa, sb, conv_w)


def _ctxkv_kernel(x_ref, mod_ref, n1_ref, wk_ref, wv_ref, k_ref, vt_ref):
    m = mod_ref[0]
    h = _rms_mod(x_ref[0], n1_ref[...], m[0:1], m[1:2]).astype(BF16)
    k_ref[0] = _dot(h, wk_ref[...]).astype(BF16)
    vt_ref[0] = _dot(h, wv_ref[...]).T.astype(BF16)


def _context_kv(ctx, modc, norm1, w_in_bf):
    b, m, d = ctx.shape
    out = jax.ShapeDtypeStruct((b, m, Q_W), BF16)
    return pl.pallas_call(
        _ctxkv_kernel,
        grid=(b,),
        in_specs=[
            pl.BlockSpec((1, m, d), lambda bi: (bi, 0, 0)),
            pl.BlockSpec((1, 6, d), lambda bi: (0, 0, 0)),
            pl.BlockSpec((1, d), lambda bi: (0, 0)),
            pl.BlockSpec((d, Q_W), lambda bi: (0, 1)),
            pl.BlockSpec((d, Q_W), lambda bi: (0, 2)),
        ],
        out_specs=[pl.BlockSpec((1, m, Q_W), lambda bi: (bi, 0, 0)), pl.BlockSpec((1, Q_W, m), lambda bi: (bi, 0, 0))],
        out_shape=[out, jax.ShapeDtypeStruct((b, Q_W, m), BF16)],
        compiler_params=_params("arbitrary"),
        name="context_kv",
    )(ctx, modc, norm1, w_in_bf, w_in_bf)


def _attn_kernel(lam_ref, q_ref, qall_ref, kc_ref, k_ref, vct_ref, vt_ref, sub_ref, o_ref,
                 kx_ref, vxt_ref, shift_ref, safe_ref, s_ref, e_ref, ox_ref, *, lam_init, m_ctx):
    nk = kx_ref.shape[0]
    tq = q_ref.shape[1]

    @pl.when(pl.program_id(2) == 0)
    def _():
        kx_ref[:m_ctx, :] = kc_ref[0]
        kx_ref[m_ctx:, :] = k_ref[0]
        vxt_ref[:V_DIM, :m_ctx] = vct_ref[0]
        vxt_ref[:V_DIM, m_ctx:] = vt_ref[0]
        row = lax.broadcasted_iota(jnp.int32, (PV_ROWS - V_DIM, nk), 0)
        vxt_ref[V_DIM:, :] = jnp.where(row == 0, 1.0, 0.0).astype(BF16)

        dim = lax.broadcasted_iota(jnp.int32, (LANES, LANES), 0)
        col = lax.broadcasted_iota(jnp.int32, (LANES, LANES), 1)
        ind = jnp.where(col == jnp.where(dim < QK_DIM, 0, 1), 1.0, 0.0).astype(BF16)

        def sq_norms(x):
            xf = x.astype(F32)
            return _dot((xf * xf).astype(BF16), ind)

        kmax = jnp.max(sq_norms(kx_ref[...]), axis=0, keepdims=True)
        qn = sq_norms(qall_ref[0]).T
        bound = jnp.sqrt(qn[0:HALO_ROWS] * jnp.concatenate(
            [kmax[:, 0:1], kmax[:, 1:2], jnp.zeros((HALO_ROWS - 2, 1), F32)], axis=0)) * 1.05
        shift_ref[...] = bound
        safe_ref[0] = (jnp.max(bound) <= SAFE_SHIFT).astype(jnp.int32)

    lp = lam_ref[...]
    lam = (jnp.exp(jnp.sum(lp[0:1] * lp[1:2], keepdims=True))
           - jnp.exp(jnp.sum(lp[2:3] * lp[3:4], keepdims=True)) + lam_init)
    q = q_ref[0]
    lane = lax.broadcasted_iota(jnp.int32, q.shape, 1)
    zero = jnp.zeros_like(q)
    qms = [jnp.where(in_map, q, zero) for in_map in (lane < QK_DIM, lane >= QK_DIM)]
    q0 = pl.multiple_of(pl.program_id(2) * tq, tq)
    shifts = [shift_ref[mi:mi + 1, pl.ds(q0, tq)] for mi in range(2)]
    bound_is_safe = safe_ref[0] == 1

    @pl.when(bound_is_safe)
    def _():
        for mi in range(2):
            st = lax.dot_general(kx_ref[...], qms[mi], NT_DIMS, preferred_element_type=F32)
            e_ref[mi] = jnp.exp2(st - shifts[mi]).astype(BF16)
            ox_ref[mi] = _dot(vxt_ref[...], e_ref[mi])

    @pl.when(jnp.logical_not(bound_is_safe))
    def _():
        for mi in range(2):
            s_ref[mi] = lax.dot_general(kx_ref[...], qms[mi], NT_DIMS, preferred_element_type=F32)
        for mi in range(2):
            e_ref[mi] = jnp.exp2(s_ref[mi] - jnp.max(s_ref[mi], axis=0, keepdims=True)).astype(BF16)
            ox_ref[mi] = _dot(vxt_ref[...], e_ref[mi])

    o1, l1 = ox_ref[0, :V_DIM, :], ox_ref[0, V_DIM:V_DIM + 1, :]
    o2, l2 = ox_ref[1, :V_DIM, :], ox_ref[1, V_DIM:V_DIM + 1, :]
    ot = o1 * (1.0 / l1) - o2 * (lam / l2)
    ot = ot * lax.rsqrt(jnp.mean(ot * ot, axis=0, keepdims=True) + NORM_EPS)
    o_ref[0] = (ot * sub_ref[...] * (1.0 - lam_init)).T.astype(BF16)


def _attention(q, kc, k, vct, vt, lam_p, subln, lam_init, tq):
    b, n, _ = q.shape
    m_ctx = kc.shape[1]
    head_rows = lambda rows: pl.BlockSpec((1, rows, V_DIM), lambda bi, h, i: (bi, 0, h))
    head_cols = lambda cols: pl.BlockSpec((1, V_DIM, cols), lambda bi, h, i: (bi, h, 0))
    return pl.pallas_call(
        functools.partial(_attn_kernel, lam_init=lam_init, m_ctx=m_ctx),
        grid=(b, ATTN_HEADS, n // tq),
        in_specs=[
            pl.BlockSpec((4, QK_DIM), lambda bi, h, i: (0, 0)),
            pl.BlockSpec((1, tq, V_DIM), lambda bi, h, i: (bi, i, h)),
            head_rows(n), head_rows(m_ctx), head_rows(n), head_cols(m_ctx), head_cols(n),
            pl.BlockSpec((V_DIM, 1), lambda bi, h, i: (0, 0)),
        ],
        out_specs=pl.BlockSpec((1, tq, V_DIM), lambda bi, h, i: (bi, i, h)),
        out_shape=jax.ShapeDtypeStruct((b, n, Q_W), BF16),
        scratch_shapes=[pltpu.VMEM((m_ctx + n, V_DIM), BF16), pltpu.VMEM((PV_ROWS, m_ctx + n), BF16),
                        pltpu.VMEM((HALO_ROWS, n), F32), pltpu.SMEM((1,), jnp.int32),
                        pltpu.VMEM((2, m_ctx + n, tq), F32), pltpu.VMEM((2, m_ctx + n, tq), BF16),
                        pltpu.VMEM((2, PV_ROWS, tq), F32)],
        compiler_params=_params("arbitrary", "arbitrary", "arbitrary"),
        name="diff_attention",
    )(lam_p, q, q, kc, k, vct, vt, subln)


def _tail(y, x, m, n2, wr_ref, x1_ref, h2_ref, pt_ref):
    x1 = x + m[2:3] * y
    x1_ref[0] = x1
    h2 = _rms_mod(x1, n2, m[3:4], m[4:5])
    h_hi, h_lo = _split_bf16(h2)
    h2_ref[0] = h_hi
    hcat = jnp.concatenate([h_hi, h_lo], axis=1)
    half = hcat.shape[0] // 2
    parts = [_dot(hcat[r0:r0 + half], wr_ref[...]) for r0 in (0, half)]
    prod = jnp.concatenate(parts, axis=0)
    logits = (prod[:, :LANES] + prod[:, LANES:]).T[:N_EXPERTS]
    ex = jnp.exp(logits - jnp.max(logits, axis=0, keepdims=True))
    pt_ref[0] = ex / jnp.sum(ex, axis=0, keepdims=True)


def _tail_specs(b, n, d, tm):
    in_specs = [
        pl.BlockSpec((1, d), lambda bi, i, *_: (0, 0)),
        pl.BlockSpec((2 * d, 2 * LANES), lambda bi, i, *_: (0, 0)),
    ]
    out_specs = [
        pl.BlockSpec((1, tm, d), lambda bi, i, *_: (bi, i, 0)),
        pl.BlockSpec((1, tm, d), lambda bi, i, *_: (bi, i, 0)),
        pl.BlockSpec((1, N_EXPERTS, tm), lambda bi, i, *_: (bi, 0, i)),
    ]
    out_shape = [
        jax.ShapeDtypeStruct((b, n, d), F32),
        jax.ShapeDtypeStruct((b, n, d), BF16),
        jax.ShapeDtypeStruct((b, N_EXPERTS, n), F32),
    ]
    return in_specs, out_specs, out_shape


def _outproj_kernel(a_ref, c_ref, x_ref, mod_ref, wo_ref, n2_ref, wr_ref,
                    x1_ref, h2_ref, pt_ref, mix_ref):
    mix_ref[:, :Q_W] = a_ref[0]
    mix_ref[:, Q_W:] = c_ref[0]
    y = _dot(mix_ref[...], wo_ref[...])
    _tail(y, x_ref[0], mod_ref[0], n2_ref[...], wr_ref, x1_ref, h2_ref, pt_ref)


def _out_projection(attn, conv, x, mod0, w_out_bf, norm2, wr_cat, tm):
    b, n, d = x.shape
    tail_in, out_specs, out_shape = _tail_specs(b, n, d, tm)
    return pl.pallas_call(
        _outproj_kernel,
        grid=(b, n // tm),
        in_specs=[
            pl.BlockSpec((1, tm, Q_W), lambda bi, i: (bi, i, 0)),
            pl.BlockSpec((1, tm, CONV_W), lambda bi, i: (bi, i, 0)),
            pl.BlockSpec((1, tm, d), lambda bi, i: (bi, i, 0)),
            pl.BlockSpec((1, 6, d), lambda bi, i: (bi, 0, 0)),
            pl.BlockSpec((d, d), lambda bi, i: (0, 0)),
        ] + tail_in,
        out_specs=out_specs,
        out_shape=out_shape,
        scratch_shapes=[pltpu.VMEM((tm, d), BF16)],
        compiler_params=_params("arbitrary", "arbitrary"),
        name="out_projection",
    )(attn, conv, x, mod0, w_out_bf, norm2, wr_cat)


def _gelu(x):
    return 0.5 * x * (1.0 + lax.erf(x * np.float32(1.0 / math.sqrt(2.0))))


def _cmlp_body(x, m, n1_ref, w1_ref, vn_ref, ws_ref, bs_ref, wo_ref, n2_ref, wr_ref,
               x1_ref, h2_ref, pt_ref, mix_ref):
    tm, d = x.shape
    h = _rms_mod(x, n1_ref[...], m[0:1], m[1:2]).astype(BF16)
    p = _gelu(_dot(h, w1_ref[...]))
    u, v = p[:, :d], p[:, d:]
    v = v * lax.rsqrt(jnp.mean(v * v, axis=-1, keepdims=True) + NORM_EPS) * vn_ref[...]
    vb = v.astype(BF16)
    gw = d // CMLP_GROUPS
    bs = bs_ref[...]
    for c in range(tm // CHUNK):
        r0 = c * CHUNK
        for g in range(CMLP_GROUPS):
            s = _dot(ws_ref[g], vb[r0:r0 + CHUNK, g * gw:(g + 1) * gw]) + bs[:, g:g + 1]
            mix_ref[r0:r0 + CHUNK, g * gw:(g + 1) * gw] = (u[r0:r0 + CHUNK, g * gw:(g + 1) * gw] * s).astype(BF16)
    y = _dot(mix_ref[...], wo_ref[...])
    _tail(y, x, m, n2_ref[...], wr_ref, x1_ref, h2_ref, pt_ref)


def _route_kernel(p_ref, sel_ref, start_ref, fits_ref, *, cap, pchunk, window):
    rows, n = p_ref.shape

    def bit_step(it, t):
        cand = t | jnp.left_shift(jnp.int32(1), 30 - it)
        cf = lax.bitcast_convert_type(cand, F32)
        cnt = jnp.sum(jnp.where(p_ref[...] >= cf, 1.0, 0.0), axis=1, keepdims=True)
        return jnp.where(cnt >= cap, cand, t)

    t = lax.fori_loop(0, 31, bit_step, jnp.zeros((rows, 1), jnp.int32))
    tf = lax.bitcast_convert_type(t, F32)
    p = p_ref[...]
    gt = p > tf
    eq = jnp.logical_and(p >= tf, jnp.logical_not(gt))
    need = cap - jnp.sum(jnp.where(gt, 1.0, 0.0), axis=1, keepdims=True)

    ri = lax.broadcasted_iota(jnp.int32, (pchunk, pchunk), 0)
    ci = lax.broadcasted_iota(jnp.int32, (pchunk, pchunk), 1)
    upper = jnp.where(ri < ci, 1.0, 0.0).astype(BF16)

    def prefix(mask):
        mf = jnp.where(mask, 1.0, 0.0)
        carry = jnp.zeros((rows, 1), F32)
        parts, before, inside = [], [], []
        for c in range(n // pchunk):
            blk = mf[:, c * pchunk:(c + 1) * pchunk]
            parts.append(_dot(blk.astype(BF16), upper) + carry)
            total = jnp.sum(blk, axis=1, keepdims=True)
            before.append(carry)
            inside.append(total)
            carry = carry + total
        return jnp.concatenate(parts, axis=1), jnp.concatenate(before, axis=1), jnp.concatenate(inside, axis=1)

    chosen = jnp.logical_or(gt, jnp.logical_and(eq, prefix(eq)[0] < need))
    rank, before, inside = prefix(chosen)
    sel_ref[...] = jnp.where(chosen, rank, -1.0)
    start = jnp.minimum(jnp.floor(before * (1.0 / ALIGN_ROWS)) * ALIGN_ROWS, float(cap - window))
    start_ref[...] = start.astype(jnp.int32)
    fits_ref[...] = jnp.where(before + inside <= start + window, 1, 0).astype(jnp.int32)


def _route(pt, cap, window):
    b, e, n = pt.shape
    rows = b * e
    pchunk = min(SUB_TOKENS, n)
    n_chunks = n // pchunk
    full = lambda cols: pl.BlockSpec((rows, cols), lambda i: (0, 0))
    sel, start, fits = pl.pallas_call(
        functools.partial(_route_kernel, cap=cap, pchunk=pchunk, window=window),
        grid=(1,),
        in_specs=[full(n)],
        out_specs=[full(n), full(n_chunks), full(n_chunks)],
        out_shape=[jax.ShapeDtypeStruct((rows, n), F32), jax.ShapeDtypeStruct((rows, n_chunks), jnp.int32),
                   jax.ShapeDtypeStruct((rows, n_chunks), jnp.int32)],
        compiler_params=_params("arbitrary"),
        name="route",
    )(pt.reshape(rows, n))
    return sel.reshape(b, e, n), start.reshape(-1), fits.reshape(-1)


def _gather_kernel(start_ref, fits_ref, sel_ref, h_ref, xs_ref, *, cap, window, sub, n_chunks):
    bi = pl.program_id(0)
    per_pass = sub // window
    entry = lambda e, c: (bi * N_EXPERTS + e) * n_chunks + c

    all_fit = fits_ref[entry(0, 0)]
    for e in range(N_EXPERTS):
        for c in range(n_chunks):
            all_fit = jnp.minimum(all_fit, fits_ref[entry(e, c)])

    @pl.when(all_fit == 1)
    def _():
        xs_ref[...] = jnp.zeros(xs_ref.shape, BF16)
        offs = lax.broadcasted_iota(jnp.int32, (window, sub), 0)
        for c in range(n_chunks):
            sel = sel_ref[0, :, c * sub:(c + 1) * sub]
            hc = h_ref[0, c * sub:(c + 1) * sub, :]
            for g in range(N_EXPERTS // per_pass):
                group = range(g * per_pass, (g + 1) * per_pass)
                starts = [pl.multiple_of(start_ref[entry(e, c)], ALIGN_ROWS) for e in group]
                hots = [jnp.where(sel[e:e + 1, :] == (offs + a).astype(F32), 1.0, 0.0).astype(BF16)
                        for e, a in zip(group, starts)]
                z = _dot(jnp.concatenate(hots, axis=0), hc).astype(BF16)
                for k, (e, a) in enumerate(zip(group, starts)):
                    xs_ref[0, e, pl.ds(a, window), :] += z[k * window:(k + 1) * window]

    @pl.when(all_fit != 1)
    def _():
        n = sel_ref.shape[-1]
        rank = lax.broadcasted_iota(jnp.int32, (cap, n), 0).astype(F32)
        for e in range(N_EXPERTS):
            onehot = jnp.where(sel_ref[0, e:e + 1, :] == rank, 1.0, 0.0).astype(BF16)
            xs_ref[0, e] = _dot(onehot, h_ref[0]).astype(BF16)


def _gather(route, h2):
    b, n, d = h2.shape
    cap = route.geom["cap"]
    return pl.pallas_call(
        functools.partial(_gather_kernel, **route.geom),
        grid_spec=pltpu.PrefetchScalarGridSpec(
            num_scalar_prefetch=2,
            grid=(b,),
            in_specs=[
                pl.BlockSpec((1, N_EXPERTS, n), lambda bi, *_: (bi, 0, 0)),
                pl.BlockSpec((1, n, d), lambda bi, *_: (bi, 0, 0)),
            ],
            out_specs=pl.BlockSpec((1, N_EXPERTS, cap, d), lambda bi, *_: (bi, 0, 0, 0)),
        ),
        out_shape=jax.ShapeDtypeStruct((b, N_EXPERTS, cap, d), BF16),
        compiler_params=_params("arbitrary"),
        name="moe_gather",
    )(route.start, route.fits, route.sel, h2)


def _expert_kernel(sel_ref, pt_ref, xs_ref, wg_ref, wu_ref, wd_ref, y_ref, wg_s, wu_s, wd_s, *, cap):
    e = pl.program_id(0)

    @pl.when(pl.program_id(1) == 0)
    def _():
        wg_s[...] = wg_ref[0, 0].astype(BF16)
        wu_s[...] = wu_ref[0, 0].astype(BF16)
        wd_s[...] = wd_ref[0, 0].astype(BF16)

    bb, _, n = sel_ref.shape
    d = xs_ref.shape[-1]
    rank = lax.broadcasted_iota(jnp.int32, (cap, n), 0).astype(F32)
    vals = []
    for j in range(bb):
        sel = sel_ref[j, pl.ds(e, 1), :]
        prob = pt_ref[j, pl.ds(e, 1), :]
        vals.append(jnp.sum(jnp.where(sel == rank, prob, 0.0), axis=1, keepdims=True))
    xs = xs_ref[...].reshape(bb * cap, d)
    a = _dot(xs, wg_s[...])
    bm = _dot(xs, wu_s[...])
    hid = (a * jax.nn.sigmoid(a) * bm).astype(BF16)
    y = _dot(hid, wd_s[...]) * jnp.concatenate(vals, axis=0)
    y_ref[...] = y.astype(BF16).reshape(y_ref.shape)


def _expert_ffn(sel, pt, xs, w_gate, w_up, w_down, layer, cap):
    b, _, _, d = xs.shape
    n = sel.shape[-1]
    hdim = w_gate.shape[-1]
    bb = next(k for k in (4, 2, 1) if b % k == 0)
    w_spec = lambda rows, cols: pl.BlockSpec((1, 1, rows, cols), lambda e, bi: (layer, e, 0, 0))
    return pl.pallas_call(
        functools.partial(_expert_kernel, cap=cap),
        grid=(N_EXPERTS, b // bb),
        in_specs=[
            pl.BlockSpec((bb, N_EXPERTS, n), lambda e, bi: (bi, 0, 0)),
            pl.BlockSpec((bb, N_EXPERTS, n), lambda e, bi: (bi, 0, 0)),
            pl.BlockSpec((bb, 1, cap, d), lambda e, bi: (bi, e, 0, 0)),
            w_spec(d, hdim), w_spec(d, hdim), w_spec(hdim, d),
        ],
        out_specs=pl.BlockSpec((bb, 1, cap, d), lambda e, bi: (bi, e, 0, 0)),
        out_shape=jax.ShapeDtypeStruct((b, N_EXPERTS, cap, d), BF16),
        scratch_shapes=[pltpu.VMEM((d, hdim), BF16), pltpu.VMEM((d, hdim), BF16), pltpu.VMEM((hdim, d), BF16)],
        compiler_params=_params("arbitrary", "arbitrary"),
        name="expert_ffn",
    )(sel, pt, xs, w_gate, w_up, w_down)


def _scatter_add(start_ref, fits_ref, sel_ref, y_ref, acc_ref, *, cap, window, sub, n_chunks):
    bi, i = pl.program_id(0), pl.program_id(1)
    tm = sel_ref.shape[-1]
    n_sub = tm // sub
    per_pass = sub // window
    entry = lambda e, j: (bi * N_EXPERTS + e) * n_chunks + i * n_sub + j

    all_fit = fits_ref[entry(0, 0)]
    for e in range(N_EXPERTS):
        for j in range(n_sub):
            all_fit = jnp.minimum(all_fit, fits_ref[entry(e, j)])

    @pl.when(all_fit == 1)
    def _():
        offs = lax.broadcasted_iota(jnp.int32, (window, sub), 0)
        for j in range(n_sub):
            sel = sel_ref[0, :, j * sub:(j + 1) * sub]
            acc = jnp.zeros((sub, acc_ref.shape[1]), F32)
            for g in range(N_EXPERTS // per_pass):
                hots, ys = [], []
                for e in range(g * per_pass, (g + 1) * per_pass):
                    a = pl.multiple_of(start_ref[entry(e, j)], ALIGN_ROWS)
                    hots.append(jnp.where(sel[e:e + 1, :] == (offs + a).astype(F32), 1.0, 0.0).astype(BF16))
                    ys.append(y_ref[0, e, pl.ds(a, window), :])
                acc = acc + lax.dot_general(jnp.concatenate(hots, axis=0), jnp.concatenate(ys, axis=0),
                                            TN_DIMS, preferred_element_type=F32)
            acc_ref[j * sub:(j + 1) * sub, :] = acc

    @pl.when(all_fit != 1)
    def _():
        sel = sel_ref[0]
        rank = lax.broadcasted_iota(jnp.int32, (cap, tm), 0).astype(F32)
        acc = jnp.zeros(acc_ref.shape, F32)
        for e in range(N_EXPERTS):
            onehot = jnp.where(sel[e:e + 1, :] == rank, 1.0, 0.0).astype(BF16)
            acc = acc + lax.dot_general(onehot, y_ref[0, e], TN_DIMS, preferred_element_type=F32)
        acc_ref[...] = acc


def _combine_final_kernel(start_ref, fits_ref, sel_ref, y_ref, x_ref, mod_ref, fn_ref, o_ref, acc_ref, **geom):
    _scatter_add(start_ref, fits_ref, sel_ref, y_ref, acc_ref, **geom)
    x2 = x_ref[0] + mod_ref[0][5:6] * acc_ref[...]
    o_ref[0] = x2 * lax.rsqrt(jnp.mean(x2 * x2, axis=-1, keepdims=True) + NORM_EPS) * fn_ref[...]


def _combine_cmlp_kernel(start_ref, fits_ref, sel_ref, y_ref, x_ref, mod0_ref, mod1_ref, n1_ref, w1_ref, vn_ref,
                         ws_ref, bs_ref, wo_ref, n2_ref, wr_ref, x1_ref, h2_ref, pt_ref, acc_ref, mix_ref, **geom):
    _scatter_add(start_ref, fits_ref, sel_ref, y_ref, acc_ref, **geom)
    x2 = x_ref[0] + mod0_ref[0][5:6] * acc_ref[...]
    _cmlp_body(x2, mod1_ref[0], n1_ref, w1_ref, vn_ref, ws_ref, bs_ref, wo_ref, n2_ref, wr_ref,
               x1_ref, h2_ref, pt_ref, mix_ref)


def _combine_specs(b, n, d, cap, tm):
    return [
        pl.BlockSpec((1, N_EXPERTS, tm), lambda bi, i, *_: (bi, 0, i)),
        pl.BlockSpec((1, N_EXPERTS, cap, d), lambda bi, i, *_: (bi, 0, 0, 0)),
        pl.BlockSpec((1, tm, d), lambda bi, i, *_: (bi, i, 0)),
        pl.BlockSpec((1, 6, d), lambda bi, i, *_: (bi, 0, 0)),
    ]


def _combine_final(route, y, x1, mod_l, final_norm, tm):
    b, n, d = x1.shape
    return pl.pallas_call(
        functools.partial(_combine_final_kernel, **route.geom),
        grid_spec=pltpu.PrefetchScalarGridSpec(
            num_scalar_prefetch=2,
            grid=(b, n // tm),
            in_specs=_combine_specs(b, n, d, route.geom["cap"], tm) + [pl.BlockSpec((1, d), lambda bi, i, *_: (0, 0))],
            out_specs=pl.BlockSpec((1, tm, d), lambda bi, i, *_: (bi, i, 0)),
            scratch_shapes=[pltpu.VMEM((tm, d), F32)],
        ),
        out_shape=jax.ShapeDtypeStruct((b, n, d), F32),
        compiler_params=_params("arbitrary", "arbitrary"),
        name="moe_combine_final",
    )(route.start, route.fits, route.sel, y, x1, mod_l, final_norm)


def _combine_chunk_mlp(route, y, x1, mod0, mod1, norm1, w1_bf, v_norm, ws_bf, bs_t, w_out_bf, norm2, wr_cat, tm):
    b, n, d = x1.shape
    const = lambda *shape: pl.BlockSpec(shape, lambda bi, i, *_: (0,) * len(shape))
    tail_in, out_specs, out_shape = _tail_specs(b, n, d, tm)
    return pl.pallas_call(
        functools.partial(_combine_cmlp_kernel, **route.geom),
        grid_spec=pltpu.PrefetchScalarGridSpec(
            num_scalar_prefetch=2,
            grid=(b, n // tm),
            in_specs=_combine_specs(b, n, d, route.geom["cap"], tm) + [
                pl.BlockSpec((1, 6, d), lambda bi, i, *_: (bi, 0, 0)),
                const(1, d), const(d, 2 * d), const(1, d), const(CMLP_GROUPS, CHUNK, CHUNK),
                const(CHUNK, CMLP_GROUPS), const(d, d),
            ] + tail_in,
            out_specs=out_specs,
            scratch_shapes=[pltpu.VMEM((tm, d), F32), pltpu.VMEM((tm, d), BF16)],
        ),
        out_shape=out_shape,
        compiler_params=_params("arbitrary", "arbitrary"),
        name="combine_chunk_mlp",
    )(route.start, route.fits, route.sel, y, x1, mod0, mod1, norm1, w1_bf, v_norm, ws_bf, bs_t, w_out_bf,
      norm2, wr_cat)


class _Routing(NamedTuple):
    sel: jax.Array
    start: jax.Array
    fits: jax.Array
    geom: dict


def _moe_experts(h2, pt, w_gate, w_up, w_down, layer):
    n = h2.shape[1]
    cap = CAPACITY_FACTOR * n // N_EXPERTS
    window = min(SLOT_WINDOW, cap)
    sub = min(SUB_TOKENS, n)
    sel, start, fits = _route(pt, cap, window)
    route = _Routing(sel, start, fits, dict(cap=cap, window=window, sub=sub, n_chunks=n // sub))
    xs = _gather(route, h2)
    return route, _expert_ffn(sel, pt, xs, w_gate, w_up, w_down, layer, cap)


def kernel(x, c, ctx, c_ctx, w_mod, b_mod, norm1, norm2, even_w_in, even_lambda, even_subln, even_conv_w,
           odd_w_in, odd_v_norm, odd_w_s, odd_b_s, w_out, w_router, w_gate, w_up, w_down, final_norm):
    b, n, d = x.shape
    depth = w_mod.shape[0]
    assert d == D_MODEL and depth == 2 and n % 256 == 0
    t_proj = min(1024, n)
    t_attn = min(1024, n)
    t_moe = min(512, n)

    cc = jnp.concatenate([c, c_ctx[None, :]], axis=0)
    mod = _modulation(cc, w_mod, b_mod).reshape(depth, b + 1, 6, d)
    row = lambda a, l: a[l].reshape(1, -1)
    wr_pad = jnp.pad(w_router, ((0, 0), (0, 0), (0, LANES - N_EXPERTS)))
    wr_hi = wr_pad.astype(BF16)
    wr_lo = (wr_pad - wr_hi.astype(F32)).astype(BF16)
    wr_cat = jnp.concatenate([jnp.concatenate([wr_hi, wr_lo], axis=2),
                              jnp.concatenate([wr_hi, jnp.zeros_like(wr_lo)], axis=2)], axis=1)
    w_out_bf = w_out.astype(BF16)
    fnorm = final_norm.reshape(1, d)

    mod0, modc = mod[0, :b], mod[0, b:]
    w_in_bf = even_w_in[0].astype(BF16)
    q, k, vt, conv = _in_projection(x, mod0, row(norm1, 0), w_in_bf, even_conv_w[0], t_proj)
    kc, vct = _context_kv(ctx, modc, row(norm1, 0), w_in_bf)
    lam_init = 0.8 - 0.6 * math.exp(-0.3 * 0)
    attn = _attention(q, kc, k, vct, vt, even_lambda[0], even_subln[0].reshape(-1, 1), lam_init, t_attn)
    x1, h2, pt = _out_projection(attn, conv, x, mod0, w_out_bf[0], row(norm2, 0), wr_cat[0], t_proj)
    route, y = _moe_experts(h2, pt, w_gate, w_up, w_down, 0)

    mod1 = mod[1, :b]
    x1, h2, pt = _combine_chunk_mlp(route, y, x1, mod0, mod1, row(norm1, 1), odd_w_in[0].astype(BF16),
                                    row(odd_v_norm, 0), odd_w_s[0].astype(BF16), odd_b_s[0].T, w_out_bf[1],
                                    row(norm2, 1), wr_cat[1], t_moe)
    route, y = _moe_experts(h2, pt, w_gate, w_up, w_down, 1)
    return _combine_final(route, y, x1, mod1, fnorm, t_moe)
```

```python
import functools
import math
from typing import NamedTuple

import numpy as np
import jax
import jax.numpy as jnp
from jax import lax
from jax.experimental import pallas as pl
from jax.experimental.pallas import tpu as pltpu

F32 = jnp.float32
BF16 = jnp.bfloat16

D_MODEL = 1024
ATTN_HEADS = 4
QK_DIM = 64
V_DIM = 128
Q_W = ATTN_HEADS * 2 * QK_DIM
CONV_W = 512
EVEN_IN = 3072
N_EXPERTS = 16
CAPACITY_FACTOR = 2
GRID_W = 64
CHUNK = 128
CMLP_GROUPS = 4
ROPE_THETA = 10000.0
NORM_EPS = 1e-6
LANES = 128
HALO_ROWS = 8
VMEM_LIMIT = 56 * 1024 * 1024
PV_ROWS = V_DIM + 16
SAFE_SHIFT = 60.0
ALIGN_ROWS = 16
SUB_TOKENS = 256
SLOT_WINDOW = 64

NT_DIMS = (((1,), (1,)), ((), ()))
TN_DIMS = (((0,), (0,)), ((), ()))


def _dot(a, b):
    return jnp.dot(a, b, preferred_element_type=F32)


def _params(*sem, flags=None):
    return pltpu.CompilerParams(dimension_semantics=sem, vmem_limit_bytes=VMEM_LIMIT, flags=flags)


def _rms_mod(x, g, shift, scale):
    y = x * lax.rsqrt(jnp.mean(x * x, axis=-1, keepdims=True) + NORM_EPS)
    return (y * g) * (1.0 + scale) + shift


def _split_bf16(x):
    hi = x.astype(BF16)
    lo = (x - hi.astype(F32)).astype(BF16)
    return hi, lo


def _mod_kernel(c_ref, w_ref, b_ref, o_ref):
    c = c_ref[...]
    a_hi, a_lo = _split_bf16(c * jax.nn.sigmoid(c))
    w_hi, w_lo = _split_bf16(w_ref[0])
    o_ref[0] = _dot(a_hi, w_hi) + _dot(a_lo, w_hi) + _dot(a_hi, w_lo) + b_ref[0]


def _modulation(cc, w_mod, b_mod):
    depth, d, six_d = w_mod.shape
    rows = cc.shape[0]
    tn = 1536
    return pl.pallas_call(
        _mod_kernel,
        grid=(depth, six_d // tn),
        in_specs=[
            pl.BlockSpec((rows, d), lambda l, j: (0, 0)),
            pl.BlockSpec((1, d, tn), lambda l, j: (l, 0, j)),
            pl.BlockSpec((1, 1, tn), lambda l, j: (l, 0, j)),
        ],
        out_specs=pl.BlockSpec((1, rows, tn), lambda l, j: (l, 0, j)),
        out_shape=jax.ShapeDtypeStruct((depth, rows, six_d), F32),
        compiler_params=_params("arbitrary", "arbitrary"),
        name="modulation",
    )(cc, w_mod, b_mod.reshape(depth, 1, six_d))


def _inproj_kernel(x_ref, xp_ref, xn_ref, mod_ref, n1_ref, w_ref, cos_ref, sa_ref, sb_ref, cw_ref,
                   q_ref, k_ref, vt_ref, c_ref, *, tm, n_tiles):
    i = pl.program_id(1)
    m = mod_ref[0]
    shift, scale = m[0:1], m[1:2]
    g = n1_ref[...]
    h = _rms_mod(x_ref[0], g, shift, scale).astype(BF16)

    cos, sa, sb = cos_ref[...], sa_ref[...], sb_ref[...]
    for col0, out_ref, qscale in ((0, q_ref, QK_DIM ** -0.5 * math.log2(math.e)), (Q_W, k_ref, 1.0)):
        p = _dot(h, w_ref[:, col0:col0 + Q_W])
        for j in range(Q_W // LANES):
            pj = p[:, j * LANES:(j + 1) * LANES]
            r = pj * cos + pltpu.roll(pj, LANES - 16, 1) * sa + pltpu.roll(pj, 16, 1) * sb
            out_ref[0, :, j * LANES:(j + 1) * LANES] = (r * qscale).astype(BF16)

    vt_ref[0] = _dot(h, w_ref[:, 2 * Q_W:2 * Q_W + 512]).T.astype(BF16)

    pc = _dot(h, w_ref[:, 2 * Q_W + 512:])
    gb = pc[:, :CONV_W]
    u = pc[:, CONV_W:2 * CONV_W] * pc[:, 2 * CONV_W:]
    xh = jnp.concatenate([xp_ref[0], xn_ref[0]], axis=0)
    hh = _rms_mod(xh, g, shift, scale).astype(BF16)
    ph = _dot(hh, w_ref[:, 2 * Q_W + 512 + CONV_W:])
    uh = ph[:, :CONV_W] * ph[:, CONV_W:]
    u_before = jnp.where(i > 0, uh[HALO_ROWS - 1:HALO_ROWS], 0.0)
    u_after = jnp.where(i < n_tiles - 1, uh[HALO_ROWS:HALO_ROWS + 1], 0.0)
    row = lax.broadcasted_iota(jnp.int32, (tm, 1), 0)
    u_prev = jnp.where(row == 0, u_before, pltpu.roll(u, 1, 0))
    u_next = jnp.where(row == tm - 1, u_after, pltpu.roll(u, tm - 1, 0))
    cw = cw_ref[...]
    conv = cw[0:1] * u_prev + cw[1:2] * u + cw[2:3] * u_next
    c_ref[0] = (gb * conv).astype(BF16)


def _rope_tables(n):
    rows = n // GRID_W
    row = jnp.repeat(jnp.arange(rows), GRID_W).astype(F32)
    col = jnp.tile(jnp.arange(GRID_W), rows).astype(F32)
    half = QK_DIM // 2
    inv = 1.0 / (ROPE_THETA ** (jnp.arange(0, half, 2, dtype=F32) / half))
    ang_r = row[:, None] * inv
    ang_c = col[:, None] * inv
    ang = jnp.concatenate([ang_r, ang_r, ang_c, ang_c], axis=-1)
    cos, sin = jnp.cos(ang), jnp.sin(ang)
    first_half = (jnp.arange(QK_DIM) % 32) < 16
    sin_a = jnp.where(first_half, -sin, 0.0)
    sin_b = jnp.where(first_half, 0.0, sin)
    tile2 = lambda t: jnp.concatenate([t, t], axis=-1)
    return tile2(cos), tile2(sin_a), tile2(sin_b)


def _in_projection(x, mod0, norm1, w_in_bf, conv_w, tm):
    b, n, d = x.shape
    n_tiles = n // tm
    hb = tm // HALO_ROWS
    n_hblocks = n // HALO_ROWS
    cos, sa, sb = _rope_tables(n)
    out = jax.ShapeDtypeStruct((b, n, Q_W), BF16)
    row_spec = pl.BlockSpec((1, tm, Q_W), lambda bi, i: (bi, i, 0))
    tab_spec = pl.BlockSpec((tm, LANES), lambda bi, i: (i, 0))
    return pl.pallas_call(
        functools.partial(_inproj_kernel, tm=tm, n_tiles=n_tiles),
        grid=(b, n_tiles),
        in_specs=[
            pl.BlockSpec((1, tm, d), lambda bi, i: (bi, i, 0)),
            pl.BlockSpec((1, HALO_ROWS, d), lambda bi, i: (bi, jnp.maximum(i * hb - 1, 0), 0)),
            pl.BlockSpec((1, HALO_ROWS, d), lambda bi, i: (bi, jnp.minimum((i + 1) * hb, n_hblocks - 1), 0)),
            pl.BlockSpec((1, 6, d), lambda bi, i: (bi, 0, 0)),
            pl.BlockSpec((1, d), lambda bi, i: (0, 0)),
            pl.BlockSpec((d, EVEN_IN), lambda bi, i: (0, 0)),
            tab_spec, tab_spec, tab_spec,
            pl.BlockSpec((3, CONV_W), lambda bi, i: (0, 0)),
        ],
        out_specs=[row_spec, row_spec, pl.BlockSpec((1, Q_W, tm), lambda bi, i: (bi, 0, i)), row_spec],
        out_shape=[out, out, jax.ShapeDtypeStruct((b, Q_W, n), BF16), out],
        compiler_params=_params("arbitrary", "arbitrary"),
        name="in_projection",
    )(x, x, x, mod0, norm1, w_in_bf, cos, sa, sb, conv_w)


def _ctxkv_kernel(x_ref, mod_ref, n1_ref, wk_ref, wv_ref, k_ref, vt_ref):
    m = mod_ref[0]
    h = _rms_mod(x_ref[0], n1_ref[...], m[0:1], m[1:2]).astype(BF16)
    k_ref[0] = _dot(h, wk_ref[...]).astype(BF16)
    vt_ref[0] = _dot(h, wv_ref[...]).T.astype(BF16)


def _context_kv(ctx, modc, norm1, w_in_bf):
    b, m, d = ctx.shape
    out = jax.ShapeDtypeStruct((b, m, Q_W), BF16)
    return pl.pallas_call(
        _ctxkv_kernel,
        grid=(b,),
        in_specs=[
            pl.BlockSpec((1, m, d), lambda bi: (bi, 0, 0)),
            pl.BlockSpec((1, 6, d), lambda bi: (0, 0, 0)),
            pl.BlockSpec((1, d), lambda bi: (0, 0)),
            pl.BlockSpec((d, Q_W), lambda bi: (0, 1)),
            pl.BlockSpec((d, Q_W), lambda bi: (0, 2)),
        ],
        out_specs=[pl.BlockSpec((1, m, Q_W), lambda bi: (bi, 0, 0)), pl.BlockSpec((1, Q_W, m), lambda bi: (bi, 0, 0))],
        out_shape=[out, jax.ShapeDtypeStruct((b, Q_W, m), BF16)],
        compiler_params=_params("arbitrary"),
        name="context_kv",
    )(ctx, modc, norm1, w_in_bf, w_in_bf)


def _attn_kernel(lam_ref, q_ref, qall_ref, kc_ref, k_ref, vct_ref, vt_ref, sub_ref, o_ref,
                 kx_ref, vxt_ref, shift_ref, safe_ref, s_ref, e_ref, ox_ref, *, lam_init, m_ctx):
    nk = kx_ref.shape[0]
    tq = q_ref.shape[1]

    @pl.when(pl.program_id(2) == 0)
    def _():
        kx_ref[:m_ctx, :] = kc_ref[0]
        kx_ref[m_ctx:, :] = k_ref[0]
        vxt_ref[:V_DIM, :m_ctx] = vct_ref[0]
        vxt_ref[:V_DIM, m_ctx:] = vt_ref[0]
        row = lax.broadcasted_iota(jnp.int32, (PV_ROWS - V_DIM, nk), 0)
        vxt_ref[V_DIM:, :] = jnp.where(row == 0, 1.0, 0.0).astype(BF16)

        dim = lax.broadcasted_iota(jnp.int32, (LANES, LANES), 0)
        col = lax.broadcasted_iota(jnp.int32, (LANES, LANES), 1)
        ind = jnp.where(col == jnp.where(dim < QK_DIM, 0, 1), 1.0, 0.0).astype(BF16)

        def sq_norms(x):
            xf = x.astype(F32)
            return _dot((xf * xf).astype(BF16), ind)

        kmax = jnp.max(sq_norms(kx_ref[...]), axis=0, keepdims=True)
        qn = sq_norms(qall_ref[0]).T
        bound = jnp.sqrt(qn[0:HALO_ROWS] * jnp.concatenate(
            [kmax[:, 0:1], kmax[:, 1:2], jnp.zeros((HALO_ROWS - 2, 1), F32)], axis=0)) * 1.05
        shift_ref[...] = bound
        safe_ref[0] = (jnp.max(bound) <= SAFE_SHIFT).astype(jnp.int32)

    lp = lam_ref[...]
    lam = (jnp.exp(jnp.sum(lp[0:1] * lp[1:2], keepdims=True))
           - jnp.exp(jnp.sum(lp[2:3] * lp[3:4], keepdims=True)) + lam_init)
    q = q_ref[0]
    lane = lax.broadcasted_iota(jnp.int32, q.shape, 1)
    zero = jnp.zeros_like(q)
    qms = [jnp.where(in_map, q, zero) for in_map in (lane < QK_DIM, lane >= QK_DIM)]
    q0 = pl.multiple_of(pl.program_id(2) * tq, tq)
    shifts = [shift_ref[mi:mi + 1, pl.ds(q0, tq)] for mi in range(2)]
    bound_is_safe = safe_ref[0] == 1

    @pl.when(bound_is_safe)
    def _():
        for mi in range(2):
            st = lax.dot_general(kx_ref[...], qms[mi], NT_DIMS, preferred_element_type=F32)
            e_ref[mi] = jnp.exp2(st - shifts[mi]).astype(BF16)
            ox_ref[mi] = _dot(vxt_ref[...], e_ref[mi])

    @pl.when(jnp.logical_not(bound_is_safe))
    def _():
        for mi in range(2):
            s_ref[mi] = lax.dot_general(kx_ref[...], qms[mi], NT_DIMS, preferred_element_type=F32)
        for mi in range(2):
            e_ref[mi] = jnp.exp2(s_ref[mi] - jnp.max(s_ref[mi], axis=0, keepdims=True)).astype(BF16)
            ox_ref[mi] = _dot(vxt_ref[...], e_ref[mi])

    o1, l1 = ox_ref[0, :V_DIM, :], ox_ref[0, V_DIM:V_DIM + 1, :]
    o2, l2 = ox_ref[1, :V_DIM, :], ox_ref[1, V_DIM:V_DIM + 1, :]
    ot = o1 * (1.0 / l1) - o2 * (lam / l2)
    ot = ot * lax.rsqrt(jnp.mean(ot * ot, axis=0, keepdims=True) + NORM_EPS)
    o_ref[0] = (ot * sub_ref[...] * (1.0 - lam_init)).T.astype(BF16)


def _attention(q, kc, k, vct, vt, lam_p, subln, lam_init, tq):
    b, n, _ = q.shape
    m_ctx = kc.shape[1]
    head_rows = lambda rows: pl.BlockSpec((1, rows, V_DIM), lambda bi, h, i: (bi, 0, h))
    head_cols = lambda cols: pl.BlockSpec((1, V_DIM, cols), lambda bi, h, i: (bi, h, 0))
    return pl.pallas_call(
        functools.partial(_attn_kernel, lam_init=lam_init, m_ctx=m_ctx),
        grid=(b, ATTN_HEADS, n // tq),
        in_specs=[
            pl.BlockSpec((4, QK_DIM), lambda bi, h, i: (0, 0)),
            pl.BlockSpec((1, tq, V_DIM), lambda bi, h, i: (bi, i, h)),
            head_rows(n), head_rows(m_ctx), head_rows(n), head_cols(m_ctx), head_cols(n),
            pl.BlockSpec((V_DIM, 1), lambda bi, h, i: (0, 0)),
        ],
        out_specs=pl.BlockSpec((1, tq, V_DIM), lambda bi, h, i: (bi, i, h)),
        out_shape=jax.ShapeDtypeStruct((b, n, Q_W), BF16),
        scratch_shapes=[pltpu.VMEM((m_ctx + n, V_DIM), BF16), pltpu.VMEM((PV_ROWS, m_ctx + n), BF16),
                        pltpu.VMEM((HALO_ROWS, n), F32), pltpu.SMEM((1,), jnp.int32),
                        pltpu.VMEM((2, m_ctx + n, tq), F32), pltpu.VMEM((2, m_ctx + n, tq), BF16),
                        pltpu.VMEM((2, PV_ROWS, tq), F32)],
        compiler_params=_params("arbitrary", "arbitrary", "arbitrary"),
        name="diff_attention",
    )(lam_p, q, q, kc, k, vct, vt, subln)


def _tail(y, x, m, n2, wr_ref, x1_ref, h2_ref, pt_ref):
    x1 = x + m[2:3] * y
    x1_ref[0] = x1
    h2 = _rms_mod(x1, n2, m[3:4], m[4:5])
    h_hi, h_lo = _split_bf16(h2)
    h2_ref[0] = h_hi
    hcat = jnp.concatenate([h_hi, h_lo], axis=1)
    half = hcat.shape[0] // 2
    parts = [_dot(hcat[r0:r0 + half], wr_ref[...]) for r0 in (0, half)]
    prod = jnp.concatenate(parts, axis=0)
    logits = (prod[:, :LANES] + prod[:, LANES:]).T[:N_EXPERTS]
    ex = jnp.exp(logits - jnp.max(logits, axis=0, keepdims=True))
    pt_ref[0] = ex / jnp.sum(ex, axis=0, keepdims=True)


def _tail_specs(b, n, d, tm):
    in_specs = [
        pl.BlockSpec((1, d), lambda bi, i, *_: (0, 0)),
        pl.BlockSpec((2 * d, 2 * LANES), lambda bi, i, *_: (0, 0)),
    ]
    out_specs = [
        pl.BlockSpec((1, tm, d), lambda bi, i, *_: (bi, i, 0)),
        pl.BlockSpec((1, tm, d), lambda bi, i, *_: (bi, i, 0)),
        pl.BlockSpec((1, N_EXPERTS, tm), lambda bi, i, *_: (bi, 0, i)),
    ]
    out_shape = [
        jax.ShapeDtypeStruct((b, n, d), F32),
        jax.ShapeDtypeStruct((b, n, d), BF16),
        jax.ShapeDtypeStruct((b, N_EXPERTS, n), F32),
    ]
    return in_specs, out_specs, out_shape


def _outproj_kernel(a_ref, c_ref, x_ref, mod_ref, wo_ref, n2_ref, wr_ref,
                    x1_ref, h2_ref, pt_ref, mix_ref):
    mix_ref[:, :Q_W] = a_ref[0]
    mix_ref[:, Q_W:] = c_ref[0]
    y = _dot(mix_ref[...], wo_ref[...])
    _tail(y, x_ref[0], mod_ref[0], n2_ref[...], wr_ref, x1_ref, h2_ref, pt_ref)


def _out_projection(attn, conv, x, mod0, w_out_bf, norm2, wr_cat, tm):
    b, n, d = x.shape
    tail_in, out_specs, out_shape = _tail_specs(b, n, d, tm)
    return pl.pallas_call(
        _outproj_kernel,
        grid=(b, n // tm),
        in_specs=[
            pl.BlockSpec((1, tm, Q_W), lambda bi, i: (bi, i, 0)),
            pl.BlockSpec((1, tm, CONV_W), lambda bi, i: (bi, i, 0)),
            pl.BlockSpec((1, tm, d), lambda bi, i: (bi, i, 0)),
            pl.BlockSpec((1, 6, d), lambda bi, i: (bi, 0, 0)),
            pl.BlockSpec((d, d), lambda bi, i: (0, 0)),
        ] + tail_in,
        out_specs=out_specs,
        out_shape=out_shape,
        scratch_shapes=[pltpu.VMEM((tm, d), BF16)],
        compiler_params=_params("arbitrary", "arbitrary"),
        name="out_projection",
    )(attn, conv, x, mod0, w_out_bf, norm2, wr_cat)


def _gelu(x):
    return 0.5 * x * (1.0 + lax.erf(x * np.float32(1.0 / math.sqrt(2.0))))


def _cmlp_body(x, m, n1_ref, w1_ref, vn_ref, ws_ref, bs_ref, wo_ref, n2_ref, wr_ref,
               x1_ref, h2_ref, pt_ref, mix_ref):
    tm, d = x.shape
    h = _rms_mod(x, n1_ref[...], m[0:1], m[1:2]).astype(BF16)
    p = _gelu(_dot(h, w1_ref[...]))
    u, v = p[:, :d], p[:, d:]
    v = v * lax.rsqrt(jnp.mean(v * v, axis=-1, keepdims=True) + NORM_EPS) * vn_ref[...]
    vb = v.astype(BF16)
    gw = d // CMLP_GROUPS
    bs = bs_ref[...]
    for c in range(tm // CHUNK):
        r0 = c * CHUNK
        for g in range(CMLP_GROUPS):
            s = _dot(ws_ref[g], vb[r0:r0 + CHUNK, g * gw:(g + 1) * gw]) + bs[:, g:g + 1]
            mix_ref[r0:r0 + CHUNK, g * gw:(g + 1) * gw] = (u[r0:r0 + CHUNK, g * gw:(g + 1) * gw] * s).astype(BF16)
    y = _dot(mix_ref[...], wo_ref[...])
    _tail(y, x, m, n2_ref[...], wr_ref, x1_ref, h2_ref, pt_ref)


def _route_kernel(p_ref, sel_ref, start_ref, fits_ref, *, cap, pchunk, window):
    rows, n = p_ref.shape

    def bit_step(it, t):
        cand = t | jnp.left_shift(jnp.int32(1), 30 - it)
        cf = lax.bitcast_convert_type(cand, F32)
        cnt = jnp.sum(jnp.where(p_ref[...] >= cf, 1.0, 0.0), axis=1, keepdims=True)
        return jnp.where(cnt >= cap, cand, t)

    t = lax.fori_loop(0, 31, bit_step, jnp.zeros((rows, 1), jnp.int32))
    tf = lax.bitcast_convert_type(t, F32)
    p = p_ref[...]
    gt = p > tf
    eq = jnp.logical_and(p >= tf, jnp.logical_not(gt))
    need = cap - jnp.sum(jnp.where(gt, 1.0, 0.0), axis=1, keepdims=True)

    ri = lax.broadcasted_iota(jnp.int32, (pchunk, pchunk), 0)
    ci = lax.broadcasted_iota(jnp.int32, (pchunk, pchunk), 1)
    upper = jnp.where(ri < ci, 1.0, 0.0).astype(BF16)

    def prefix(mask):
        mf = jnp.where(mask, 1.0, 0.0)
        carry = jnp.zeros((rows, 1), F32)
        parts, before, inside = [], [], []
        for c in range(n // pchunk):
            blk = mf[:, c * pchunk:(c + 1) * pchunk]
            parts.append(_dot(blk.astype(BF16), upper) + carry)
            total = jnp.sum(blk, axis=1, keepdims=True)
            before.append(carry)
            inside.append(total)
            carry = carry + total
        return jnp.concatenate(parts, axis=1), jnp.concatenate(before, axis=1), jnp.concatenate(inside, axis=1)

    chosen = jnp.logical_or(gt, jnp.logical_and(eq, prefix(eq)[0] < need))
    rank, before, inside = prefix(chosen)
    sel_ref[...] = jnp.where(chosen, rank, -1.0)
    start = jnp.minimum(jnp.floor(before * (1.0 / ALIGN_ROWS)) * ALIGN_ROWS, float(cap - window))
    start_ref[...] = start.astype(jnp.int32)
    fits_ref[...] = jnp.where(before + inside <= start + window, 1, 0).astype(jnp.int32)


def _route(pt, cap, window):
    b, e, n = pt.shape
    rows = b * e
    pchunk = min(SUB_TOKENS, n)
    n_chunks = n // pchunk
    full = lambda cols: pl.BlockSpec((rows, cols), lambda i: (0, 0))
    sel, start, fits = pl.pallas_call(
        functools.partial(_route_kernel, cap=cap, pchunk=pchunk, window=window),
        grid=(1,),
        in_specs=[full(n)],
        out_specs=[full(n), full(n_chunks), full(n_chunks)],
        out_shape=[jax.ShapeDtypeStruct((rows, n), F32), jax.ShapeDtypeStruct((rows, n_chunks), jnp.int32),
                   jax.ShapeDtypeStruct((rows, n_chunks), jnp.int32)],
        compiler_params=_params("arbitrary"),
        name="route",
    )(pt.reshape(rows, n))
    return sel.reshape(b, e, n), start.reshape(-1), fits.reshape(-1)


def _gather_kernel(start_ref, fits_ref, sel_ref, pt_ref, h_ref, xs_ref, vals_ref, *, cap, window, sub, n_chunks):
    bi = pl.program_id(0)
    per_pass = sub // window
    entry = lambda e, c: (bi * N_EXPERTS + e) * n_chunks + c
    lane = lax.broadcasted_iota(jnp.int32, (1, LANES), 1)

    all_fit = fits_ref[entry(0, 0)]
    for e in range(N_EXPERTS):
        for c in range(n_chunks):
            all_fit = jnp.minimum(all_fit, fits_ref[entry(e, c)])

    vals_ref[...] = jnp.zeros(vals_ref.shape, F32)

    @pl.when(all_fit == 1)
    def _():
        xs_ref[...] = jnp.zeros(xs_ref.shape, BF16)
        offs = lax.broadcasted_iota(jnp.int32, (window, sub), 0)
        for c in range(n_chunks):
            sel = sel_ref[0, :, c * sub:(c + 1) * sub]
            prob = pt_ref[0, :, c * sub:(c + 1) * sub]
            hc = h_ref[0, c * sub:(c + 1) * sub, :]
            for g in range(N_EXPERTS // per_pass):
                group = range(g * per_pass, (g + 1) * per_pass)
                starts = [pl.multiple_of(start_ref[entry(e, c)], ALIGN_ROWS) for e in group]
                hits = [sel[e:e + 1, :] == (offs + a).astype(F32) for e, a in zip(group, starts)]
                hots = [jnp.where(hit, 1.0, 0.0).astype(BF16) for hit in hits]
                z = _dot(jnp.concatenate(hots, axis=0), hc).astype(BF16)
                for k, (e, a) in enumerate(zip(group, starts)):
                    xs_ref[0, e, pl.ds(a, window), :] += z[k * window:(k + 1) * window]
                    w = jnp.sum(jnp.where(hits[k], prob[e:e + 1, :], 0.0), axis=1, keepdims=True)
                    vals_ref[0, pl.ds(a, window), :] += jnp.where(lane == e, w, 0.0)

    @pl.when(all_fit != 1)
    def _():
        n = sel_ref.shape[-1]
        rank = lax.broadcasted_iota(jnp.int32, (cap, n), 0).astype(F32)
        for e in range(N_EXPERTS):
            hit = sel_ref[0, e:e + 1, :] == rank
            xs_ref[0, e] = _dot(jnp.where(hit, 1.0, 0.0).astype(BF16), h_ref[0]).astype(BF16)
            w = jnp.sum(jnp.where(hit, pt_ref[0, e:e + 1, :], 0.0), axis=1, keepdims=True)
            vals_ref[0] += jnp.where(lane == e, w, 0.0)


def _gather(route, pt, h2):
    b, n, d = h2.shape
    cap = route.geom["cap"]
    experts_by_tokens = pl.BlockSpec((1, N_EXPERTS, n), lambda bi, *_: (bi, 0, 0))
    return pl.pallas_call(
        functools.partial(_gather_kernel, **route.geom),
        grid_spec=pltpu.PrefetchScalarGridSpec(
            num_scalar_prefetch=2,
            grid=(b,),
            in_specs=[experts_by_tokens, experts_by_tokens, pl.BlockSpec((1, n, d), lambda bi, *_: (bi, 0, 0))],
            out_specs=[pl.BlockSpec((1, N_EXPERTS, cap, d), lambda bi, *_: (bi, 0, 0, 0)),
                       pl.BlockSpec((1, cap, LANES), lambda bi, *_: (bi, 0, 0))],
        ),
        out_shape=[jax.ShapeDtypeStruct((b, N_EXPERTS, cap, d), BF16), jax.ShapeDtypeStruct((b, cap, LANES), F32)],
        compiler_params=_params("arbitrary"),
        name="moe_gather",
    )(route.start, route.fits, route.sel, pt, h2)


def _expert_kernel(vals_ref, xs_ref, wg_ref, wu_ref, wd_ref, y_ref, wg_s, wu_s, wd_s):
    e = pl.program_id(0)

    @pl.when(pl.program_id(1) == 0)
    def _():
        wg_s[...] = wg_ref[0, 0].astype(BF16)
        wu_s[...] = wu_ref[0, 0].astype(BF16)
        wd_s[...] = wd_ref[0, 0].astype(BF16)

    bb, cap, _ = vals_ref.shape
    d = xs_ref.shape[-1]
    lane = lax.broadcasted_iota(jnp.int32, (1, 1, LANES), 2)
    vals = jnp.sum(jnp.where(lane == e, vals_ref[...], 0.0), axis=2, keepdims=True).reshape(bb * cap, 1)
    xs = xs_ref[...].reshape(bb * cap, d)
    a = _dot(xs, wg_s[...])
    bm = _dot(xs, wu_s[...])
    hid = (a * jax.nn.sigmoid(a) * bm * vals).astype(BF16)
    y_ref[...] = _dot(hid, wd_s[...]).astype(BF16).reshape(y_ref.shape)


def _expert_ffn(vals, xs, w_gate, w_up, w_down, layer):
    b, _, cap, d = xs.shape
    hdim = w_gate.shape[-1]
    bb = next(k for k in (4, 2, 1) if b % k == 0)
    w_spec = lambda rows, cols: pl.BlockSpec((1, 1, rows, cols), lambda e, bi: (layer, e, 0, 0))
    return pl.pallas_call(
        _expert_kernel,
        grid=(N_EXPERTS, b // bb),
        in_specs=[
            pl.BlockSpec((bb, cap, LANES), lambda e, bi: (bi, 0, 0)),
            pl.BlockSpec((bb, 1, cap, d), lambda e, bi: (bi, e, 0, 0)),
            w_spec(d, hdim), w_spec(d, hdim), w_spec(hdim, d),
        ],
        out_specs=pl.BlockSpec((bb, 1, cap, d), lambda e, bi: (bi, e, 0, 0)),
        out_shape=jax.ShapeDtypeStruct((b, N_EXPERTS, cap, d), BF16),
        scratch_shapes=[pltpu.VMEM((d, hdim), BF16), pltpu.VMEM((d, hdim), BF16), pltpu.VMEM((hdim, d), BF16)],
        compiler_params=_params("arbitrary", "arbitrary"),
        name="expert_ffn",
    )(vals, xs, w_gate, w_up, w_down)


def _scatter_add(start_ref, fits_ref, sel_ref, y_ref, acc_ref, *, cap, window, sub, n_chunks):
    bi, i = pl.program_id(0), pl.program_id(1)
    tm = sel_ref.shape[-1]
    n_sub = tm // sub
    per_pass = sub // window
    entry = lambda e, j: (bi * N_EXPERTS + e) * n_chunks + i * n_sub + j

    all_fit = fits_ref[entry(0, 0)]
    for e in range(N_EXPERTS):
        for j in range(n_sub):
            all_fit = jnp.minimum(all_fit, fits_ref[entry(e, j)])

    @pl.when(all_fit == 1)
    def _():
        offs = lax.broadcasted_iota(jnp.int32, (window, sub), 0)
        for j in range(n_sub):
            sel = sel_ref[0, :, j * sub:(j + 1) * sub]
            acc = jnp.zeros((sub, acc_ref.shape[1]), F32)
            for g in range(N_EXPERTS // per_pass):
                hots, ys = [], []
                for e in range(g * per_pass, (g + 1) * per_pass):
                    a = pl.multiple_of(start_ref[entry(e, j)], ALIGN_ROWS)
                    hots.append(jnp.where(sel[e:e + 1, :] == (offs + a).astype(F32), 1.0, 0.0).astype(BF16))
                    ys.append(y_ref[0, e, pl.ds(a, window), :])
                acc = acc + lax.dot_general(jnp.concatenate(hots, axis=0), jnp.concatenate(ys, axis=0),
                                            TN_DIMS, preferred_element_type=F32)
            acc_ref[j * sub:(j + 1) * sub, :] = acc

    @pl.when(all_fit != 1)
    def _():
        sel = sel_ref[0]
        rank = lax.broadcasted_iota(jnp.int32, (cap, tm), 0).astype(F32)
        acc = jnp.zeros(acc_ref.shape, F32)
        for e in range(N_EXPERTS):
            onehot = jnp.where(sel[e:e + 1, :] == rank, 1.0, 0.0).astype(BF16)
            acc = acc + lax.dot_general(onehot, y_ref[0, e], TN_DIMS, preferred_element_type=F32)
        acc_ref[...] = acc


def _combine_final_kernel(start_ref, fits_ref, sel_ref, y_ref, x_ref, mod_ref, fn_ref, o_ref, acc_ref, **geom):
    _scatter_add(start_ref, fits_ref, sel_ref, y_ref, acc_ref, **geom)
    x2 = x_ref[0] + mod_ref[0][5:6] * acc_ref[...]
    o_ref[0] = x2 * lax.rsqrt(jnp.mean(x2 * x2, axis=-1, keepdims=True) + NORM_EPS) * fn_ref[...]


def _combine_cmlp_kernel(start_ref, fits_ref, sel_ref, y_ref, x_ref, mod0_ref, mod1_ref, n1_ref, w1_ref, vn_ref,
                         ws_ref, bs_ref, wo_ref, n2_ref, wr_ref, x1_ref, h2_ref, pt_ref, acc_ref, mix_ref, **geom):
    _scatter_add(start_ref, fits_ref, sel_ref, y_ref, acc_ref, **geom)
    x2 = x_ref[0] + mod0_ref[0][5:6] * acc_ref[...]
    _cmlp_body(x2, mod1_ref[0], n1_ref, w1_ref, vn_ref, ws_ref, bs_ref, wo_ref, n2_ref, wr_ref,
               x1_ref, h2_ref, pt_ref, mix_ref)


def _combine_specs(b, n, d, cap, tm):
    return [
        pl.BlockSpec((1, N_EXPERTS, tm), lambda bi, i, *_: (bi, 0, i)),
        pl.BlockSpec((1, N_EXPERTS, cap, d), lambda bi, i, *_: (bi, 0, 0, 0)),
        pl.BlockSpec((1, tm, d), lambda bi, i, *_: (bi, i, 0)),
        pl.BlockSpec((1, 6, d), lambda bi, i, *_: (bi, 0, 0)),
    ]


def _combine_final(route, y, x1, mod_l, final_norm, tm):
    b, n, d = x1.shape
    return pl.pallas_call(
        functools.partial(_combine_final_kernel, **route.geom),
        grid_spec=pltpu.PrefetchScalarGridSpec(
            num_scalar_prefetch=2,
            grid=(b, n // tm),
            in_specs=_combine_specs(b, n, d, route.geom["cap"], tm) + [pl.BlockSpec((1, d), lambda bi, i, *_: (0, 0))],
            out_specs=pl.BlockSpec((1, tm, d), lambda bi, i, *_: (bi, i, 0)),
            scratch_shapes=[pltpu.VMEM((tm, d), F32)],
        ),
        out_shape=jax.ShapeDtypeStruct((b, n, d), F32),
        compiler_params=_params("arbitrary", "arbitrary"),
        name="moe_combine_final",
    )(route.start, route.fits, route.sel, y, x1, mod_l, final_norm)


def _combine_chunk_mlp(route, y, x1, mod0, mod1, norm1, w1_bf, v_norm, ws_bf, bs_t, w_out_bf, norm2, wr_cat, tm):
    b, n, d = x1.shape
    const = lambda *shape: pl.BlockSpec(shape, lambda bi, i, *_: (0,) * len(shape))
    tail_in, out_specs, out_shape = _tail_specs(b, n, d, tm)
    return pl.pallas_call(
        functools.partial(_combine_cmlp_kernel, **route.geom),
        grid_spec=pltpu.PrefetchScalarGridSpec(
            num_scalar_prefetch=2,
            grid=(b, n // tm),
            in_specs=_combine_specs(b, n, d, route.geom["cap"], tm) + [
                pl.BlockSpec((1, 6, d), lambda bi, i, *_: (bi, 0, 0)),
                const(1, d), const(d, 2 * d), const(1, d), const(CMLP_GROUPS, CHUNK, CHUNK),
                const(CHUNK, CMLP_GROUPS), const(d, d),
            ] + tail_in,
            out_specs=out_specs,
            scratch_shapes=[pltpu.VMEM((tm, d), F32), pltpu.VMEM((tm, d), BF16)],
        ),
        out_shape=out_shape,
        compiler_params=_params("arbitrary", "arbitrary"),
        name="combine_chunk_mlp",
    )(route.start, route.fits, route.sel, y, x1, mod0, mod1, norm1, w1_bf, v_norm, ws_bf, bs_t, w_out_bf,
      norm2, wr_cat)


class _Routing(NamedTuple):
    sel: jax.Array
    start: jax.Array
    fits: jax.Array
    geom: dict


def _moe_experts(h2, pt, w_gate, w_up, w_down, layer):
    n = h2.shape[1]
    cap = CAPACITY_FACTOR * n // N_EXPERTS
    window = min(SLOT_WINDOW, cap)
    sub = min(SUB_TOKENS, n)
    sel, start, fits = _route(pt, cap, window)
    route = _Routing(sel, start, fits, dict(cap=cap, window=window, sub=sub, n_chunks=n // sub))
    xs, vals = _gather(route, pt, h2)
    return route, _expert_ffn(vals, xs, w_gate, w_up, w_down, layer)


def kernel(x, c, ctx, c_ctx, w_mod, b_mod, norm1, norm2, even_w_in, even_lambda, even_subln, even_conv_w,
           odd_w_in, odd_v_norm, odd_w_s, odd_b_s, w_out, w_router, w_gate, w_up, w_down, final_norm):
    b, n, d = x.shape
    depth = w_mod.shape[0]
    assert d == D_MODEL and depth == 2 and n % 256 == 0
    t_proj = min(1024, n)
    t_attn = min(1024, n)
    t_moe = min(512, n)
    t_final = min(1024, n)

    cc = jnp.concatenate([c, c_ctx[None, :]], axis=0)
    mod = _modulation(cc, w_mod, b_mod).reshape(depth, b + 1, 6, d)
    row = lambda a, l: a[l].reshape(1, -1)
    wr_pad = jnp.pad(w_router, ((0, 0), (0, 0), (0, LANES - N_EXPERTS)))
    wr_hi = wr_pad.astype(BF16)
    wr_lo = (wr_pad - wr_hi.astype(F32)).astype(BF16)
    wr_cat = jnp.concatenate([jnp.concatenate([wr_hi, wr_lo], axis=2),
                              jnp.concatenate([wr_hi, jnp.zeros_like(wr_lo)], axis=2)], axis=1)
    w_out_bf = w_out.astype(BF16)
    fnorm = final_norm.reshape(1, d)

    mod0, modc = mod[0, :b], mod[0, b:]
    w_in_bf = even_w_in[0].astype(BF16)
    q, k, vt, conv = _in_projection(x, mod0, row(norm1, 0), w_in_bf, even_conv_w[0], t_proj)
    kc, vct = _context_kv(ctx, modc, row(norm1, 0), w_in_bf)
    lam_init = 0.8 - 0.6 * math.exp(-0.3 * 0)
    attn = _attention(q, kc, k, vct, vt, even_lambda[0], even_subln[0].reshape(-1, 1), lam_init, t_attn)
    x1, h2, pt = _out_projection(attn, conv, x, mod0, w_out_bf[0], row(norm2, 0), wr_cat[0], t_proj)
    route, y = _moe_experts(h2, pt, w_gate, w_up, w_down, 0)

    mod1 = mod[1, :b]
    x1, h2, pt = _combine_chunk_mlp(route, y, x1, mod0, mod1, row(norm1, 1), odd_w_in[0].astype(BF16),
                                    row(odd_v_norm, 0), odd_w_s[0].astype(BF16), odd_b_s[0].T, w_out_bf[1],
                                    row(norm2, 1), wr_cat[1], t_moe)
    route, y = _moe_experts(h2, pt, w_gate, w_up, w_down, 1)
    return _combine_final(route, y, x1, mod1, fnorm, t_final)
```

```python
import functools
import math
from typing import NamedTuple

import numpy as np
import jax
import jax.numpy as jnp
from jax import lax
from jax.experimental import pallas as pl
from jax.experimental.pallas import tpu as pltpu

F32 = jnp.float32
BF16 = jnp.bfloat16

D_MODEL = 1024
ATTN_HEADS = 4
QK_DIM = 64
V_DIM = 128
Q_W = ATTN_HEADS * 2 * QK_DIM
CONV_W = 512
EVEN_IN = 3072
N_EXPERTS = 16
CAPACITY_FACTOR = 2
GRID_W = 64
CHUNK = 128
CMLP_GROUPS = 4
ROPE_THETA = 10000.0
NORM_EPS = 1e-6
LANES = 128
HALO_ROWS = 8
VMEM_LIMIT = 56 * 1024 * 1024
PV_ROWS = V_DIM + 16
SAFE_SHIFT = 60.0
ALIGN_ROWS = 16
SUB_TOKENS = 256
SLOT_WINDOW = 64

NT_DIMS = (((1,), (1,)), ((), ()))
TN_DIMS = (((0,), (0,)), ((), ()))


def _dot(a, b):
    return jnp.dot(a, b, preferred_element_type=F32)


def _params(*sem):
    return pltpu.CompilerParams(dimension_semantics=sem, vmem_limit_bytes=VMEM_LIMIT)


def _rms_mod(x, g, shift, scale):
    y = x * lax.rsqrt(jnp.mean(x * x, axis=-1, keepdims=True) + NORM_EPS)
    return (y * g) * (1.0 + scale) + shift


def _split_bf16(x):
    hi = x.astype(BF16)
    lo = (x - hi.astype(F32)).astype(BF16)
    return hi, lo


def _mod_kernel(c_ref, w_ref, b_ref, o_ref):
    c = c_ref[...]
    a_hi, a_lo = _split_bf16(c * jax.nn.sigmoid(c))
    w_hi, w_lo = _split_bf16(w_ref[0])
    o_ref[0] = _dot(a_hi, w_hi) + _dot(a_lo, w_hi) + _dot(a_hi, w_lo) + b_ref[0]


def _modulation(cc, w_mod, b_mod):
    depth, d, six_d = w_mod.shape
    rows = cc.shape[0]
    tn = 1536
    return pl.pallas_call(
        _mod_kernel,
        grid=(depth, six_d // tn),
        in_specs=[
            pl.BlockSpec((rows, d), lambda l, j: (0, 0)),
            pl.BlockSpec((1, d, tn), lambda l, j: (l, 0, j)),
            pl.BlockSpec((1, 1, tn), lambda l, j: (l, 0, j)),
        ],
        out_specs=pl.BlockSpec((1, rows, tn), lambda l, j: (l, 0, j)),
        out_shape=jax.ShapeDtypeStruct((depth, rows, six_d), F32),
        compiler_params=_params("arbitrary", "arbitrary"),
        name="modulation",
    )(cc, w_mod, b_mod.reshape(depth, 1, six_d))


def _inproj_kernel(x_ref, xp_ref, xn_ref, mod_ref, n1_ref, w_ref, cos_ref, sa_ref, sb_ref, cw_ref,
                   q_ref, k_ref, vt_ref, c_ref, *, tm, n_tiles):
    i = pl.program_id(1)
    m = mod_ref[0]
    shift, scale = m[0:1], m[1:2]
    g = n1_ref[...]
    h = _rms_mod(x_ref[0], g, shift, scale).astype(BF16)

    cos, sa, sb = cos_ref[...], sa_ref[...], sb_ref[...]
    for col0, out_ref, qscale in ((0, q_ref, QK_DIM ** -0.5 * math.log2(math.e)), (Q_W, k_ref, 1.0)):
        p = _dot(h, w_ref[:, col0:col0 + Q_W])
        for j in range(Q_W // LANES):
            pj = p[:, j * LANES:(j + 1) * LANES]
            r = pj * cos + pltpu.roll(pj, LANES - 16, 1) * sa + pltpu.roll(pj, 16, 1) * sb
            out_ref[0, :, j * LANES:(j + 1) * LANES] = (r * qscale).astype(BF16)

    vt_ref[0] = _dot(h, w_ref[:, 2 * Q_W:2 * Q_W + 512]).T.astype(BF16)

    pc = _dot(h, w_ref[:, 2 * Q_W + 512:])
    gb = pc[:, :CONV_W]
    u = pc[:, CONV_W:2 * CONV_W] * pc[:, 2 * CONV_W:]
    xh = jnp.concatenate([xp_ref[0], xn_ref[0]], axis=0)
    hh = _rms_mod(xh, g, shift, scale).astype(BF16)
    ph = _dot(hh, w_ref[:, 2 * Q_W + 512 + CONV_W:])
    uh = ph[:, :CONV_W] * ph[:, CONV_W:]
    u_before = jnp.where(i > 0, uh[HALO_ROWS - 1:HALO_ROWS], 0.0)
    u_after = jnp.where(i < n_tiles - 1, uh[HALO_ROWS:HALO_ROWS + 1], 0.0)
    row = lax.broadcasted_iota(jnp.int32, (tm, 1), 0)
    u_prev = jnp.where(row == 0, u_before, pltpu.roll(u, 1, 0))
    u_next = jnp.where(row == tm - 1, u_after, pltpu.roll(u, tm - 1, 0))
    cw = cw_ref[...]
    conv = cw[0:1] * u_prev + cw[1:2] * u + cw[2:3] * u_next
    c_ref[0] = (gb * conv).astype(BF16)


def _rope_tables(n):
    rows = n // GRID_W
    row = jnp.repeat(jnp.arange(rows), GRID_W).astype(F32)
    col = jnp.tile(jnp.arange(GRID_W), rows).astype(F32)
    half = QK_DIM // 2
    inv = 1.0 / (ROPE_THETA ** (jnp.arange(0, half, 2, dtype=F32) / half))
    ang_r = row[:, None] * inv
    ang_c = col[:, None] * inv
    ang = jnp.concatenate([ang_r, ang_r, ang_c, ang_c], axis=-1)
    cos, sin = jnp.cos(ang), jnp.sin(ang)
    first_half = (jnp.arange(QK_DIM) % 32) < 16
    sin_a = jnp.where(first_half, -sin, 0.0)
    sin_b = jnp.where(first_half, 0.0, sin)
    tile2 = lambda t: jnp.concatenate([t, t], axis=-1)
    return tile2(cos), tile2(sin_a), tile2(sin_b)


def _in_projection(x, mod0, norm1, w_in_bf, conv_w, tm):
    b, n, d = x.shape
    n_tiles = n // tm
    hb = tm // HALO_ROWS
    n_hblocks = n // HALO_ROWS
    cos, sa, sb = _rope_tables(n)
    out = jax.ShapeDtypeStruct((b, n, Q_W), BF16)
    row_spec = pl.BlockSpec((1, tm, Q_W), lambda bi, i: (bi, i, 0))
    tab_spec = pl.BlockSpec((tm, LANES), lambda bi, i: (i, 0))
    return pl.pallas_call(
        functools.partial(_inproj_kernel, tm=tm, n_tiles=n_tiles),
        grid=(b, n_tiles),
        in_specs=[
            pl.BlockSpec((1, tm, d), lambda bi, i: (bi, i, 0)),
            pl.BlockSpec((1, HALO_ROWS, d), lambda bi, i: (bi, jnp.maximum(i * hb - 1, 0), 0)),
            pl.BlockSpec((1, HALO_ROWS, d), lambda bi, i: (bi, jnp.minimum((i + 1) * hb, n_hblocks - 1), 0)),
            pl.BlockSpec((1, 6, d), lambda bi, i: (bi, 0, 0)),
            pl.BlockSpec((1, d), lambda bi, i: (0, 0)),
            pl.BlockSpec((d, EVEN_IN), lambda bi, i: (0, 0)),
            tab_spec, tab_spec, tab_spec,
            pl.BlockSpec((3, CONV_W), lambda bi, i: (0, 0)),
        ],
        out_specs=[row_spec, row_spec, pl.BlockSpec((1, Q_W, tm), lambda bi, i: (bi, 0, i)), row_spec],
        out_shape=[out, out, jax.ShapeDtypeStruct((b, Q_W, n), BF16), out],
        compiler_params=_params("arbitrary", "arbitrary"),
        name="in_projection",
    )(x, x, x, mod0, norm1, w_in_bf, cos, sa, sb, conv_w)


def _ctxkv_kernel(x_ref, mod_ref, n1_ref, wk_ref, wv_ref, k_ref, vt_ref):
    bb, m_ctx, d = x_ref.shape
    m = mod_ref[0]
    h = _rms_mod(x_ref[...].reshape(bb * m_ctx, d), n1_ref[...], m[0:1], m[1:2]).astype(BF16)
    k_ref[...] = _dot(h, wk_ref[...]).astype(BF16).reshape(k_ref.shape)
    v = _dot(h, wv_ref[...])
    for j in range(bb):
        vt_ref[j] = v[j * m_ctx:(j + 1) * m_ctx].T.astype(BF16)


def _context_kv(ctx, modc, norm1, w_in_bf):
    b, m, d = ctx.shape
    bb = next(k for k in (4, 2, 1) if b % k == 0)
    out = jax.ShapeDtypeStruct((b, m, Q_W), BF16)
    return pl.pallas_call(
        _ctxkv_kernel,
        grid=(b // bb,),
        in_specs=[
            pl.BlockSpec((bb, m, d), lambda bi: (bi, 0, 0)),
            pl.BlockSpec((1, 6, d), lambda bi: (0, 0, 0)),
            pl.BlockSpec((1, d), lambda bi: (0, 0)),
            pl.BlockSpec((d, Q_W), lambda bi: (0, 1)),
            pl.BlockSpec((d, Q_W), lambda bi: (0, 2)),
        ],
        out_specs=[pl.BlockSpec((bb, m, Q_W), lambda bi: (bi, 0, 0)), pl.BlockSpec((bb, Q_W, m), lambda bi: (bi, 0, 0))],
        out_shape=[out, jax.ShapeDtypeStruct((b, Q_W, m), BF16)],
        compiler_params=_params("arbitrary"),
        name="context_kv",
    )(ctx, modc, norm1, w_in_bf, w_in_bf)


def _attn_kernel(lam_ref, q_ref, qall_ref, kc_ref, k_ref, vct_ref, vt_ref, sub_ref, o_ref,
                 kx_ref, vxt_ref, shift_ref, safe_ref, s_ref, e_ref, ox_ref, *, lam_init, m_ctx):
    nk = kx_ref.shape[0]
    tq = q_ref.shape[1]

    @pl.when(pl.program_id(2) == 0)
    def _():
        kx_ref[:m_ctx, :] = kc_ref[0]
        kx_ref[m_ctx:, :] = k_ref[0]
        vxt_ref[:V_DIM, :m_ctx] = vct_ref[0]
        vxt_ref[:V_DIM, m_ctx:] = vt_ref[0]
        row = lax.broadcasted_iota(jnp.int32, (PV_ROWS - V_DIM, nk), 0)
        vxt_ref[V_DIM:, :] = jnp.where(row == 0, 1.0, 0.0).astype(BF16)

        dim = lax.broadcasted_iota(jnp.int32, (LANES, LANES), 0)
        col = lax.broadcasted_iota(jnp.int32, (LANES, LANES), 1)
        ind = jnp.where(col == jnp.where(dim < QK_DIM, 0, 1), 1.0, 0.0).astype(BF16)

        def sq_norms(x):
            xf = x.astype(F32)
            return _dot((xf * xf).astype(BF16), ind)

        kmax = jnp.max(sq_norms(kx_ref[...]), axis=0, keepdims=True)
        qn = sq_norms(qall_ref[0]).T
        bound = jnp.sqrt(qn[0:HALO_ROWS] * jnp.concatenate(
            [kmax[:, 0:1], kmax[:, 1:2], jnp.zeros((HALO_ROWS - 2, 1), F32)], axis=0)) * 1.05
        shift_ref[...] = bound
        safe_ref[0] = (jnp.max(bound) <= SAFE_SHIFT).astype(jnp.int32)

    lp = lam_ref[...]
    lam = (jnp.exp(jnp.sum(lp[0:1] * lp[1:2], keepdims=True))
           - jnp.exp(jnp.sum(lp[2:3] * lp[3:4], keepdims=True)) + lam_init)
    q = q_ref[0]
    lane = lax.broadcasted_iota(jnp.int32, q.shape, 1)
    zero = jnp.zeros_like(q)
    qms = [jnp.where(in_map, q, zero) for in_map in (lane < QK_DIM, lane >= QK_DIM)]
    q0 = pl.multiple_of(pl.program_id(2) * tq, tq)
    shifts = [shift_ref[mi:mi + 1, pl.ds(q0, tq)] for mi in range(2)]
    bound_is_safe = safe_ref[0] == 1

    @pl.when(bound_is_safe)
    def _():
        for mi in range(2):
            st = lax.dot_general(kx_ref[...], qms[mi], NT_DIMS, preferred_element_type=F32)
            e_ref[mi] = jnp.exp2(st - shifts[mi]).astype(BF16)
            ox_ref[mi] = _dot(vxt_ref[...], e_ref[mi])

    @pl.when(jnp.logical_not(bound_is_safe))
    def _():
        for mi in range(2):
            s_ref[mi] = lax.dot_general(kx_ref[...], qms[mi], NT_DIMS, preferred_element_type=F32)
        for mi in range(2):
            e_ref[mi] = jnp.exp2(s_ref[mi] - jnp.max(s_ref[mi], axis=0, keepdims=True)).astype(BF16)
            ox_ref[mi] = _dot(vxt_ref[...], e_ref[mi])

    o1, l1 = ox_ref[0, :V_DIM, :], ox_ref[0, V_DIM:V_DIM + 1, :]
    o2, l2 = ox_ref[1, :V_DIM, :], ox_ref[1, V_DIM:V_DIM + 1, :]
    ot = o1 * (1.0 / l1) - o2 * (lam / l2)
    ot = ot * lax.rsqrt(jnp.mean(ot * ot, axis=0, keepdims=True) + NORM_EPS)
    o_ref[0] = (ot * sub_ref[...] * (1.0 - lam_init)).T.astype(BF16)


def _attention(q, kc, k, vct, vt, lam_p, subln, lam_init, tq):
    b, n, _ = q.shape
    m_ctx = kc.shape[1]
    head_rows = lambda rows: pl.BlockSpec((1, rows, V_DIM), lambda bi, h, i: (bi, 0, h))
    head_cols = lambda cols: pl.BlockSpec((1, V_DIM, cols), lambda bi, h, i: (bi, h, 0))
    return pl.pallas_call(
        functools.partial(_attn_kernel, lam_init=lam_init, m_ctx=m_ctx),
        grid=(b, ATTN_HEADS, n // tq),
        in_specs=[
            pl.BlockSpec((4, QK_DIM), lambda bi, h, i: (0, 0)),
            pl.BlockSpec((1, tq, V_DIM), lambda bi, h, i: (bi, i, h)),
            head_rows(n), head_rows(m_ctx), head_rows(n), head_cols(m_ctx), head_cols(n),
            pl.BlockSpec((V_DIM, 1), lambda bi, h, i: (0, 0)),
        ],
        out_specs=pl.BlockSpec((1, tq, V_DIM), lambda bi, h, i: (bi, i, h)),
        out_shape=jax.ShapeDtypeStruct((b, n, Q_W), BF16),
        scratch_shapes=[pltpu.VMEM((m_ctx + n, V_DIM), BF16), pltpu.VMEM((PV_ROWS, m_ctx + n), BF16),
                        pltpu.VMEM((HALO_ROWS, n), F32), pltpu.SMEM((1,), jnp.int32),
                        pltpu.VMEM((2, m_ctx + n, tq), F32), pltpu.VMEM((2, m_ctx + n, tq), BF16),
                        pltpu.VMEM((2, PV_ROWS, tq), F32)],
        compiler_params=_params("arbitrary", "arbitrary", "arbitrary"),
        name="diff_attention",
    )(lam_p, q, q, kc, k, vct, vt, subln)


def _tail(y, x, m, n2, wr_ref, x1_ref, h2_ref, pt_ref):
    x1 = x + m[2:3] * y
    x1_ref[0] = x1
    h2 = _rms_mod(x1, n2, m[3:4], m[4:5])
    h_hi, h_lo = _split_bf16(h2)
    h2_ref[0] = h_hi
    hcat = jnp.concatenate([h_hi, h_lo], axis=1)
    half = hcat.shape[0] // 2
    parts = [_dot(hcat[r0:r0 + half], wr_ref[...]) for r0 in (0, half)]
    prod = jnp.concatenate(parts, axis=0)
    logits = (prod[:, :LANES] + prod[:, LANES:]).T[:N_EXPERTS]
    ex = jnp.exp(logits - jnp.max(logits, axis=0, keepdims=True))
    pt_ref[0] = ex / jnp.sum(ex, axis=0, keepdims=True)


def _tail_specs(b, n, d, tm):
    in_specs = [
        pl.BlockSpec((1, d), lambda bi, i, *_: (0, 0)),
        pl.BlockSpec((2 * d, 2 * LANES), lambda bi, i, *_: (0, 0)),
    ]
    out_specs = [
        pl.BlockSpec((1, tm, d), lambda bi, i, *_: (bi, i, 0)),
        pl.BlockSpec((1, tm, d), lambda bi, i, *_: (bi, i, 0)),
        pl.BlockSpec((1, N_EXPERTS, tm), lambda bi, i, *_: (bi, 0, i)),
    ]
    out_shape = [
        jax.ShapeDtypeStruct((b, n, d), F32),
        jax.ShapeDtypeStruct((b, n, d), BF16),
        jax.ShapeDtypeStruct((b, N_EXPERTS, n), F32),
    ]
    return in_specs, out_specs, out_shape


def _outproj_kernel(a_ref, c_ref, x_ref, mod_ref, wo_ref, n2_ref, wr_ref,
                    x1_ref, h2_ref, pt_ref, mix_ref):
    mix_ref[:, :Q_W] = a_ref[0]
    mix_ref[:, Q_W:] = c_ref[0]
    y = _dot(mix_ref[...], wo_ref[...])
    _tail(y, x_ref[0], mod_ref[0], n2_ref[...], wr_ref, x1_ref, h2_ref, pt_ref)


def _out_projection(attn, conv, x, mod0, w_out_bf, norm2, wr_cat, tm):
    b, n, d = x.shape
    tail_in, out_specs, out_shape = _tail_specs(b, n, d, tm)
    return pl.pallas_call(
        _outproj_kernel,
        grid=(b, n // tm),
        in_specs=[
            pl.BlockSpec((1, tm, Q_W), lambda bi, i: (bi, i, 0)),
            pl.BlockSpec((1, tm, CONV_W), lambda bi, i: (bi, i, 0)),
            pl.BlockSpec((1, tm, d), lambda bi, i: (bi, i, 0)),
            pl.BlockSpec((1, 6, d), lambda bi, i: (bi, 0, 0)),
            pl.BlockSpec((d, d), lambda bi, i: (0, 0)),
        ] + tail_in,
        out_specs=out_specs,
        out_shape=out_shape,
        scratch_shapes=[pltpu.VMEM((tm, d), BF16)],
        compiler_params=_params("arbitrary", "arbitrary"),
        name="out_projection",
    )(attn, conv, x, mod0, w_out_bf, norm2, wr_cat)


def _gelu(x):
    return 0.5 * x * (1.0 + lax.erf(x * np.float32(1.0 / math.sqrt(2.0))))


def _cmlp_body(x, m, n1_ref, w1_ref, vn_ref, ws_ref, bs_ref, wo_ref, n2_ref, wr_ref,
               x1_ref, h2_ref, pt_ref, mix_ref):
    tm, d = x.shape
    h = _rms_mod(x, n1_ref[...], m[0:1], m[1:2]).astype(BF16)
    p = _gelu(_dot(h, w1_ref[...]))
    u, v = p[:, :d], p[:, d:]
    v = v * lax.rsqrt(jnp.mean(v * v, axis=-1, keepdims=True) + NORM_EPS) * vn_ref[...]
    vb = v.astype(BF16)
    gw = d // CMLP_GROUPS
    bs = bs_ref[...]
    for c in range(tm // CHUNK):
        r0 = c * CHUNK
        for g in range(CMLP_GROUPS):
            s = _dot(ws_ref[g], vb[r0:r0 + CHUNK, g * gw:(g + 1) * gw]) + bs[:, g:g + 1]
            mix_ref[r0:r0 + CHUNK, g * gw:(g + 1) * gw] = (u[r0:r0 + CHUNK, g * gw:(g + 1) * gw] * s).astype(BF16)
    y = _dot(mix_ref[...], wo_ref[...])
    _tail(y, x, m, n2_ref[...], wr_ref, x1_ref, h2_ref, pt_ref)


def _route_kernel(p_ref, sel_ref, start_ref, fits_ref, *, cap, pchunk, window):
    rows, n = p_ref.shape

    def bit_step(it, t):
        cand = t | jnp.left_shift(jnp.int32(1), 30 - it)
        cf = lax.bitcast_convert_type(cand, F32)
        cnt = jnp.sum(jnp.where(p_ref[...] >= cf, 1.0, 0.0), axis=1, keepdims=True)
        return jnp.where(cnt >= cap, cand, t)

    t = lax.fori_loop(0, 31, bit_step, jnp.zeros((rows, 1), jnp.int32))
    tf = lax.bitcast_convert_type(t, F32)
    p = p_ref[...]
    gt = p > tf
    eq = jnp.logical_and(p >= tf, jnp.logical_not(gt))
    need = cap - jnp.sum(jnp.where(gt, 1.0, 0.0), axis=1, keepdims=True)

    ri = lax.broadcasted_iota(jnp.int32, (pchunk, pchunk), 0)
    ci = lax.broadcasted_iota(jnp.int32, (pchunk, pchunk), 1)
    upper = jnp.where(ri < ci, 1.0, 0.0).astype(BF16)

    def prefix(mask):
        mf = jnp.where(mask, 1.0, 0.0)
        carry = jnp.zeros((rows, 1), F32)
        parts, before, inside = [], [], []
        for c in range(n // pchunk):
            blk = mf[:, c * pchunk:(c + 1) * pchunk]
            parts.append(_dot(blk.astype(BF16), upper) + carry)
            total = jnp.sum(blk, axis=1, keepdims=True)
            before.append(carry)
            inside.append(total)
            carry = carry + total
        return jnp.concatenate(parts, axis=1), jnp.concatenate(before, axis=1), jnp.concatenate(inside, axis=1)

    chosen = jnp.logical_or(gt, jnp.logical_and(eq, prefix(eq)[0] < need))
    rank, before, inside = prefix(chosen)
    sel_ref[...] = jnp.where(chosen, rank, -1.0)
    start = jnp.minimum(jnp.floor(before * (1.0 / ALIGN_ROWS)) * ALIGN_ROWS, float(cap - window))
    start_ref[...] = start.astype(jnp.int32)
    fits_ref[...] = jnp.where(before + inside <= start + window, 1, 0).astype(jnp.int32)


def _route(pt, cap, window):
    b, e, n = pt.shape
    rows = b * e
    pchunk = min(SUB_TOKENS, n)
    n_chunks = n // pchunk
    full = lambda cols: pl.BlockSpec((rows, cols), lambda i: (0, 0))
    sel, start, fits = pl.pallas_call(
        functools.partial(_route_kernel, cap=cap, pchunk=pchunk, window=window),
        grid=(1,),
        in_specs=[full(n)],
        out_specs=[full(n), full(n_chunks), full(n_chunks)],
        out_shape=[jax.ShapeDtypeStruct((rows, n), F32), jax.ShapeDtypeStruct((rows, n_chunks), jnp.int32),
                   jax.ShapeDtypeStruct((rows, n_chunks), jnp.int32)],
        compiler_params=_params("arbitrary"),
        name="route",
    )(pt.reshape(rows, n))
    return sel.reshape(b, e, n), start.reshape(-1), fits.reshape(-1)


def _gather_kernel(start_ref, fits_ref, sel_ref, pt_ref, h_ref, xs_ref, vals_ref, *, cap, window, sub, n_chunks):
    bi = pl.program_id(0)
    per_pass = sub // window
    entry = lambda e, c: (bi * N_EXPERTS + e) * n_chunks + c
    lane = lax.broadcasted_iota(jnp.int32, (1, LANES), 1)

    all_fit = fits_ref[entry(0, 0)]
    for e in range(N_EXPERTS):
        for c in range(n_chunks):
            all_fit = jnp.minimum(all_fit, fits_ref[entry(e, c)])

    vals_ref[...] = jnp.zeros(vals_ref.shape, F32)

    @pl.when(all_fit == 1)
    def _():
        xs_ref[...] = jnp.zeros(xs_ref.shape, BF16)
        offs = lax.broadcasted_iota(jnp.int32, (window, sub), 0).astype(F32)
        for c in range(n_chunks):
            sel = sel_ref[0, :, c * sub:(c + 1) * sub]
            prob = pt_ref[0, :, c * sub:(c + 1) * sub]
            hc = h_ref[0, c * sub:(c + 1) * sub, :]
            for g in range(N_EXPERTS // per_pass):
                group = range(g * per_pass, (g + 1) * per_pass)
                starts = [pl.multiple_of(start_ref[entry(e, c)], ALIGN_ROWS) for e in group]
                hits = [sel[e:e + 1, :] - a.astype(F32) == offs for e, a in zip(group, starts)]
                hots = [jnp.where(hit, 1.0, 0.0).astype(BF16) for hit in hits]
                z = _dot(jnp.concatenate(hots, axis=0), hc).astype(BF16)
                for k, (e, a) in enumerate(zip(group, starts)):
                    xs_ref[0, e, pl.ds(a, window), :] += z[k * window:(k + 1) * window]
                    w = jnp.sum(jnp.where(hits[k], prob[e:e + 1, :], 0.0), axis=1, keepdims=True)
                    vals_ref[0, pl.ds(a, window), :] += jnp.where(lane == e, w, 0.0)

    @pl.when(all_fit != 1)
    def _():
        n = sel_ref.shape[-1]
        rank = lax.broadcasted_iota(jnp.int32, (cap, n), 0).astype(F32)
        for e in range(N_EXPERTS):
            hit = sel_ref[0, e:e + 1, :] == rank
            xs_ref[0, e] = _dot(jnp.where(hit, 1.0, 0.0).astype(BF16), h_ref[0]).astype(BF16)
            w = jnp.sum(jnp.where(hit, pt_ref[0, e:e + 1, :], 0.0), axis=1, keepdims=True)
            vals_ref[0] += jnp.where(lane == e, w, 0.0)


def _gather(route, pt, h2):
    b, n, d = h2.shape
    cap = route.geom["cap"]
    experts_by_tokens = pl.BlockSpec((1, N_EXPERTS, n), lambda bi, *_: (bi, 0, 0))
    return pl.pallas_call(
        functools.partial(_gather_kernel, **route.geom),
        grid_spec=pltpu.PrefetchScalarGridSpec(
            num_scalar_prefetch=2,
            grid=(b,),
            in_specs=[experts_by_tokens, experts_by_tokens, pl.BlockSpec((1, n, d), lambda bi, *_: (bi, 0, 0))],
            out_specs=[pl.BlockSpec((1, N_EXPERTS, cap, d), lambda bi, *_: (bi, 0, 0, 0)),
                       pl.BlockSpec((1, cap, LANES), lambda bi, *_: (bi, 0, 0))],
        ),
        out_shape=[jax.ShapeDtypeStruct((b, N_EXPERTS, cap, d), BF16), jax.ShapeDtypeStruct((b, cap, LANES), F32)],
        compiler_params=_params("arbitrary"),
        name="moe_gather",
    )(route.start, route.fits, route.sel, pt, h2)


def _expert_kernel(vals_ref, xs_ref, wg_ref, wu_ref, wd_ref, y_ref, wg_s, wu_s, wd_s):
    e = pl.program_id(0)

    @pl.when(pl.program_id(1) == 0)
    def _():
        wg_s[...] = wg_ref[0, 0].astype(BF16)
        wu_s[...] = wu_ref[0, 0].astype(BF16)
        wd_s[...] = wd_ref[0, 0].astype(BF16)

    bb, cap, _ = vals_ref.shape
    d = xs_ref.shape[-1]
    lane = lax.broadcasted_iota(jnp.int32, (1, 1, LANES), 2)
    vals = jnp.sum(jnp.where(lane == e, vals_ref[...], 0.0), axis=2, keepdims=True).reshape(bb * cap, 1)
    xs = xs_ref[...].reshape(bb * cap, d)
    a = _dot(xs, wg_s[...])
    bm = _dot(xs, wu_s[...])
    hid = (a * jax.nn.sigmoid(a) * bm * vals).astype(BF16)
    y_ref[...] = _dot(hid, wd_s[...]).astype(BF16).reshape(y_ref.shape)


def _expert_ffn(vals, xs, w_gate, w_up, w_down, layer):
    b, _, cap, d = xs.shape
    hdim = w_gate.shape[-1]
    bb = next(k for k in (4, 2, 1) if b % k == 0)
    w_spec = lambda rows, cols: pl.BlockSpec((1, 1, rows, cols), lambda e, bi: (layer, e, 0, 0))
    return pl.pallas_call(
        _expert_kernel,
        grid=(N_EXPERTS, b // bb),
        in_specs=[
            pl.BlockSpec((bb, cap, LANES), lambda e, bi: (bi, 0, 0)),
            pl.BlockSpec((bb, 1, cap, d), lambda e, bi: (bi, e, 0, 0)),
            w_spec(d, hdim), w_spec(d, hdim), w_spec(hdim, d),
        ],
        out_specs=pl.BlockSpec((bb, 1, cap, d), lambda e, bi: (bi, e, 0, 0)),
        out_shape=jax.ShapeDtypeStruct((b, N_EXPERTS, cap, d), BF16),
        scratch_shapes=[pltpu.VMEM((d, hdim), BF16), pltpu.VMEM((d, hdim), BF16), pltpu.VMEM((hdim, d), BF16)],
        compiler_params=_params("arbitrary", "arbitrary"),
        name="expert_ffn",
    )(vals, xs, w_gate, w_up, w_down)


def _scatter_add(start_ref, fits_ref, sel_ref, y_ref, acc_ref, *, cap, window, sub, n_chunks):
    bi, i = pl.program_id(0), pl.program_id(1)
    tm = sel_ref.shape[-1]
    n_sub = tm // sub
    per_pass = sub // window
    entry = lambda e, j: (bi * N_EXPERTS + e) * n_chunks + i * n_sub + j

    all_fit = fits_ref[entry(0, 0)]
    for e in range(N_EXPERTS):
        for j in range(n_sub):
            all_fit = jnp.minimum(all_fit, fits_ref[entry(e, j)])

    @pl.when(all_fit == 1)
    def _():
        offs = lax.broadcasted_iota(jnp.int32, (window, sub), 0).astype(F32)
        for j in range(n_sub):
            sel = sel_ref[0, :, j * sub:(j + 1) * sub]
            acc = jnp.zeros((sub, acc_ref.shape[1]), F32)
            for g in range(N_EXPERTS // per_pass):
                hots, ys = [], []
                for e in range(g * per_pass, (g + 1) * per_pass):
                    a = pl.multiple_of(start_ref[entry(e, j)], ALIGN_ROWS)
                    hots.append(jnp.where(sel[e:e + 1, :] - a.astype(F32) == offs, 1.0, 0.0).astype(BF16))
                    ys.append(y_ref[0, e, pl.ds(a, window), :])
                acc = acc + lax.dot_general(jnp.concatenate(hots, axis=0), jnp.concatenate(ys, axis=0),
                                            TN_DIMS, preferred_element_type=F32)
            acc_ref[j * sub:(j + 1) * sub, :] = acc

    @pl.when(all_fit != 1)
    def _():
        sel = sel_ref[0]
        rank = lax.broadcasted_iota(jnp.int32, (cap, tm), 0).astype(F32)
        acc = jnp.zeros(acc_ref.shape, F32)
        for e in range(N_EXPERTS):
            onehot = jnp.where(sel[e:e + 1, :] == rank, 1.0, 0.0).astype(BF16)
            acc = acc + lax.dot_general(onehot, y_ref[0, e], TN_DIMS, preferred_element_type=F32)
        acc_ref[...] = acc


def _combine_final_kernel(start_ref, fits_ref, sel_ref, y_ref, x_ref, mod_ref, fn_ref, o_ref, acc_ref, **geom):
    _scatter_add(start_ref, fits_ref, sel_ref, y_ref, acc_ref, **geom)
    x2 = x_ref[0] + mod_ref[0][5:6] * acc_ref[...]
    o_ref[0] = x2 * lax.rsqrt(jnp.mean(x2 * x2, axis=-1, keepdims=True) + NORM_EPS) * fn_ref[...]


def _combine_cmlp_kernel(start_ref, fits_ref, sel_ref, y_ref, x_ref, mod0_ref, mod1_ref, n1_ref, w1_ref, vn_ref,
                         ws_ref, bs_ref, wo_ref, n2_ref, wr_ref, x1_ref, h2_ref, pt_ref, acc_ref, mix_ref, **geom):
    _scatter_add(start_ref, fits_ref, sel_ref, y_ref, acc_ref, **geom)
    x2 = x_ref[0] + mod0_ref[0][5:6] * acc_ref[...]
    _cmlp_body(x2, mod1_ref[0], n1_ref, w1_ref, vn_ref, ws_ref, bs_ref, wo_ref, n2_ref, wr_ref,
               x1_ref, h2_ref, pt_ref, mix_ref)


def _combine_specs(b, n, d, cap, tm):
    return [
        pl.BlockSpec((1, N_EXPERTS, tm), lambda bi, i, *_: (bi, 0, i)),
        pl.BlockSpec((1, N_EXPERTS, cap, d), lambda bi, i, *_: (bi, 0, 0, 0)),
        pl.BlockSpec((1, tm, d), lambda bi, i, *_: (bi, i, 0)),
        pl.BlockSpec((1, 6, d), lambda bi, i, *_: (bi, 0, 0)),
    ]


def _combine_final(route, y, x1, mod_l, final_norm, tm):
    b, n, d = x1.shape
    return pl.pallas_call(
        functools.partial(_combine_final_kernel, **route.geom),
        grid_spec=pltpu.PrefetchScalarGridSpec(
            num_scalar_prefetch=2,
            grid=(b, n // tm),
            in_specs=_combine_specs(b, n, d, route.geom["cap"], tm) + [pl.BlockSpec((1, d), lambda bi, i, *_: (0, 0))],
            out_specs=pl.BlockSpec((1, tm, d), lambda bi, i, *_: (bi, i, 0)),
            scratch_shapes=[pltpu.VMEM((tm, d), F32)],
        ),
        out_shape=jax.ShapeDtypeStruct((b, n, d), F32),
        compiler_params=_params("arbitrary", "arbitrary"),
        name="moe_combine_final",
    )(route.start, route.fits, route.sel, y, x1, mod_l, final_norm)


def _combine_chunk_mlp(route, y, x1, mod0, mod1, norm1, w1_bf, v_norm, ws_bf, bs_t, w_out_bf, norm2, wr_cat, tm):
    b, n, d = x1.shape
    const = lambda *shape: pl.BlockSpec(shape, lambda bi, i, *_: (0,) * len(shape))
    tail_in, out_specs, out_shape = _tail_specs(b, n, d, tm)
    return pl.pallas_call(
        functools.partial(_combine_cmlp_kernel, **route.geom),
        grid_spec=pltpu.PrefetchScalarGridSpec(
            num_scalar_prefetch=2,
            grid=(b, n // tm),
            in_specs=_combine_specs(b, n, d, route.geom["cap"], tm) + [
                pl.BlockSpec((1, 6, d), lambda bi, i, *_: (bi, 0, 0)),
                const(1, d), const(d, 2 * d), const(1, d), const(CMLP_GROUPS, CHUNK, CHUNK),
                const(CHUNK, CMLP_GROUPS), const(d, d),
            ] + tail_in,
            out_specs=out_specs,
            scratch_shapes=[pltpu.VMEM((tm, d), F32), pltpu.VMEM((tm, d), BF16)],
        ),
        out_shape=out_shape,
        compiler_params=_params("arbitrary", "arbitrary"),
        name="combine_chunk_mlp",
    )(route.start, route.fits, route.sel, y, x1, mod0, mod1, norm1, w1_bf, v_norm, ws_bf, bs_t, w_out_bf,
      norm2, wr_cat)


class _Routing(NamedTuple):
    sel: jax.Array
    start: jax.Array
    fits: jax.Array
    geom: dict


def _moe_experts(h2, pt, w_gate, w_up, w_down, layer):
    n = h2.shape[1]
    cap = CAPACITY_FACTOR * n // N_EXPERTS
    window = min(SLOT_WINDOW, cap)
    sub = min(SUB_TOKENS, n)
    sel, start, fits = _route(pt, cap, window)
    route = _Routing(sel, start, fits, dict(cap=cap, window=window, sub=sub, n_chunks=n // sub))
    xs, vals = _gather(route, pt, h2)
    return route, _expert_ffn(vals, xs, w_gate, w_up, w_down, layer)


def kernel(x, c, ctx, c_ctx, w_mod, b_mod, norm1, norm2, even_w_in, even_lambda, even_subln, even_conv_w,
           odd_w_in, odd_v_norm, odd_w_s, odd_b_s, w_out, w_router, w_gate, w_up, w_down, final_norm):
    b, n, d = x.shape
    depth = w_mod.shape[0]
    assert d == D_MODEL and depth == 2 and n % 256 == 0
    t_proj = min(1024, n)
    t_attn = min(1024, n)
    t_moe = min(512, n)
    t_final = min(1024, n)

    cc = jnp.concatenate([c, c_ctx[None, :]], axis=0)
    mod = _modulation(cc, w_mod, b_mod).reshape(depth, b + 1, 6, d)
    row = lambda a, l: a[l].reshape(1, -1)
    wr_pad = jnp.pad(w_router, ((0, 0), (0, 0), (0, LANES - N_EXPERTS)))
    wr_hi = wr_pad.astype(BF16)
    wr_lo = (wr_pad - wr_hi.astype(F32)).astype(BF16)
    wr_cat = jnp.concatenate([jnp.concatenate([wr_hi, wr_lo], axis=2),
                              jnp.concatenate([wr_hi, jnp.zeros_like(wr_lo)], axis=2)], axis=1)
    w_out_bf = w_out.astype(BF16)
    fnorm = final_norm.reshape(1, d)

    mod0, modc = mod[0, :b], mod[0, b:]
    w_in_bf = even_w_in[0].astype(BF16)
    q, k, vt, conv = _in_projection(x, mod0, row(norm1, 0), w_in_bf, even_conv_w[0], t_proj)
    kc, vct = _context_kv(ctx, modc, row(norm1, 0), w_in_bf)
    lam_init = 0.8 - 0.6 * math.exp(-0.3 * 0)
    attn = _attention(q, kc, k, vct, vt, even_lambda[0], even_subln[0].reshape(-1, 1), lam_init, t_attn)
    x1, h2, pt = _out_projection(attn, conv, x, mod0, w_out_bf[0], row(norm2, 0), wr_cat[0], t_proj)
    route, y = _moe_experts(h2, pt, w_gate, w_up, w_down, 0)

    mod1 = mod[1, :b]
    x1, h2, pt = _combine_chunk_mlp(route, y, x1, mod0, mod1, row(norm1, 1), odd_w_in[0].astype(BF16),
                                    row(odd_v_norm, 0), odd_w_s[0].astype(BF16), odd_b_s[0].T, w_out_bf[1],
                                    row(norm2, 1), wr_cat[1], t_moe)
    route, y = _moe_experts(h2, pt, w_gate, w_up, w_down, 1)
    return _combine_final(route, y, x1, mod1, fnorm, t_final)
```

```python
import functools
import math
from typing import NamedTuple

import numpy as np
import jax
import jax.numpy as jnp
from jax import lax
from jax.experimental import pallas as pl
from jax.experimental.pallas import tpu as pltpu

F32 = jnp.float32
BF16 = jnp.bfloat16

D_MODEL = 1024
ATTN_HEADS = 4
QK_DIM = 64
V_DIM = 128
Q_W = ATTN_HEADS * 2 * QK_DIM
CONV_W = 512
EVEN_IN = 3072
N_EXPERTS = 16
CAPACITY_FACTOR = 2
GRID_W = 64
CHUNK = 128
CMLP_GROUPS = 4
ROPE_THETA = 10000.0
NORM_EPS = 1e-6
LANES = 128
SUBLANES = 8
HALO_ROWS = SUBLANES
VMEM_LIMIT = 56 * 1024 * 1024
PV_ROWS = V_DIM + 16
SAFE_SHIFT = 60.0
ALIGN_ROWS = 16
SUB_TOKENS = 256
SLOT_WINDOW = 64

NT_DIMS = (((1,), (1,)), ((), ()))
TN_DIMS = (((0,), (0,)), ((), ()))


def _dot(a, b):
    return jnp.dot(a, b, preferred_element_type=F32)


def _params(*sem):
    return pltpu.CompilerParams(dimension_semantics=sem, vmem_limit_bytes=VMEM_LIMIT)


def _rms_mod(x, g, shift, scale):
    y = x * lax.rsqrt(jnp.mean(x * x, axis=-1, keepdims=True) + NORM_EPS)
    return y * (g * (1.0 + scale)) + shift


def _split_bf16(x):
    hi = x.astype(BF16)
    lo = (x - hi.astype(F32)).astype(BF16)
    return hi, lo


def _mod_kernel(c_ref, w_ref, b_ref, o_ref):
    c = c_ref[...]
    a_hi, a_lo = _split_bf16(c * jax.nn.sigmoid(c))
    w_hi, w_lo = _split_bf16(w_ref[0])
    o_ref[0] = _dot(a_hi, w_hi) + _dot(a_lo, w_hi) + _dot(a_hi, w_lo) + b_ref[0]


def _modulation(cc, w_mod, b_mod):
    depth, d, six_d = w_mod.shape
    rows = cc.shape[0]
    tn = 1536
    return pl.pallas_call(
        _mod_kernel,
        grid=(depth, six_d // tn),
        in_specs=[
            pl.BlockSpec((rows, d), lambda l, j: (0, 0)),
            pl.BlockSpec((1, d, tn), lambda l, j: (l, 0, j)),
            pl.BlockSpec((1, 1, tn), lambda l, j: (l, 0, j)),
        ],
        out_specs=pl.BlockSpec((1, rows, tn), lambda l, j: (l, 0, j)),
        out_shape=jax.ShapeDtypeStruct((depth, rows, six_d), F32),
        compiler_params=_params("arbitrary", "arbitrary"),
        name="modulation",
    )(cc, w_mod, b_mod.reshape(depth, 1, six_d))


def _inproj_kernel(x_ref, xp_ref, xn_ref, mod_ref, n1_ref, w_ref, cos_ref, sa_ref, sb_ref, cw_ref,
                   q_ref, k_ref, vt_ref, c_ref, *, tm, n_tiles):
    i = pl.program_id(1)
    m = mod_ref[0]
    shift, scale = m[0:1], m[1:2]
    g = n1_ref[...]
    h = _rms_mod(x_ref[0], g, shift, scale).astype(BF16)

    cos, sa, sb = cos_ref[...], sa_ref[...], sb_ref[...]
    for col0, out_ref, qscale in ((0, q_ref, QK_DIM ** -0.5 * math.log2(math.e)), (Q_W, k_ref, 1.0)):
        p = _dot(h, w_ref[:, col0:col0 + Q_W])
        for j in range(Q_W // LANES):
            pj = p[:, j * LANES:(j + 1) * LANES]
            r = pj * cos + pltpu.roll(pj, LANES - 16, 1) * sa + pltpu.roll(pj, 16, 1) * sb
            out_ref[0, :, j * LANES:(j + 1) * LANES] = (r * qscale).astype(BF16)

    vt_ref[0] = _dot(h, w_ref[:, 2 * Q_W:2 * Q_W + 512]).T.astype(BF16)

    pc = _dot(h, w_ref[:, 2 * Q_W + 512:])
    gb = pc[:, :CONV_W]
    u = pc[:, CONV_W:2 * CONV_W] * pc[:, 2 * CONV_W:]
    xh = jnp.concatenate([xp_ref[0], xn_ref[0]], axis=0)
    hh = _rms_mod(xh, g, shift, scale).astype(BF16)
    ph = _dot(hh, w_ref[:, 2 * Q_W + 512 + CONV_W:])
    uh = ph[:, :CONV_W] * ph[:, CONV_W:]
    u_before = jnp.where(i > 0, uh[HALO_ROWS - 1:HALO_ROWS], 0.0)
    u_after = jnp.where(i < n_tiles - 1, uh[HALO_ROWS:HALO_ROWS + 1], 0.0)
    row = lax.broadcasted_iota(jnp.int32, (tm, 1), 0)
    u_prev = jnp.where(row == 0, u_before, pltpu.roll(u, 1, 0))
    u_next = jnp.where(row == tm - 1, u_after, pltpu.roll(u, tm - 1, 0))
    cw = cw_ref[...]
    conv = cw[0:1] * u_prev + cw[1:2] * u + cw[2:3] * u_next
    c_ref[0] = (gb * conv).astype(BF16)


def _rope_tables(n):
    rows = n // GRID_W
    row = jnp.repeat(jnp.arange(rows), GRID_W).astype(F32)
    col = jnp.tile(jnp.arange(GRID_W), rows).astype(F32)
    half = QK_DIM // 2
    inv = 1.0 / (ROPE_THETA ** (jnp.arange(0, half, 2, dtype=F32) / half))
    ang_r = row[:, None] * inv
    ang_c = col[:, None] * inv
    ang = jnp.concatenate([ang_r, ang_r, ang_c, ang_c], axis=-1)
    cos, sin = jnp.cos(ang), jnp.sin(ang)
    first_half = (jnp.arange(QK_DIM) % 32) < 16
    sin_a = jnp.where(first_half, -sin, 0.0)
    sin_b = jnp.where(first_half, 0.0, sin)
    tile2 = lambda t: jnp.concatenate([t, t], axis=-1)
    return tile2(cos), tile2(sin_a), tile2(sin_b)


def _in_projection(x, mod0, norm1, w_in_bf, conv_w, tm):
    b, n, d = x.shape
    n_tiles = n // tm
    hb = tm // HALO_ROWS
    n_hblocks = n // HALO_ROWS
    cos, sa, sb = _rope_tables(n)
    out = jax.ShapeDtypeStruct((b, n, Q_W), BF16)
    row_spec = pl.BlockSpec((1, tm, Q_W), lambda bi, i: (bi, i, 0))
    tab_spec = pl.BlockSpec((tm, LANES), lambda bi, i: (i, 0))
    return pl.pallas_call(
        functools.partial(_inproj_kernel, tm=tm, n_tiles=n_tiles),
        grid=(b, n_tiles),
        in_specs=[
            pl.BlockSpec((1, tm, d), lambda bi, i: (bi, i, 0)),
            pl.BlockSpec((1, HALO_ROWS, d), lambda bi, i: (bi, jnp.maximum(i * hb - 1, 0), 0)),
            pl.BlockSpec((1, HALO_ROWS, d), lambda bi, i: (bi, jnp.minimum((i + 1) * hb, n_hblocks - 1), 0)),
            pl.BlockSpec((1, 6, d), lambda bi, i: (bi, 0, 0)),
            pl.BlockSpec((1, d), lambda bi, i: (0, 0)),
            pl.BlockSpec((d, EVEN_IN), lambda bi, i: (0, 0)),
            tab_spec, tab_spec, tab_spec,
            pl.BlockSpec((3, CONV_W), lambda bi, i: (0, 0)),
        ],
        out_specs=[row_spec, row_spec, pl.BlockSpec((1, Q_W, tm), lambda bi, i: (bi, 0, i)), row_spec],
        out_shape=[out, out, jax.ShapeDtypeStruct((b, Q_W, n), BF16), out],
        compiler_params=_params("arbitrary", "arbitrary"),
        name="in_projection",
    )(x, x, x, mod0, norm1, w_in_bf, cos, sa, sb, conv_w)


def _ctxkv_kernel(x_ref, mod_ref, n1_ref, wk_ref, wv_ref, k_ref, vt_ref):
    bb, m_ctx, d = x_ref.shape
    m = mod_ref[0]
    h = _rms_mod(x_ref[...].reshape(bb * m_ctx, d), n1_ref[...], m[0:1], m[1:2]).astype(BF16)
    k_ref[...] = _dot(h, wk_ref[...]).astype(BF16).reshape(k_ref.shape)
    v = _dot(h, wv_ref[...])
    for j in range(bb):
        vt_ref[j] = v[j * m_ctx:(j + 1) * m_ctx].T.astype(BF16)


def _context_kv(ctx, modc, norm1, w_in_bf):
    b, m, d = ctx.shape
    bb = next(k for k in (4, 2, 1) if b % k == 0)
    out = jax.ShapeDtypeStruct((b, m, Q_W), BF16)
    return pl.pallas_call(
        _ctxkv_kernel,
        grid=(b // bb,),
        in_specs=[
            pl.BlockSpec((bb, m, d), lambda bi: (bi, 0, 0)),
            pl.BlockSpec((1, 6, d), lambda bi: (0, 0, 0)),
            pl.BlockSpec((1, d), lambda bi: (0, 0)),
            pl.BlockSpec((d, Q_W), lambda bi: (0, 1)),
            pl.BlockSpec((d, Q_W), lambda bi: (0, 2)),
        ],
        out_specs=[pl.BlockSpec((bb, m, Q_W), lambda bi: (bi, 0, 0)), pl.BlockSpec((bb, Q_W, m), lambda bi: (bi, 0, 0))],
        out_shape=[out, jax.ShapeDtypeStruct((b, Q_W, m), BF16)],
        compiler_params=_params("arbitrary"),
        name="context_kv",
    )(ctx, modc, norm1, w_in_bf, w_in_bf)


def _attn_kernel(lam_ref, q_ref, qall_ref, kc_ref, k_ref, vct_ref, vt_ref, sub_ref, o_ref,
                 kx_ref, vxt_ref, shift_ref, safe_ref, s_ref, e_ref, ox_ref, *, lam_init, m_ctx):
    nk = kx_ref.shape[0]
    tq = q_ref.shape[1]

    @pl.when(pl.program_id(2) == 0)
    def _():
        kx_ref[:m_ctx, :] = kc_ref[0]
        kx_ref[m_ctx:, :] = k_ref[0]
        vxt_ref[:V_DIM, :m_ctx] = vct_ref[0]
        vxt_ref[:V_DIM, m_ctx:] = vt_ref[0]
        row = lax.broadcasted_iota(jnp.int32, (PV_ROWS - V_DIM, nk), 0)
        vxt_ref[V_DIM:, :] = jnp.where(row == 0, 1.0, 0.0).astype(BF16)

        dim = lax.broadcasted_iota(jnp.int32, (LANES, LANES), 0)
        col = lax.broadcasted_iota(jnp.int32, (LANES, LANES), 1)
        ind = jnp.where(col == jnp.where(dim < QK_DIM, 0, 1), 1.0, 0.0).astype(BF16)

        def sq_norms(x):
            xf = x.astype(F32)
            return _dot((xf * xf).astype(BF16), ind)

        kmax = jnp.max(sq_norms(kx_ref[...]), axis=0, keepdims=True)
        qn = sq_norms(qall_ref[0]).T
        bound = jnp.sqrt(qn[0:SUBLANES] * jnp.concatenate(
            [kmax[:, 0:1], kmax[:, 1:2], jnp.zeros((SUBLANES - 2, 1), F32)], axis=0)) * 1.05
        shift_ref[...] = bound
        safe_ref[0] = (jnp.max(bound) <= SAFE_SHIFT).astype(jnp.int32)

    lp = lam_ref[...]
    lam = (jnp.exp(jnp.sum(lp[0:1] * lp[1:2], keepdims=True))
           - jnp.exp(jnp.sum(lp[2:3] * lp[3:4], keepdims=True)) + lam_init)
    q = q_ref[0]
    lane = lax.broadcasted_iota(jnp.int32, q.shape, 1)
    zero = jnp.zeros_like(q)
    qms = [jnp.where(in_map, q, zero) for in_map in (lane < QK_DIM, lane >= QK_DIM)]
    q0 = pl.multiple_of(pl.program_id(2) * tq, tq)
    shifts = [shift_ref[mi:mi + 1, pl.ds(q0, tq)] for mi in range(2)]
    bound_is_safe = safe_ref[0] == 1

    @pl.when(bound_is_safe)
    def _():
        for mi in range(2):
            st = lax.dot_general(kx_ref[...], qms[mi], NT_DIMS, preferred_element_type=F32)
            e_ref[mi] = jnp.exp2(st - shifts[mi]).astype(BF16)
            ox_ref[mi] = _dot(vxt_ref[...], e_ref[mi])

    @pl.when(jnp.logical_not(bound_is_safe))
    def _():
        for mi in range(2):
            s_ref[mi] = lax.dot_general(kx_ref[...], qms[mi], NT_DIMS, preferred_element_type=F32)
        for mi in range(2):
            e_ref[mi] = jnp.exp2(s_ref[mi] - jnp.max(s_ref[mi], axis=0, keepdims=True)).astype(BF16)
            ox_ref[mi] = _dot(vxt_ref[...], e_ref[mi])

    o1, l1 = ox_ref[0, :V_DIM, :], ox_ref[0, V_DIM:V_DIM + 1, :]
    o2, l2 = ox_ref[1, :V_DIM, :], ox_ref[1, V_DIM:V_DIM + 1, :]
    ot = o1 * (1.0 / l1) - o2 * (lam / l2)
    ot = ot * lax.rsqrt(jnp.mean(ot * ot, axis=0, keepdims=True) + NORM_EPS)
    o_ref[0] = (ot * sub_ref[...] * (1.0 - lam_init)).T.astype(BF16)


def _attention(q, kc, k, vct, vt, lam_p, subln, lam_init, tq):
    b, n, _ = q.shape
    m_ctx = kc.shape[1]
    head_rows = lambda rows: pl.BlockSpec((1, rows, V_DIM), lambda bi, h, i: (bi, 0, h))
    head_cols = lambda cols: pl.BlockSpec((1, V_DIM, cols), lambda bi, h, i: (bi, h, 0))
    return pl.pallas_call(
        functools.partial(_attn_kernel, lam_init=lam_init, m_ctx=m_ctx),
        grid=(b, ATTN_HEADS, n // tq),
        in_specs=[
            pl.BlockSpec((4, QK_DIM), lambda bi, h, i: (0, 0)),
            pl.BlockSpec((1, tq, V_DIM), lambda bi, h, i: (bi, i, h)),
            head_rows(n), head_rows(m_ctx), head_rows(n), head_cols(m_ctx), head_cols(n),
            pl.BlockSpec((V_DIM, 1), lambda bi, h, i: (0, 0)),
        ],
        out_specs=pl.BlockSpec((1, tq, V_DIM), lambda bi, h, i: (bi, i, h)),
        out_shape=jax.ShapeDtypeStruct((b, n, Q_W), BF16),
        scratch_shapes=[pltpu.VMEM((m_ctx + n, V_DIM), BF16), pltpu.VMEM((PV_ROWS, m_ctx + n), BF16),
                        pltpu.VMEM((SUBLANES, n), F32), pltpu.SMEM((1,), jnp.int32),
                        pltpu.VMEM((2, m_ctx + n, tq), F32), pltpu.VMEM((2, m_ctx + n, tq), BF16),
                        pltpu.VMEM((2, PV_ROWS, tq), F32)],
        compiler_params=_params("arbitrary", "arbitrary", "arbitrary"),
        name="diff_attention",
    )(lam_p, q, q, kc, k, vct, vt, subln)


def _tail(y, x, m, n2, wr_ref, x1_ref, h2_ref, pt_ref):
    x1 = x + m[2:3] * y
    x1_ref[0] = x1
    h2 = _rms_mod(x1, n2, m[3:4], m[4:5])
    h_hi, h_lo = _split_bf16(h2)
    h2_ref[0] = h_hi
    hcat = jnp.concatenate([h_hi, h_lo], axis=1)
    half = hcat.shape[0] // 2
    parts = [_dot(hcat[r0:r0 + half], wr_ref[...]) for r0 in (0, half)]
    prod = jnp.concatenate(parts, axis=0)
    logits = (prod[:, :LANES] + prod[:, LANES:]).T[:N_EXPERTS]
    ex = jnp.exp(logits - jnp.max(logits, axis=0, keepdims=True))
    pt_ref[0] = ex / jnp.sum(ex, axis=0, keepdims=True)


def _tail_specs(b, n, d, tm):
    in_specs = [
        pl.BlockSpec((1, d), lambda bi, i, *_: (0, 0)),
        pl.BlockSpec((2 * d, 2 * LANES), lambda bi, i, *_: (0, 0)),
    ]
    out_specs = [
        pl.BlockSpec((1, tm, d), lambda bi, i, *_: (bi, i, 0)),
        pl.BlockSpec((1, tm, d), lambda bi, i, *_: (bi, i, 0)),
        pl.BlockSpec((1, N_EXPERTS, tm), lambda bi, i, *_: (bi, 0, i)),
    ]
    out_shape = [
        jax.ShapeDtypeStruct((b, n, d), F32),
        jax.ShapeDtypeStruct((b, n, d), BF16),
        jax.ShapeDtypeStruct((b, N_EXPERTS, n), F32),
    ]
    return in_specs, out_specs, out_shape


def _outproj_kernel(a_ref, c_ref, x_ref, mod_ref, wo_ref, n2_ref, wr_ref,
                    x1_ref, h2_ref, pt_ref, mix_ref):
    mix_ref[:, :Q_W] = a_ref[0]
    mix_ref[:, Q_W:] = c_ref[0]
    y = _dot(mix_ref[...], wo_ref[...])
    _tail(y, x_ref[0], mod_ref[0], n2_ref[...], wr_ref, x1_ref, h2_ref, pt_ref)


def _out_projection(attn, conv, x, mod0, w_out_bf, norm2, wr_cat, tm):
    b, n, d = x.shape
    tail_in, out_specs, out_shape = _tail_specs(b, n, d, tm)
    return pl.pallas_call(
        _outproj_kernel,
        grid=(b, n // tm),
        in_specs=[
            pl.BlockSpec((1, tm, Q_W), lambda bi, i: (bi, i, 0)),
            pl.BlockSpec((1, tm, CONV_W), lambda bi, i: (bi, i, 0)),
            pl.BlockSpec((1, tm, d), lambda bi, i: (bi, i, 0)),
            pl.BlockSpec((1, 6, d), lambda bi, i: (bi, 0, 0)),
            pl.BlockSpec((d, d), lambda bi, i: (0, 0)),
        ] + tail_in,
        out_specs=out_specs,
        out_shape=out_shape,
        scratch_shapes=[pltpu.VMEM((tm, d), BF16)],
        compiler_params=_params("arbitrary", "arbitrary"),
        name="out_projection",
    )(attn, conv, x, mod0, w_out_bf, norm2, wr_cat)


def _gelu(x):
    return 0.5 * x * (1.0 + lax.erf(x * np.float32(1.0 / math.sqrt(2.0))))


def _cmlp_body(x, m, n1_ref, w1_ref, vn_ref, ws_ref, bs_ref, wo_ref, n2_ref, wr_ref,
               x1_ref, h2_ref, pt_ref, mix_ref):
    tm, d = x.shape
    h = _rms_mod(x, n1_ref[...], m[0:1], m[1:2]).astype(BF16)
    p = _gelu(_dot(h, w1_ref[...]))
    u, v = p[:, :d], p[:, d:]
    v = v * lax.rsqrt(jnp.mean(v * v, axis=-1, keepdims=True) + NORM_EPS) * vn_ref[...]
    vb = v.astype(BF16)
    gw = d // CMLP_GROUPS
    bs = bs_ref[...]
    for c in range(tm // CHUNK):
        r0 = c * CHUNK
        for g in range(CMLP_GROUPS):
            s = _dot(ws_ref[g], vb[r0:r0 + CHUNK, g * gw:(g + 1) * gw]) + bs[:, g:g + 1]
            mix_ref[r0:r0 + CHUNK, g * gw:(g + 1) * gw] = (u[r0:r0 + CHUNK, g * gw:(g + 1) * gw] * s).astype(BF16)
    y = _dot(mix_ref[...], wo_ref[...])
    _tail(y, x, m, n2_ref[...], wr_ref, x1_ref, h2_ref, pt_ref)


def _route_kernel(p_ref, sel_ref, start_ref, fits_ref, *, cap, pchunk, window):
    rows, n = p_ref.shape

    def bit_step(it, t):
        cand = t | jnp.left_shift(jnp.int32(1), 30 - it)
        cf = lax.bitcast_convert_type(cand, F32)
        cnt = jnp.sum(jnp.where(p_ref[...] >= cf, 1.0, 0.0), axis=1, keepdims=True)
        return jnp.where(cnt >= cap, cand, t)

    t = lax.fori_loop(0, 31, bit_step, jnp.zeros((rows, 1), jnp.int32))
    tf = lax.bitcast_convert_type(t, F32)
    p = p_ref[...]
    gt = p > tf
    eq = jnp.logical_and(p >= tf, jnp.logical_not(gt))
    need = cap - jnp.sum(jnp.where(gt, 1.0, 0.0), axis=1, keepdims=True)

    ri = lax.broadcasted_iota(jnp.int32, (pchunk, pchunk), 0)
    ci = lax.broadcasted_iota(jnp.int32, (pchunk, pchunk), 1)
    upper = jnp.where(ri < ci, 1.0, 0.0).astype(BF16)

    def prefix(mask):
        mf = jnp.where(mask, 1.0, 0.0)
        carry = jnp.zeros((rows, 1), F32)
        parts, before, inside = [], [], []
        for c in range(n // pchunk):
            blk = mf[:, c * pchunk:(c + 1) * pchunk]
            parts.append(_dot(blk.astype(BF16), upper) + carry)
            total = jnp.sum(blk, axis=1, keepdims=True)
            before.append(carry)
            inside.append(total)
            carry = carry + total
        return jnp.concatenate(parts, axis=1), jnp.concatenate(before, axis=1), jnp.concatenate(inside, axis=1)

    chosen = jnp.logical_or(gt, jnp.logical_and(eq, prefix(eq)[0] < need))
    rank, before, inside = prefix(chosen)
    sel_ref[...] = jnp.where(chosen, rank, -1.0)
    start = jnp.minimum(jnp.floor(before * (1.0 / ALIGN_ROWS)) * ALIGN_ROWS, float(cap - window))
    start_ref[...] = start.astype(jnp.int32)
    fits_ref[...] = jnp.where(before + inside <= start + window, 1, 0).astype(jnp.int32)


def _route(pt, cap, window):
    b, e, n = pt.shape
    rows = b * e
    pchunk = min(SUB_TOKENS, n)
    n_chunks = n // pchunk
    full = lambda cols: pl.BlockSpec((rows, cols), lambda i: (0, 0))
    sel, start, fits = pl.pallas_call(
        functools.partial(_route_kernel, cap=cap, pchunk=pchunk, window=window),
        grid=(1,),
        in_specs=[full(n)],
        out_specs=[full(n), full(n_chunks), full(n_chunks)],
        out_shape=[jax.ShapeDtypeStruct((rows, n), F32), jax.ShapeDtypeStruct((rows, n_chunks), jnp.int32),
                   jax.ShapeDtypeStruct((rows, n_chunks), jnp.int32)],
        compiler_params=_params("arbitrary"),
        name="route",
    )(pt.reshape(rows, n))
    return sel.reshape(b, e, n), start.reshape(-1), fits.reshape(-1)


def _gather_kernel(start_ref, fits_ref, sel_ref, pt_ref, h_ref, xs_ref, vals_ref, *, cap, window, sub, n_chunks):
    bi = pl.program_id(0)
    per_pass = sub // window
    entry = lambda e, c: (bi * N_EXPERTS + e) * n_chunks + c
    lane = lax.broadcasted_iota(jnp.int32, (1, LANES), 1)

    all_fit = fits_ref[entry(0, 0)]
    for e in range(N_EXPERTS):
        for c in range(n_chunks):
            all_fit = jnp.minimum(all_fit, fits_ref[entry(e, c)])

    vals_ref[...] = jnp.zeros(vals_ref.shape, F32)

    @pl.when(all_fit == 1)
    def _():
        xs_ref[...] = jnp.zeros(xs_ref.shape, BF16)
        offs = lax.broadcasted_iota(jnp.int32, (window, sub), 0).astype(F32)
        for c in range(n_chunks):
            sel = sel_ref[0, :, c * sub:(c + 1) * sub]
            prob = pt_ref[0, :, c * sub:(c + 1) * sub]
            hc = h_ref[0, c * sub:(c + 1) * sub, :]
            for g in range(N_EXPERTS // per_pass):
                group = range(g * per_pass, (g + 1) * per_pass)
                starts = [pl.multiple_of(start_ref[entry(e, c)], ALIGN_ROWS) for e in group]
                hits = [sel[e:e + 1, :] - a.astype(F32) == offs for e, a in zip(group, starts)]
                hots = [jnp.where(hit, 1.0, 0.0).astype(BF16) for hit in hits]
                z = _dot(jnp.concatenate(hots, axis=0), hc).astype(BF16)
                for k, (e, a) in enumerate(zip(group, starts)):
                    xs_ref[0, e, pl.ds(a, window), :] += z[k * window:(k + 1) * window]
                    w = jnp.sum(jnp.where(hits[k], prob[e:e + 1, :], 0.0), axis=1, keepdims=True)
                    vals_ref[0, pl.ds(a, window), :] += jnp.where(lane == e, w, 0.0)

    @pl.when(all_fit != 1)
    def _():
        n = sel_ref.shape[-1]
        rank = lax.broadcasted_iota(jnp.int32, (cap, n), 0).astype(F32)
        for e in range(N_EXPERTS):
            hit = sel_ref[0, e:e + 1, :] == rank
            xs_ref[0, e] = _dot(jnp.where(hit, 1.0, 0.0).astype(BF16), h_ref[0]).astype(BF16)
            w = jnp.sum(jnp.where(hit, pt_ref[0, e:e + 1, :], 0.0), axis=1, keepdims=True)
            vals_ref[0] += jnp.where(lane == e, w, 0.0)


def _gather(route, pt, h2):
    b, n, d = h2.shape
    cap = route.geom["cap"]
    experts_by_tokens = pl.BlockSpec((1, N_EXPERTS, n), lambda bi, *_: (bi, 0, 0))
    return pl.pallas_call(
        functools.partial(_gather_kernel, **route.geom),
        grid_spec=pltpu.PrefetchScalarGridSpec(
            num_scalar_prefetch=2,
            grid=(b,),
            in_specs=[experts_by_tokens, experts_by_tokens, pl.BlockSpec((1, n, d), lambda bi, *_: (bi, 0, 0))],
            out_specs=[pl.BlockSpec((1, N_EXPERTS, cap, d), lambda bi, *_: (bi, 0, 0, 0)),
                       pl.BlockSpec((1, cap, LANES), lambda bi, *_: (bi, 0, 0))],
        ),
        out_shape=[jax.ShapeDtypeStruct((b, N_EXPERTS, cap, d), BF16), jax.ShapeDtypeStruct((b, cap, LANES), F32)],
        compiler_params=_params("arbitrary"),
        name="moe_gather",
    )(route.start, route.fits, route.sel, pt, h2)


def _expert_kernel(vals_ref, xs_ref, wg_ref, wu_ref, wd_ref, y_ref, wg_s, wu_s, wd_s):
    e = pl.program_id(0)

    @pl.when(pl.program_id(1) == 0)
    def _():
        wg_s[...] = wg_ref[0, 0].astype(BF16)
        wu_s[...] = wu_ref[0, 0].astype(BF16)
        wd_s[...] = wd_ref[0, 0].astype(BF16)

    bb, cap, _ = vals_ref.shape
    d = xs_ref.shape[-1]
    lane = lax.broadcasted_iota(jnp.int32, (1, 1, LANES), 2)
    vals = jnp.sum(jnp.where(lane == e, vals_ref[...], 0.0), axis=2, keepdims=True).reshape(bb * cap, 1)
    xs = xs_ref[...].reshape(bb * cap, d)
    a = _dot(xs, wg_s[...])
    bm = _dot(xs, wu_s[...])
    hid = (a * jax.nn.sigmoid(a) * bm * vals).astype(BF16)
    y_ref[...] = _dot(hid, wd_s[...]).astype(BF16).reshape(y_ref.shape)


def _expert_ffn(vals, xs, w_gate, w_up, w_down, layer):
    b, _, cap, d = xs.shape
    hdim = w_gate.shape[-1]
    bb = next(k for k in (4, 2, 1) if b % k == 0)
    w_spec = lambda rows, cols: pl.BlockSpec((1, 1, rows, cols), lambda e, bi: (layer, e, 0, 0))
    return pl.pallas_call(
        _expert_kernel,
        grid=(N_EXPERTS, b // bb),
        in_specs=[
            pl.BlockSpec((bb, cap, LANES), lambda e, bi: (bi, 0, 0)),
            pl.BlockSpec((bb, 1, cap, d), lambda e, bi: (bi, e, 0, 0)),
            w_spec(d, hdim), w_spec(d, hdim), w_spec(hdim, d),
        ],
        out_specs=pl.BlockSpec((bb, 1, cap, d), lambda e, bi: (bi, e, 0, 0)),
        out_shape=jax.ShapeDtypeStruct((b, N_EXPERTS, cap, d), BF16),
        scratch_shapes=[pltpu.VMEM((d, hdim), BF16), pltpu.VMEM((d, hdim), BF16), pltpu.VMEM((hdim, d), BF16)],
        compiler_params=_params("arbitrary", "arbitrary"),
        name="expert_ffn",
    )(vals, xs, w_gate, w_up, w_down)


def _scatter_add(start_ref, fits_ref, sel_ref, y_ref, acc_ref, *, cap, window, sub, n_chunks):
    bi, i = pl.program_id(0), pl.program_id(1)
    tm = sel_ref.shape[-1]
    n_sub = tm // sub
    per_pass = sub // window
    entry = lambda e, j: (bi * N_EXPERTS + e) * n_chunks + i * n_sub + j

    all_fit = fits_ref[entry(0, 0)]
    for e in range(N_EXPERTS):
        for j in range(n_sub):
            all_fit = jnp.minimum(all_fit, fits_ref[entry(e, j)])

    @pl.when(all_fit == 1)
    def _():
        offs = lax.broadcasted_iota(jnp.int32, (window, sub), 0).astype(F32)
        for j in range(n_sub):
            sel = sel_ref[0, :, j * sub:(j + 1) * sub]
            acc = jnp.zeros((sub, acc_ref.shape[1]), F32)
            for g in range(N_EXPERTS // per_pass):
                hots, ys = [], []
                for e in range(g * per_pass, (g + 1) * per_pass):
                    a = pl.multiple_of(start_ref[entry(e, j)], ALIGN_ROWS)
                    hots.append(jnp.where(sel[e:e + 1, :] - a.astype(F32) == offs, 1.0, 0.0).astype(BF16))
                    ys.append(y_ref[0, e, pl.ds(a, window), :])
                acc = acc + lax.dot_general(jnp.concatenate(hots, axis=0), jnp.concatenate(ys, axis=0),
                                            TN_DIMS, preferred_element_type=F32)
            acc_ref[j * sub:(j + 1) * sub, :] = acc

    @pl.when(all_fit != 1)
    def _():
        sel = sel_ref[0]
        rank = lax.broadcasted_iota(jnp.int32, (cap, tm), 0).astype(F32)
        acc = jnp.zeros(acc_ref.shape, F32)
        for e in range(N_EXPERTS):
            onehot = jnp.where(sel[e:e + 1, :] == rank, 1.0, 0.0).astype(BF16)
            acc = acc + lax.dot_general(onehot, y_ref[0, e], TN_DIMS, preferred_element_type=F32)
        acc_ref[...] = acc


def _combine_final_kernel(start_ref, fits_ref, sel_ref, y_ref, x_ref, mod_ref, fn_ref, o_ref, acc_ref, **geom):
    _scatter_add(start_ref, fits_ref, sel_ref, y_ref, acc_ref, **geom)
    x2 = x_ref[0] + mod_ref[0][5:6] * acc_ref[...]
    o_ref[0] = x2 * lax.rsqrt(jnp.mean(x2 * x2, axis=-1, keepdims=True) + NORM_EPS) * fn_ref[...]


def _combine_cmlp_kernel(start_ref, fits_ref, sel_ref, y_ref, x_ref, mod0_ref, mod1_ref, n1_ref, w1_ref, vn_ref,
                         ws_ref, bs_ref, wo_ref, n2_ref, wr_ref, x1_ref, h2_ref, pt_ref, acc_ref, mix_ref, **geom):
    _scatter_add(start_ref, fits_ref, sel_ref, y_ref, acc_ref, **geom)
    x2 = x_ref[0] + mod0_ref[0][5:6] * acc_ref[...]
    _cmlp_body(x2, mod1_ref[0], n1_ref, w1_ref, vn_ref, ws_ref, bs_ref, wo_ref, n2_ref, wr_ref,
               x1_ref, h2_ref, pt_ref, mix_ref)


def _combine_specs(b, n, d, cap, tm):
    return [
        pl.BlockSpec((1, N_EXPERTS, tm), lambda bi, i, *_: (bi, 0, i)),
        pl.BlockSpec((1, N_EXPERTS, cap, d), lambda bi, i, *_: (bi, 0, 0, 0)),
        pl.BlockSpec((1, tm, d), lambda bi, i, *_: (bi, i, 0)),
        pl.BlockSpec((1, 6, d), lambda bi, i, *_: (bi, 0, 0)),
    ]


def _combine_final(route, y, x1, mod_l, final_norm, tm):
    b, n, d = x1.shape
    return pl.pallas_call(
        functools.partial(_combine_final_kernel, **route.geom),
        grid_spec=pltpu.PrefetchScalarGridSpec(
            num_scalar_prefetch=2,
            grid=(b, n // tm),
            in_specs=_combine_specs(b, n, d, route.geom["cap"], tm) + [pl.BlockSpec((1, d), lambda bi, i, *_: (0, 0))],
            out_specs=pl.BlockSpec((1, tm, d), lambda bi, i, *_: (bi, i, 0)),
            scratch_shapes=[pltpu.VMEM((tm, d), F32)],
        ),
        out_shape=jax.ShapeDtypeStruct((b, n, d), F32),
        compiler_params=_params("arbitrary", "arbitrary"),
        name="moe_combine_final",
    )(route.start, route.fits, route.sel, y, x1, mod_l, final_norm)


def _combine_chunk_mlp(route, y, x1, mod0, mod1, norm1, w1_bf, v_norm, ws_bf, bs_t, w_out_bf, norm2, wr_cat, tm):
    b, n, d = x1.shape
    const = lambda *shape: pl.BlockSpec(shape, lambda bi, i, *_: (0,) * len(shape))
    tail_in, out_specs, out_shape = _tail_specs(b, n, d, tm)
    return pl.pallas_call(
        functools.partial(_combine_cmlp_kernel, **route.geom),
        grid_spec=pltpu.PrefetchScalarGridSpec(
            num_scalar_prefetch=2,
            grid=(b, n // tm),
            in_specs=_combine_specs(b, n, d, route.geom["cap"], tm) + [
                pl.BlockSpec((1, 6, d), lambda bi, i, *_: (bi, 0, 0)),
                const(1, d), const(d, 2 * d), const(1, d), const(CMLP_GROUPS, CHUNK, CHUNK),
                const(CHUNK, CMLP_GROUPS), const(d, d),
            ] + tail_in,
            out_specs=out_specs,
            scratch_shapes=[pltpu.VMEM((tm, d), F32), pltpu.VMEM((tm, d), BF16)],
        ),
        out_shape=out_shape,
        compiler_params=_params("arbitrary", "arbitrary"),
        name="combine_chunk_mlp",
    )(route.start, route.fits, route.sel, y, x1, mod0, mod1, norm1, w1_bf, v_norm, ws_bf, bs_t, w_out_bf,
      norm2, wr_cat)


class _Routing(NamedTuple):
    sel: jax.Array
    start: jax.Array
    fits: jax.Array
    geom: dict


def _moe_experts(h2, pt, w_gate, w_up, w_down, layer):
    n = h2.shape[1]
    cap = CAPACITY_FACTOR * n // N_EXPERTS
    window = min(SLOT_WINDOW, cap)
    sub = min(SUB_TOKENS, n)
    sel, start, fits = _route(pt, cap, window)
    route = _Routing(sel, start, fits, dict(cap=cap, window=window, sub=sub, n_chunks=n // sub))
    xs, vals = _gather(route, pt, h2)
    return route, _expert_ffn(vals, xs, w_gate, w_up, w_down, layer)


def kernel(x, c, ctx, c_ctx, w_mod, b_mod, norm1, norm2, even_w_in, even_lambda, even_subln, even_conv_w,
           odd_w_in, odd_v_norm, odd_w_s, odd_b_s, w_out, w_router, w_gate, w_up, w_down, final_norm):
    b, n, d = x.shape
    depth = w_mod.shape[0]
    assert d == D_MODEL and depth == 2 and n % 256 == 0
    t_proj = min(1024, n)
    t_attn = min(1024, n)
    t_moe = min(512, n)
    t_final = min(1024, n)

    cc = jnp.concatenate([c, c_ctx[None, :]], axis=0)
    mod = _modulation(cc, w_mod, b_mod).reshape(depth, b + 1, 6, d)
    row = lambda a, l: a[l].reshape(1, -1)
    wr_pad = jnp.pad(w_router, ((0, 0), (0, 0), (0, LANES - N_EXPERTS)))
    wr_hi = wr_pad.astype(BF16)
    wr_lo = (wr_pad - wr_hi.astype(F32)).astype(BF16)
    wr_cat = jnp.concatenate([jnp.concatenate([wr_hi, wr_lo], axis=2),
                              jnp.concatenate([wr_hi, jnp.zeros_like(wr_lo)], axis=2)], axis=1)
    w_out_bf = w_out.astype(BF16)
    fnorm = final_norm.reshape(1, d)

    mod0, modc = mod[0, :b], mod[0, b:]
    w_in_bf = even_w_in[0].astype(BF16)
    q, k, vt, conv = _in_projection(x, mod0, row(norm1, 0), w_in_bf, even_conv_w[0], t_proj)
    kc, vct = _context_kv(ctx, modc, row(norm1, 0), w_in_bf)
    lam_init = 0.8 - 0.6 * math.exp(-0.3 * 0)
    attn = _attention(q, kc, k, vct, vt, even_lambda[0], even_subln[0].reshape(-1, 1), lam_init, t_attn)
    x1, h2, pt = _out_projection(attn, conv, x, mod0, w_out_bf[0], row(norm2, 0), wr_cat[0], t_proj)
    route, y = _moe_experts(h2, pt, w_gate, w_up, w_down, 0)

    mod1 = mod[1, :b]
    x1, h2, pt = _combine_chunk_mlp(route, y, x1, mod0, mod1, row(norm1, 1), odd_w_in[0].astype(BF16),
                                    row(odd_v_norm, 0), odd_w_s[0].astype(BF16), odd_b_s[0].T, w_out_bf[1],
                                    row(norm2, 1), wr_cat[1], t_moe)
    route, y = _moe_experts(h2, pt, w_gate, w_up, w_down, 1)
    return _combine_final(route, y, x1, mod1, fnorm, t_final)
```

```python
import functools
import math
from typing import NamedTuple

import numpy as np
import jax
import jax.numpy as jnp
from jax import lax
from jax.experimental import pallas as pl
from jax.experimental.pallas import tpu as pltpu

F32 = jnp.float32
BF16 = jnp.bfloat16

D_MODEL = 1024
ATTN_HEADS = 4
QK_DIM = 64
V_DIM = 128
Q_W = ATTN_HEADS * 2 * QK_DIM
CONV_W = 512
EVEN_IN = 3072
N_EXPERTS = 16
CAPACITY_FACTOR = 2
GRID_W = 64
CHUNK = 128
CMLP_GROUPS = 4
ROPE_THETA = 10000.0
NORM_EPS = 1e-6
LANES = 128
SUBLANES = 8
HALO_ROWS = SUBLANES
VMEM_LIMIT = 56 * 1024 * 1024
SAFE_SHIFT = 60.0
ALIGN_ROWS = 16
SUB_TOKENS = 256
SLOT_WINDOW = 64

NT_DIMS = (((1,), (1,)), ((), ()))
TN_DIMS = (((0,), (0,)), ((), ()))


def _dot(a, b):
    return jnp.dot(a, b, preferred_element_type=F32)


def _params(*sem):
    return pltpu.CompilerParams(dimension_semantics=sem, vmem_limit_bytes=VMEM_LIMIT)


def _rms_mod(x, g, shift, scale):
    y = x * lax.rsqrt(jnp.mean(x * x, axis=-1, keepdims=True) + NORM_EPS)
    return y * (g * (1.0 + scale)) + shift


def _split_bf16(x):
    hi = x.astype(BF16)
    lo = (x - hi.astype(F32)).astype(BF16)
    return hi, lo


def _mod_kernel(c_ref, w_ref, b_ref, o_ref):
    c = c_ref[...]
    a_hi, a_lo = _split_bf16(c * jax.nn.sigmoid(c))
    w_hi, w_lo = _split_bf16(w_ref[0])
    o_ref[0] = _dot(a_hi, w_hi) + _dot(a_lo, w_hi) + _dot(a_hi, w_lo) + b_ref[0]


def _modulation(cc, w_mod, b_mod):
    depth, d, six_d = w_mod.shape
    rows = cc.shape[0]
    tn = 1536
    return pl.pallas_call(
        _mod_kernel,
        grid=(depth, six_d // tn),
        in_specs=[
            pl.BlockSpec((rows, d), lambda l, j: (0, 0)),
            pl.BlockSpec((1, d, tn), lambda l, j: (l, 0, j)),
            pl.BlockSpec((1, 1, tn), lambda l, j: (l, 0, j)),
        ],
        out_specs=pl.BlockSpec((1, rows, tn), lambda l, j: (l, 0, j)),
        out_shape=jax.ShapeDtypeStruct((depth, rows, six_d), F32),
        compiler_params=_params("arbitrary", "arbitrary"),
        name="modulation",
    )(cc, w_mod, b_mod.reshape(depth, 1, six_d))


def _inproj_kernel(x_ref, xp_ref, xn_ref, mod_ref, n1_ref, w_ref, cos_ref, sa_ref, sb_ref, cw_ref,
                   q_ref, k_ref, vt_ref, c_ref, *, tm, n_tiles):
    i = pl.program_id(1)
    m = mod_ref[0]
    shift, scale = m[0:1], m[1:2]
    g = n1_ref[...]
    h = _rms_mod(x_ref[0], g, shift, scale).astype(BF16)

    cos, sa, sb = cos_ref[...], sa_ref[...], sb_ref[...]
    for col0, out_ref, qscale in ((0, q_ref, QK_DIM ** -0.5 * math.log2(math.e)), (Q_W, k_ref, 1.0)):
        p = _dot(h, w_ref[:, col0:col0 + Q_W])
        for j in range(Q_W // LANES):
            pj = p[:, j * LANES:(j + 1) * LANES]
            r = pj * cos + pltpu.roll(pj, LANES - 16, 1) * sa + pltpu.roll(pj, 16, 1) * sb
            out_ref[0, :, j * LANES:(j + 1) * LANES] = (r * qscale).astype(BF16)

    vt_ref[0] = _dot(h, w_ref[:, 2 * Q_W:2 * Q_W + 512]).T.astype(BF16)

    pc = _dot(h, w_ref[:, 2 * Q_W + 512:])
    gb = pc[:, :CONV_W]
    u = pc[:, CONV_W:2 * CONV_W] * pc[:, 2 * CONV_W:]
    xh = jnp.concatenate([xp_ref[0], xn_ref[0]], axis=0)
    hh = _rms_mod(xh, g, shift, scale).astype(BF16)
    ph = _dot(hh, w_ref[:, 2 * Q_W + 512 + CONV_W:])
    uh = ph[:, :CONV_W] * ph[:, CONV_W:]
    u_before = jnp.where(i > 0, uh[HALO_ROWS - 1:HALO_ROWS], 0.0)
    u_after = jnp.where(i < n_tiles - 1, uh[HALO_ROWS:HALO_ROWS + 1], 0.0)
    row = lax.broadcasted_iota(jnp.int32, (tm, 1), 0)
    u_prev = jnp.where(row == 0, u_before, pltpu.roll(u, 1, 0))
    u_next = jnp.where(row == tm - 1, u_after, pltpu.roll(u, tm - 1, 0))
    cw = cw_ref[...]
    conv = cw[0:1] * u_prev + cw[1:2] * u + cw[2:3] * u_next
    c_ref[0] = (gb * conv).astype(BF16)


def _rope_tables(n):
    rows = n // GRID_W
    row = jnp.repeat(jnp.arange(rows), GRID_W).astype(F32)
    col = jnp.tile(jnp.arange(GRID_W), rows).astype(F32)
    half = QK_DIM // 2
    inv = 1.0 / (ROPE_THETA ** (jnp.arange(0, half, 2, dtype=F32) / half))
    ang_r = row[:, None] * inv
    ang_c = col[:, None] * inv
    ang = jnp.concatenate([ang_r, ang_r, ang_c, ang_c], axis=-1)
    cos, sin = jnp.cos(ang), jnp.sin(ang)
    first_half = (jnp.arange(QK_DIM) % 32) < 16
    sin_a = jnp.where(first_half, -sin, 0.0)
    sin_b = jnp.where(first_half, 0.0, sin)
    tile2 = lambda t: jnp.concatenate([t, t], axis=-1)
    return tile2(cos), tile2(sin_a), tile2(sin_b)


def _in_projection(x, mod0, norm1, w_in_bf, conv_w, tm):
    b, n, d = x.shape
    n_tiles = n // tm
    hb = tm // HALO_ROWS
    n_hblocks = n // HALO_ROWS
    cos, sa, sb = _rope_tables(n)
    out = jax.ShapeDtypeStruct((b, n, Q_W), BF16)
    row_spec = pl.BlockSpec((1, tm, Q_W), lambda bi, i: (bi, i, 0))
    tab_spec = pl.BlockSpec((tm, LANES), lambda bi, i: (i, 0))
    return pl.pallas_call(
        functools.partial(_inproj_kernel, tm=tm, n_tiles=n_tiles),
        grid=(b, n_tiles),
        in_specs=[
            pl.BlockSpec((1, tm, d), lambda bi, i: (bi, i, 0)),
            pl.BlockSpec((1, HALO_ROWS, d), lambda bi, i: (bi, jnp.maximum(i * hb - 1, 0), 0)),
            pl.BlockSpec((1, HALO_ROWS, d), lambda bi, i: (bi, jnp.minimum((i + 1) * hb, n_hblocks - 1), 0)),
            pl.BlockSpec((1, 6, d), lambda bi, i: (bi, 0, 0)),
            pl.BlockSpec((1, d), lambda bi, i: (0, 0)),
            pl.BlockSpec((d, EVEN_IN), lambda bi, i: (0, 0)),
            tab_spec, tab_spec, tab_spec,
            pl.BlockSpec((3, CONV_W), lambda bi, i: (0, 0)),
        ],
        out_specs=[row_spec, row_spec, pl.BlockSpec((1, Q_W, tm), lambda bi, i: (bi, 0, i)), row_spec],
        out_shape=[out, out, jax.ShapeDtypeStruct((b, Q_W, n), BF16), out],
        compiler_params=_params("arbitrary", "arbitrary"),
        name="in_projection",
    )(x, x, x, mod0, norm1, w_in_bf, cos, sa, sb, conv_w)


def _ctxkv_kernel(x_ref, mod_ref, n1_ref, wk_ref, wv_ref, k_ref, vt_ref):
    bb, m_ctx, d = x_ref.shape
    m = mod_ref[0]
    h = _rms_mod(x_ref[...].reshape(bb * m_ctx, d), n1_ref[...], m[0:1], m[1:2]).astype(BF16)
    k_ref[...] = _dot(h, wk_ref[...]).astype(BF16).reshape(k_ref.shape)
    v = _dot(h, wv_ref[...])
    for j in range(bb):
        vt_ref[j] = v[j * m_ctx:(j + 1) * m_ctx].T.astype(BF16)


def _context_kv(ctx, modc, norm1, w_in_bf):
    b, m, d = ctx.shape
    bb = next(k for k in (4, 2, 1) if b % k == 0)
    out = jax.ShapeDtypeStruct((b, m, Q_W), BF16)
    return pl.pallas_call(
        _ctxkv_kernel,
        grid=(b // bb,),
        in_specs=[
            pl.BlockSpec((bb, m, d), lambda bi: (bi, 0, 0)),
            pl.BlockSpec((1, 6, d), lambda bi: (0, 0, 0)),
            pl.BlockSpec((1, d), lambda bi: (0, 0)),
            pl.BlockSpec((d, Q_W), lambda bi: (0, 1)),
            pl.BlockSpec((d, Q_W), lambda bi: (0, 2)),
        ],
        out_specs=[pl.BlockSpec((bb, m, Q_W), lambda bi: (bi, 0, 0)), pl.BlockSpec((bb, Q_W, m), lambda bi: (bi, 0, 0))],
        out_shape=[out, jax.ShapeDtypeStruct((b, Q_W, m), BF16)],
        compiler_params=_params("arbitrary"),
        name="context_kv",
    )(ctx, modc, norm1, w_in_bf, w_in_bf)


def _attn_kernel(lam_ref, q_ref, qall_ref, kc_ref, k_ref, vct_ref, vt_ref, sub_ref, o_ref,
                 kx_ref, vxt_ref, shift_ref, safe_ref, s_ref, e_ref, ox_ref, *, lam_init, m_ctx):
    tq = q_ref.shape[1]

    @pl.when(pl.program_id(2) == 0)
    def _():
        kx_ref[:m_ctx, :] = kc_ref[0]
        kx_ref[m_ctx:, :] = k_ref[0]
        vxt_ref[:, :m_ctx] = vct_ref[0]
        vxt_ref[:, m_ctx:] = vt_ref[0]

        dim = lax.broadcasted_iota(jnp.int32, (LANES, LANES), 0)
        col = lax.broadcasted_iota(jnp.int32, (LANES, LANES), 1)
        ind = jnp.where(col == jnp.where(dim < QK_DIM, 0, 1), 1.0, 0.0).astype(BF16)

        def sq_norms(x):
            xf = x.astype(F32)
            return _dot((xf * xf).astype(BF16), ind)

        kmax = jnp.max(sq_norms(kx_ref[...]), axis=0, keepdims=True)
        qn = sq_norms(qall_ref[0]).T
        bound = jnp.sqrt(qn[0:SUBLANES] * jnp.concatenate(
            [kmax[:, 0:1], kmax[:, 1:2], jnp.zeros((SUBLANES - 2, 1), F32)], axis=0)) * 1.05
        shift_ref[...] = bound
        safe_ref[0] = (jnp.max(bound) <= SAFE_SHIFT).astype(jnp.int32)

    lp = lam_ref[...]
    lam = (jnp.exp(jnp.sum(lp[0:1] * lp[1:2], keepdims=True))
           - jnp.exp(jnp.sum(lp[2:3] * lp[3:4], keepdims=True)) + lam_init)
    q = q_ref[0]
    lane = lax.broadcasted_iota(jnp.int32, q.shape, 1)
    zero = jnp.zeros_like(q)
    qms = [jnp.where(in_map, q, zero) for in_map in (lane < QK_DIM, lane >= QK_DIM)]
    q0 = pl.multiple_of(pl.program_id(2) * tq, tq)
    shifts = [shift_ref[mi:mi + 1, pl.ds(q0, tq)] for mi in range(2)]
    bound_is_safe = safe_ref[0] == 1

    def weighted_values(mi, e32):
        e_ref[mi] = e32.astype(BF16)
        ox_ref[mi, :V_DIM, :] = _dot(vxt_ref[...], e_ref[mi])
        ox_ref[mi, V_DIM:, :] = jnp.broadcast_to(jnp.sum(e32, axis=0, keepdims=True), (SUBLANES, tq))

    @pl.when(bound_is_safe)
    def _():
        for mi in range(2):
            st = lax.dot_general(kx_ref[...], qms[mi], NT_DIMS, preferred_element_type=F32)
            weighted_values(mi, jnp.exp2(st - shifts[mi]))

    @pl.when(jnp.logical_not(bound_is_safe))
    def _():
        for mi in range(2):
            s_ref[mi] = lax.dot_general(kx_ref[...], qms[mi], NT_DIMS, preferred_element_type=F32)
        for mi in range(2):
            weighted_values(mi, jnp.exp2(s_ref[mi] - jnp.max(s_ref[mi], axis=0, keepdims=True)))

    o1, l1 = ox_ref[0, :V_DIM, :], ox_ref[0, V_DIM:V_DIM + 1, :]
    o2, l2 = ox_ref[1, :V_DIM, :], ox_ref[1, V_DIM:V_DIM + 1, :]
    ot = o1 * (1.0 / l1) - o2 * (lam / l2)
    ot = ot * lax.rsqrt(jnp.mean(ot * ot, axis=0, keepdims=True) + NORM_EPS)
    o_ref[0] = (ot * sub_ref[...] * (1.0 - lam_init)).T.astype(BF16)


def _attention(q, kc, k, vct, vt, lam_p, subln, lam_init, tq):
    b, n, _ = q.shape
    m_ctx = kc.shape[1]
    head_rows = lambda rows: pl.BlockSpec((1, rows, V_DIM), lambda bi, h, i: (bi, 0, h))
    head_cols = lambda cols: pl.BlockSpec((1, V_DIM, cols), lambda bi, h, i: (bi, h, 0))
    return pl.pallas_call(
        functools.partial(_attn_kernel, lam_init=lam_init, m_ctx=m_ctx),
        grid=(b, ATTN_HEADS, n // tq),
        in_specs=[
            pl.BlockSpec((4, QK_DIM), lambda bi, h, i: (0, 0)),
            pl.BlockSpec((1, tq, V_DIM), lambda bi, h, i: (bi, i, h)),
            head_rows(n), head_rows(m_ctx), head_rows(n), head_cols(m_ctx), head_cols(n),
            pl.BlockSpec((V_DIM, 1), lambda bi, h, i: (0, 0)),
        ],
        out_specs=pl.BlockSpec((1, tq, V_DIM), lambda bi, h, i: (bi, i, h)),
        out_shape=jax.ShapeDtypeStruct((b, n, Q_W), BF16),
        scratch_shapes=[pltpu.VMEM((m_ctx + n, V_DIM), BF16), pltpu.VMEM((V_DIM, m_ctx + n), BF16),
                        pltpu.VMEM((SUBLANES, n), F32), pltpu.SMEM((1,), jnp.int32),
                        pltpu.VMEM((2, m_ctx + n, tq), F32), pltpu.VMEM((2, m_ctx + n, tq), BF16),
                        pltpu.VMEM((2, V_DIM + SUBLANES, tq), F32)],
        compiler_params=_params("arbitrary", "arbitrary", "arbitrary"),
        name="diff_attention",
    )(lam_p, q, q, kc, k, vct, vt, subln)


def _tail(y, x, m, n2, wr_ref, x1_ref, h2_ref, pt_ref):
    x1 = x + m[2:3] * y
    x1_ref[0] = x1
    h2 = _rms_mod(x1, n2, m[3:4], m[4:5])
    h_hi, h_lo = _split_bf16(h2)
    h2_ref[0] = h_hi
    hcat = jnp.concatenate([h_hi, h_lo], axis=1)
    half = hcat.shape[0] // 2
    parts = [_dot(hcat[r0:r0 + half], wr_ref[...]) for r0 in (0, half)]
    prod = jnp.concatenate(parts, axis=0)
    logits = (prod[:, :LANES] + prod[:, LANES:]).T[:N_EXPERTS]
    ex = jnp.exp(logits - jnp.max(logits, axis=0, keepdims=True))
    pt_ref[0] = ex / jnp.sum(ex, axis=0, keepdims=True)


def _tail_specs(b, n, d, tm):
    in_specs = [
        pl.BlockSpec((1, d), lambda bi, i, *_: (0, 0)),
        pl.BlockSpec((2 * d, 2 * LANES), lambda bi, i, *_: (0, 0)),
    ]
    out_specs = [
        pl.BlockSpec((1, tm, d), lambda bi, i, *_: (bi, i, 0)),
        pl.BlockSpec((1, tm, d), lambda bi, i, *_: (bi, i, 0)),
        pl.BlockSpec((1, N_EXPERTS, tm), lambda bi, i, *_: (bi, 0, i)),
    ]
    out_shape = [
        jax.ShapeDtypeStruct((b, n, d), F32),
        jax.ShapeDtypeStruct((b, n, d), BF16),
        jax.ShapeDtypeStruct((b, N_EXPERTS, n), F32),
    ]
    return in_specs, out_specs, out_shape


def _outproj_kernel(a_ref, c_ref, x_ref, mod_ref, wo_ref, n2_ref, wr_ref,
                    x1_ref, h2_ref, pt_ref, mix_ref):
    mix_ref[:, :Q_W] = a_ref[0]
    mix_ref[:, Q_W:] = c_ref[0]
    y = _dot(mix_ref[...], wo_ref[...])
    _tail(y, x_ref[0], mod_ref[0], n2_ref[...], wr_ref, x1_ref, h2_ref, pt_ref)


def _out_projection(attn, conv, x, mod0, w_out_bf, norm2, wr_cat, tm):
    b, n, d = x.shape
    tail_in, out_specs, out_shape = _tail_specs(b, n, d, tm)
    return pl.pallas_call(
        _outproj_kernel,
        grid=(b, n // tm),
        in_specs=[
            pl.BlockSpec((1, tm, Q_W), lambda bi, i: (bi, i, 0)),
            pl.BlockSpec((1, tm, CONV_W), lambda bi, i: (bi, i, 0)),
            pl.BlockSpec((1, tm, d), lambda bi, i: (bi, i, 0)),
            pl.BlockSpec((1, 6, d), lambda bi, i: (bi, 0, 0)),
            pl.BlockSpec((d, d), lambda bi, i: (0, 0)),
        ] + tail_in,
        out_specs=out_specs,
        out_shape=out_shape,
        scratch_shapes=[pltpu.VMEM((tm, d), BF16)],
        compiler_params=_params("arbitrary", "arbitrary"),
        name="out_projection",
    )(attn, conv, x, mod0, w_out_bf, norm2, wr_cat)


def _gelu(x):
    return 0.5 * x * (1.0 + lax.erf(x * np.float32(1.0 / math.sqrt(2.0))))


def _cmlp_body(x, m, n1_ref, w1_ref, vn_ref, ws_ref, bs_ref, wo_ref, n2_ref, wr_ref,
               x1_ref, h2_ref, pt_ref, mix_ref):
    tm, d = x.shape
    h = _rms_mod(x, n1_ref[...], m[0:1], m[1:2]).astype(BF16)
    p = _gelu(_dot(h, w1_ref[...]))
    u, v = p[:, :d], p[:, d:]
    v = v * lax.rsqrt(jnp.mean(v * v, axis=-1, keepdims=True) + NORM_EPS) * vn_ref[...]
    vb = v.astype(BF16)
    gw = d // CMLP_GROUPS
    bs = bs_ref[...]
    for c in range(tm // CHUNK):
        r0 = c * CHUNK
        for g in range(CMLP_GROUPS):
            s = _dot(ws_ref[g], vb[r0:r0 + CHUNK, g * gw:(g + 1) * gw]) + bs[:, g:g + 1]
            mix_ref[r0:r0 + CHUNK, g * gw:(g + 1) * gw] = (u[r0:r0 + CHUNK, g * gw:(g + 1) * gw] * s).astype(BF16)
    y = _dot(mix_ref[...], wo_ref[...])
    _tail(y, x, m, n2_ref[...], wr_ref, x1_ref, h2_ref, pt_ref)


def _route_kernel(p_ref, sel_ref, start_ref, fits_ref, *, cap, pchunk, window):
    rows, n = p_ref.shape

    def bit_step(it, t):
        cand = t | jnp.left_shift(jnp.int32(1), 30 - it)
        cf = lax.bitcast_convert_type(cand, F32)
        cnt = jnp.sum(jnp.where(p_ref[...] >= cf, 1.0, 0.0), axis=1, keepdims=True)
        return jnp.where(cnt >= cap, cand, t)

    t = lax.fori_loop(0, 31, bit_step, jnp.zeros((rows, 1), jnp.int32))
    tf = lax.bitcast_convert_type(t, F32)
    p = p_ref[...]
    gt = p > tf
    eq = jnp.logical_and(p >= tf, jnp.logical_not(gt))
    need = cap - jnp.sum(jnp.where(gt, 1.0, 0.0), axis=1, keepdims=True)

    ri = lax.broadcasted_iota(jnp.int32, (pchunk, pchunk), 0)
    ci = lax.broadcasted_iota(jnp.int32, (pchunk, pchunk), 1)
    upper = jnp.where(ri < ci, 1.0, 0.0).astype(BF16)

    def prefix(mask):
        mf = jnp.where(mask, 1.0, 0.0)
        carry = jnp.zeros((rows, 1), F32)
        parts, before, inside = [], [], []
        for c in range(n // pchunk):
            blk = mf[:, c * pchunk:(c + 1) * pchunk]
            parts.append(_dot(blk.astype(BF16), upper) + carry)
            total = jnp.sum(blk, axis=1, keepdims=True)
            before.append(carry)
            inside.append(total)
            carry = carry + total
        return jnp.concatenate(parts, axis=1), jnp.concatenate(before, axis=1), jnp.concatenate(inside, axis=1)

    chosen = jnp.logical_or(gt, jnp.logical_and(eq, prefix(eq)[0] < need))
    rank, before, inside = prefix(chosen)
    sel_ref[...] = jnp.where(chosen, rank, -1.0)
    start = jnp.minimum(jnp.floor(before * (1.0 / ALIGN_ROWS)) * ALIGN_ROWS, float(cap - window))
    start_ref[...] = start.astype(jnp.int32)
    fits_ref[...] = jnp.where(before + inside <= start + window, 1, 0).astype(jnp.int32)


def _route(pt, cap, window):
    b, e, n = pt.shape
    rows = b * e
    pchunk = min(SUB_TOKENS, n)
    n_chunks = n // pchunk
    full = lambda cols: pl.BlockSpec((rows, cols), lambda i: (0, 0))
    sel, start, fits = pl.pallas_call(
        functools.partial(_route_kernel, cap=cap, pchunk=pchunk, window=window),
        grid=(1,),
        in_specs=[full(n)],
        out_specs=[full(n), full(n_chunks), full(n_chunks)],
        out_shape=[jax.ShapeDtypeStruct((rows, n), F32), jax.ShapeDtypeStruct((rows, n_chunks), jnp.int32),
                   jax.ShapeDtypeStruct((rows, n_chunks), jnp.int32)],
        compiler_params=_params("arbitrary"),
        name="route",
    )(pt.reshape(rows, n))
    return sel.reshape(b, e, n), start.reshape(-1), fits.reshape(-1)


def _gather_kernel(start_ref, fits_ref, sel_ref, pt_ref, h_ref, xs_ref, vals_ref, *, cap, window, sub, n_chunks):
    bi = pl.program_id(0)
    per_pass = sub // window
    entry = lambda e, c: (bi * N_EXPERTS + e) * n_chunks + c
    lane = lax.broadcasted_iota(jnp.int32, (1, LANES), 1)

    all_fit = fits_ref[entry(0, 0)]
    for e in range(N_EXPERTS):
        for c in range(n_chunks):
            all_fit = jnp.minimum(all_fit, fits_ref[entry(e, c)])

    vals_ref[...] = jnp.zeros(vals_ref.shape, F32)

    @pl.when(all_fit == 1)
    def _():
        xs_ref[...] = jnp.zeros(xs_ref.shape, BF16)
        offs = lax.broadcasted_iota(jnp.int32, (window, sub), 0).astype(F32)
        for c in range(n_chunks):
            sel = sel_ref[0, :, c * sub:(c + 1) * sub]
            prob = pt_ref[0, :, c * sub:(c + 1) * sub]
            hc = h_ref[0, c * sub:(c + 1) * sub, :]
            for g in range(N_EXPERTS // per_pass):
                group = range(g * per_pass, (g + 1) * per_pass)
                starts = [pl.multiple_of(start_ref[entry(e, c)], ALIGN_ROWS) for e in group]
                hits = [sel[e:e + 1, :] - a.astype(F32) == offs for e, a in zip(group, starts)]
                hots = [jnp.where(hit, 1.0, 0.0).astype(BF16) for hit in hits]
                z = _dot(jnp.concatenate(hots, axis=0), hc).astype(BF16)
                for k, (e, a) in enumerate(zip(group, starts)):
                    xs_ref[0, e, pl.ds(a, window), :] += z[k * window:(k + 1) * window]
                    w = jnp.sum(jnp.where(hits[k], prob[e:e + 1, :], 0.0), axis=1, keepdims=True)
                    vals_ref[0, pl.ds(a, window), :] += jnp.where(lane == e, w, 0.0)

    @pl.when(all_fit != 1)
    def _():
        n = sel_ref.shape[-1]
        rank = lax.broadcasted_iota(jnp.int32, (cap, n), 0).astype(F32)
        for e in range(N_EXPERTS):
            hit = sel_ref[0, e:e + 1, :] == rank
            xs_ref[0, e] = _dot(jnp.where(hit, 1.0, 0.0).astype(BF16), h_ref[0]).astype(BF16)
            w = jnp.sum(jnp.where(hit, pt_ref[0, e:e + 1, :], 0.0), axis=1, keepdims=True)
            vals_ref[0] += jnp.where(lane == e, w, 0.0)


def _gather(route, pt, h2):
    b, n, d = h2.shape
    cap = route.geom["cap"]
    experts_by_tokens = pl.BlockSpec((1, N_EXPERTS, n), lambda bi, *_: (bi, 0, 0))
    return pl.pallas_call(
        functools.partial(_gather_kernel, **route.geom),
        grid_spec=pltpu.PrefetchScalarGridSpec(
            num_scalar_prefetch=2,
            grid=(b,),
            in_specs=[experts_by_tokens, experts_by_tokens, pl.BlockSpec((1, n, d), lambda bi, *_: (bi, 0, 0))],
            out_specs=[pl.BlockSpec((1, N_EXPERTS, cap, d), lambda bi, *_: (bi, 0, 0, 0)),
                       pl.BlockSpec((1, cap, LANES), lambda bi, *_: (bi, 0, 0))],
        ),
        out_shape=[jax.ShapeDtypeStruct((b, N_EXPERTS, cap, d), BF16), jax.ShapeDtypeStruct((b, cap, LANES), F32)],
        compiler_params=_params("arbitrary"),
        name="moe_gather",
    )(route.start, route.fits, route.sel, pt, h2)


def _expert_kernel(vals_ref, xs_ref, wg_ref, wu_ref, wd_ref, y_ref, wgu_s, wd_s):
    e = pl.program_id(0)
    hdim = wd_s.shape[0]

    @pl.when(pl.program_id(1) == 0)
    def _():
        wgu_s[:, :hdim] = wg_ref[0, 0].astype(BF16)
        wgu_s[:, hdim:] = wu_ref[0, 0].astype(BF16)
        wd_s[...] = wd_ref[0, 0].astype(BF16)

    bb, cap, _ = vals_ref.shape
    d = xs_ref.shape[-1]
    lane = lax.broadcasted_iota(jnp.int32, (1, 1, LANES), 2)
    vals = jnp.sum(jnp.where(lane == e, vals_ref[...], 0.0), axis=2, keepdims=True).reshape(bb * cap, 1)
    xs = xs_ref[...].reshape(bb * cap, d)
    ab = _dot(xs, wgu_s[...])
    a, bm = ab[:, :hdim], ab[:, hdim:]
    hid = (a * jax.nn.sigmoid(a) * bm * vals).astype(BF16)
    y_ref[...] = _dot(hid, wd_s[...]).astype(BF16).reshape(y_ref.shape)


def _expert_ffn(vals, xs, w_gate, w_up, w_down, layer):
    b, _, cap, d = xs.shape
    hdim = w_gate.shape[-1]
    bb = next(k for k in (4, 2, 1) if b % k == 0)
    w_spec = lambda rows, cols: pl.BlockSpec((1, 1, rows, cols), lambda e, bi: (layer, e, 0, 0))
    return pl.pallas_call(
        _expert_kernel,
        grid=(N_EXPERTS, b // bb),
        in_specs=[
            pl.BlockSpec((bb, cap, LANES), lambda e, bi: (bi, 0, 0)),
            pl.BlockSpec((bb, 1, cap, d), lambda e, bi: (bi, e, 0, 0)),
            w_spec(d, hdim), w_spec(d, hdim), w_spec(hdim, d),
        ],
        out_specs=pl.BlockSpec((bb, 1, cap, d), lambda e, bi: (bi, e, 0, 0)),
        out_shape=jax.ShapeDtypeStruct((b, N_EXPERTS, cap, d), BF16),
        scratch_shapes=[pltpu.VMEM((d, 2 * hdim), BF16), pltpu.VMEM((hdim, d), BF16)],
        compiler_params=_params("arbitrary", "arbitrary"),
        name="expert_ffn",
    )(vals, xs, w_gate, w_up, w_down)


def _scatter_add(start_ref, fits_ref, sel_ref, y_ref, acc_ref, *, cap, window, sub, n_chunks):
    bi, i = pl.program_id(0), pl.program_id(1)
    tm = sel_ref.shape[-1]
    n_sub = tm // sub
    per_pass = sub // window
    entry = lambda e, j: (bi * N_EXPERTS + e) * n_chunks + i * n_sub + j

    all_fit = fits_ref[entry(0, 0)]
    for e in range(N_EXPERTS):
        for j in range(n_sub):
            all_fit = jnp.minimum(all_fit, fits_ref[entry(e, j)])

    @pl.when(all_fit == 1)
    def _():
        offs = lax.broadcasted_iota(jnp.int32, (window, sub), 0).astype(F32)
        for j in range(n_sub):
            sel = sel_ref[0, :, j * sub:(j + 1) * sub]
            acc = jnp.zeros((sub, acc_ref.shape[1]), F32)
            for g in range(N_EXPERTS // per_pass):
                hots, ys = [], []
                for e in range(g * per_pass, (g + 1) * per_pass):
                    a = pl.multiple_of(start_ref[entry(e, j)], ALIGN_ROWS)
                    hots.append(jnp.where(sel[e:e + 1, :] - a.astype(F32) == offs, 1.0, 0.0).astype(BF16))
                    ys.append(y_ref[0, e, pl.ds(a, window), :])
                acc = acc + lax.dot_general(jnp.concatenate(hots, axis=0), jnp.concatenate(ys, axis=0),
                                            TN_DIMS, preferred_element_type=F32)
            acc_ref[j * sub:(j + 1) * sub, :] = acc

    @pl.when(all_fit != 1)
    def _():
        sel = sel_ref[0]
        rank = lax.broadcasted_iota(jnp.int32, (cap, tm), 0).astype(F32)
        acc = jnp.zeros(acc_ref.shape, F32)
        for e in range(N_EXPERTS):
            onehot = jnp.where(sel[e:e + 1, :] == rank, 1.0, 0.0).astype(BF16)
            acc = acc + lax.dot_general(onehot, y_ref[0, e], TN_DIMS, preferred_element_type=F32)
        acc_ref[...] = acc


def _combine_final_kernel(start_ref, fits_ref, sel_ref, y_ref, x_ref, mod_ref, fn_ref, o_ref, acc_ref, **geom):
    _scatter_add(start_ref, fits_ref, sel_ref, y_ref, acc_ref, **geom)
    x2 = x_ref[0] + mod_ref[0][5:6] * acc_ref[...]
    o_ref[0] = x2 * lax.rsqrt(jnp.mean(x2 * x2, axis=-1, keepdims=True) + NORM_EPS) * fn_ref[...]


def _combine_cmlp_kernel(start_ref, fits_ref, sel_ref, y_ref, x_ref, mod0_ref, mod1_ref, n1_ref, w1_ref, vn_ref,
                         ws_ref, bs_ref, wo_ref, n2_ref, wr_ref, x1_ref, h2_ref, pt_ref, acc_ref, mix_ref, **geom):
    _scatter_add(start_ref, fits_ref, sel_ref, y_ref, acc_ref, **geom)
    x2 = x_ref[0] + mod0_ref[0][5:6] * acc_ref[...]
    _cmlp_body(x2, mod1_ref[0], n1_ref, w1_ref, vn_ref, ws_ref, bs_ref, wo_ref, n2_ref, wr_ref,
               x1_ref, h2_ref, pt_ref, mix_ref)


def _combine_specs(b, n, d, cap, tm):
    return [
        pl.BlockSpec((1, N_EXPERTS, tm), lambda bi, i, *_: (bi, 0, i)),
        pl.BlockSpec((1, N_EXPERTS, cap, d), lambda bi, i, *_: (bi, 0, 0, 0)),
        pl.BlockSpec((1, tm, d), lambda bi, i, *_: (bi, i, 0)),
        pl.BlockSpec((1, 6, d), lambda bi, i, *_: (bi, 0, 0)),
    ]


def _combine_final(route, y, x1, mod_l, final_norm, tm):
    b, n, d = x1.shape
    return pl.pallas_call(
        functools.partial(_combine_final_kernel, **route.geom),
        grid_spec=pltpu.PrefetchScalarGridSpec(
            num_scalar_prefetch=2,
            grid=(b, n // tm),
            in_specs=_combine_specs(b, n, d, route.geom["cap"], tm) + [pl.BlockSpec((1, d), lambda bi, i, *_: (0, 0))],
            out_specs=pl.BlockSpec((1, tm, d), lambda bi, i, *_: (bi, i, 0)),
            scratch_shapes=[pltpu.VMEM((tm, d), F32)],
        ),
        out_shape=jax.ShapeDtypeStruct((b, n, d), F32),
        compiler_params=_params("arbitrary", "arbitrary"),
        name="moe_combine_final",
    )(route.start, route.fits, route.sel, y, x1, mod_l, final_norm)


def _combine_chunk_mlp(route, y, x1, mod0, mod1, norm1, w1_bf, v_norm, ws_bf, bs_t, w_out_bf, norm2, wr_cat, tm):
    b, n, d = x1.shape
    const = lambda *shape: pl.BlockSpec(shape, lambda bi, i, *_: (0,) * len(shape))
    tail_in, out_specs, out_shape = _tail_specs(b, n, d, tm)
    return pl.pallas_call(
        functools.partial(_combine_cmlp_kernel, **route.geom),
        grid_spec=pltpu.PrefetchScalarGridSpec(
            num_scalar_prefetch=2,
            grid=(b, n // tm),
            in_specs=_combine_specs(b, n, d, route.geom["cap"], tm) + [
                pl.BlockSpec((1, 6, d), lambda bi, i, *_: (bi, 0, 0)),
                const(1, d), const(d, 2 * d), const(1, d), const(CMLP_GROUPS, CHUNK, CHUNK),
                const(CHUNK, CMLP_GROUPS), const(d, d),
            ] + tail_in,
            out_specs=out_specs,
            scratch_shapes=[pltpu.VMEM((tm, d), F32), pltpu.VMEM((tm, d), BF16)],
        ),
        out_shape=out_shape,
        compiler_params=_params("arbitrary", "arbitrary"),
        name="combine_chunk_mlp",
    )(route.start, route.fits, route.sel, y, x1, mod0, mod1, norm1, w1_bf, v_norm, ws_bf, bs_t, w_out_bf,
      norm2, wr_cat)


class _Routing(NamedTuple):
    sel: jax.Array
    start: jax.Array
    fits: jax.Array
    geom: dict


def _moe_experts(h2, pt, w_gate, w_up, w_down, layer):
    n = h2.shape[1]
    cap = CAPACITY_FACTOR * n // N_EXPERTS
    window = min(SLOT_WINDOW, cap)
    sub = min(SUB_TOKENS, n)
    sel, start, fits = _route(pt, cap, window)
    route = _Routing(sel, start, fits, dict(cap=cap, window=window, sub=sub, n_chunks=n // sub))
    xs, vals = _gather(route, pt, h2)
    return route, _expert_ffn(vals, xs, w_gate, w_up, w_down, layer)


def kernel(x, c, ctx, c_ctx, w_mod, b_mod, norm1, norm2, even_w_in, even_lambda, even_subln, even_conv_w,
           odd_w_in, odd_v_norm, odd_w_s, odd_b_s, w_out, w_router, w_gate, w_up, w_down, final_norm):
    b, n, d = x.shape
    depth = w_mod.shape[0]
    assert d == D_MODEL and depth == 2 and n % 256 == 0
    t_proj = min(1024, n)
    t_attn = min(1024, n)
    t_moe = min(512, n)
    t_final = min(1024, n)

    cc = jnp.concatenate([c, c_ctx[None, :]], axis=0)
    mod = _modulation(cc, w_mod, b_mod).reshape(depth, b + 1, 6, d)
    row = lambda a, l: a[l].reshape(1, -1)
    wr_pad = jnp.pad(w_router, ((0, 0), (0, 0), (0, LANES - N_EXPERTS)))
    wr_hi = wr_pad.astype(BF16)
    wr_lo = (wr_pad - wr_hi.astype(F32)).astype(BF16)
    wr_cat = jnp.concatenate([jnp.concatenate([wr_hi, wr_lo], axis=2),
                              jnp.concatenate([wr_hi, jnp.zeros_like(wr_lo)], axis=2)], axis=1)
    w_out_bf = w_out.astype(BF16)
    fnorm = final_norm.reshape(1, d)

    mod0, modc = mod[0, :b], mod[0, b:]
    w_in_bf = even_w_in[0].astype(BF16)
    q, k, vt, conv = _in_projection(x, mod0, row(norm1, 0), w_in_bf, even_conv_w[0], t_proj)
    kc, vct = _context_kv(ctx, modc, row(norm1, 0), w_in_bf)
    lam_init = 0.8 - 0.6 * math.exp(-0.3 * 0)
    attn = _attention(q, kc, k, vct, vt, even_lambda[0], even_subln[0].reshape(-1, 1), lam_init, t_attn)
    x1, h2, pt = _out_projection(attn, conv, x, mod0, w_out_bf[0], row(norm2, 0), wr_cat[0], t_proj)
    route, y = _moe_experts(h2, pt, w_gate, w_up, w_down, 0)

    mod1 = mod[1, :b]
    x1, h2, pt = _combine_chunk_mlp(route, y, x1, mod0, mod1, row(norm1, 1), odd_w_in[0].astype(BF16),
                                    row(odd_v_norm, 0), odd_w_s[0].astype(BF16), odd_b_s[0].T, w_out_bf[1],
                                    row(norm2, 1), wr_cat[1], t_moe)
    route, y = _moe_experts(h2, pt, w_gate, w_up, w_down, 1)
    return _combine_final(route, y, x1, mod1, fnorm, t_final)
```

```python
import functools
import math
from typing import NamedTuple

import numpy as np
import jax
import jax.numpy as jnp
from jax import lax
from jax.experimental import pallas as pl
from jax.experimental.pallas import tpu as pltpu

F32 = jnp.float32
BF16 = jnp.bfloat16

D_MODEL = 1024
ATTN_HEADS = 4
QK_DIM = 64
V_DIM = 128
Q_W = ATTN_HEADS * 2 * QK_DIM
CONV_W = 512
EVEN_IN = 3072
N_EXPERTS = 16
CAPACITY_FACTOR = 2
GRID_W = 64
CHUNK = 128
CMLP_GROUPS = 4
ROPE_THETA = 10000.0
NORM_EPS = 1e-6
LANES = 128
SUBLANES = 8
HALO_ROWS = SUBLANES
VMEM_LIMIT = 56 * 1024 * 1024
SAFE_SHIFT = 60.0
ALIGN_ROWS = 16
SUB_TOKENS = 256
SLOT_WINDOW = 64

NT_DIMS = (((1,), (1,)), ((), ()))
TN_DIMS = (((0,), (0,)), ((), ()))


def _dot(a, b):
    return jnp.dot(a, b, preferred_element_type=F32)


def _params(*sem):
    return pltpu.CompilerParams(dimension_semantics=sem, vmem_limit_bytes=VMEM_LIMIT)


def _rms_mod(x, g, shift, scale):
    y = x * lax.rsqrt(jnp.mean(x * x, axis=-1, keepdims=True) + NORM_EPS)
    return y * (g * (1.0 + scale)) + shift


def _split_bf16(x):
    hi = x.astype(BF16)
    lo = (x - hi.astype(F32)).astype(BF16)
    return hi, lo


def _mod_kernel(c_ref, w_ref, b_ref, o_ref):
    c = c_ref[...]
    a_hi, a_lo = _split_bf16(c * jax.nn.sigmoid(c))
    w_hi, w_lo = _split_bf16(w_ref[0])
    o_ref[0] = _dot(a_hi, w_hi) + _dot(a_lo, w_hi) + _dot(a_hi, w_lo) + b_ref[0]


def _modulation(cc, w_mod, b_mod):
    depth, d, six_d = w_mod.shape
    rows = cc.shape[0]
    tn = 1536
    return pl.pallas_call(
        _mod_kernel,
        grid=(depth, six_d // tn),
        in_specs=[
            pl.BlockSpec((rows, d), lambda l, j: (0, 0)),
            pl.BlockSpec((1, d, tn), lambda l, j: (l, 0, j)),
            pl.BlockSpec((1, 1, tn), lambda l, j: (l, 0, j)),
        ],
        out_specs=pl.BlockSpec((1, rows, tn), lambda l, j: (l, 0, j)),
        out_shape=jax.ShapeDtypeStruct((depth, rows, six_d), F32),
        compiler_params=_params("arbitrary", "arbitrary"),
        name="modulation",
    )(cc, w_mod, b_mod.reshape(depth, 1, six_d))


def _inproj_kernel(x_ref, xp_ref, xn_ref, mod_ref, n1_ref, w_ref, cos_ref, sa_ref, sb_ref, cw_ref,
                   q_ref, k_ref, vt_ref, c_ref, *, tm, n_tiles):
    i = pl.program_id(1)
    m = mod_ref[0]
    shift, scale = m[0:1], m[1:2]
    g = n1_ref[...]
    h = _rms_mod(x_ref[0], g, shift, scale).astype(BF16)

    cos, sa, sb = cos_ref[...], sa_ref[...], sb_ref[...]
    for col0, out_ref, qscale in ((0, q_ref, QK_DIM ** -0.5 * math.log2(math.e)), (Q_W, k_ref, 1.0)):
        p = _dot(h, w_ref[:, col0:col0 + Q_W])
        for j in range(Q_W // LANES):
            pj = p[:, j * LANES:(j + 1) * LANES]
            r = pj * cos + pltpu.roll(pj, LANES - 16, 1) * sa + pltpu.roll(pj, 16, 1) * sb
            out_ref[0, :, j * LANES:(j + 1) * LANES] = (r * qscale).astype(BF16)

    vt_ref[0] = _dot(h, w_ref[:, 2 * Q_W:2 * Q_W + 512]).T.astype(BF16)

    pc = _dot(h, w_ref[:, 2 * Q_W + 512:])
    gb = pc[:, :CONV_W]
    u = pc[:, CONV_W:2 * CONV_W] * pc[:, 2 * CONV_W:]
    xh = jnp.concatenate([xp_ref[0], xn_ref[0]], axis=0)
    hh = _rms_mod(xh, g, shift, scale).astype(BF16)
    ph = _dot(hh, w_ref[:, 2 * Q_W + 512 + CONV_W:])
    uh = ph[:, :CONV_W] * ph[:, CONV_W:]
    u_before = jnp.where(i > 0, uh[HALO_ROWS - 1:HALO_ROWS], 0.0)
    u_after = jnp.where(i < n_tiles - 1, uh[HALO_ROWS:HALO_ROWS + 1], 0.0)
    row = lax.broadcasted_iota(jnp.int32, (tm, 1), 0)
    u_prev = jnp.where(row == 0, u_before, pltpu.roll(u, 1, 0))
    u_next = jnp.where(row == tm - 1, u_after, pltpu.roll(u, tm - 1, 0))
    cw = cw_ref[...]
    conv = cw[0:1] * u_prev + cw[1:2] * u + cw[2:3] * u_next
    c_ref[0] = (gb * conv).astype(BF16)


def _rope_tables(n):
    rows = n // GRID_W
    row = jnp.repeat(jnp.arange(rows), GRID_W).astype(F32)
    col = jnp.tile(jnp.arange(GRID_W), rows).astype(F32)
    half = QK_DIM // 2
    inv = 1.0 / (ROPE_THETA ** (jnp.arange(0, half, 2, dtype=F32) / half))
    ang_r = row[:, None] * inv
    ang_c = col[:, None] * inv
    ang = jnp.concatenate([ang_r, ang_r, ang_c, ang_c], axis=-1)
    cos, sin = jnp.cos(ang), jnp.sin(ang)
    first_half = (jnp.arange(QK_DIM) % 32) < 16
    sin_a = jnp.where(first_half, -sin, 0.0)
    sin_b = jnp.where(first_half, 0.0, sin)
    tile2 = lambda t: jnp.concatenate([t, t], axis=-1)
    return tile2(cos), tile2(sin_a), tile2(sin_b)


def _in_projection(x, mod0, norm1, w_in_bf, conv_w, tm):
    b, n, d = x.shape
    n_tiles = n // tm
    hb = tm // HALO_ROWS
    n_hblocks = n // HALO_ROWS
    cos, sa, sb = _rope_tables(n)
    out = jax.ShapeDtypeStruct((b, n, Q_W), BF16)
    row_spec = pl.BlockSpec((1, tm, Q_W), lambda bi, i: (bi, i, 0))
    tab_spec = pl.BlockSpec((tm, LANES), lambda bi, i: (i, 0))
    return pl.pallas_call(
        functools.partial(_inproj_kernel, tm=tm, n_tiles=n_tiles),
        grid=(b, n_tiles),
        in_specs=[
            pl.BlockSpec((1, tm, d), lambda bi, i: (bi, i, 0)),
            pl.BlockSpec((1, HALO_ROWS, d), lambda bi, i: (bi, jnp.maximum(i * hb - 1, 0), 0)),
            pl.BlockSpec((1, HALO_ROWS, d), lambda bi, i: (bi, jnp.minimum((i + 1) * hb, n_hblocks - 1), 0)),
            pl.BlockSpec((1, 6, d), lambda bi, i: (bi, 0, 0)),
            pl.BlockSpec((1, d), lambda bi, i: (0, 0)),
            pl.BlockSpec((d, EVEN_IN), lambda bi, i: (0, 0)),
            tab_spec, tab_spec, tab_spec,
            pl.BlockSpec((3, CONV_W), lambda bi, i: (0, 0)),
        ],
        out_specs=[row_spec, row_spec, pl.BlockSpec((1, Q_W, tm), lambda bi, i: (bi, 0, i)), row_spec],
        out_shape=[out, out, jax.ShapeDtypeStruct((b, Q_W, n), BF16), out],
        compiler_params=_params("arbitrary", "arbitrary"),
        name="in_projection",
    )(x, x, x, mod0, norm1, w_in_bf, cos, sa, sb, conv_w)


def _ctxkv_kernel(x_ref, mod_ref, n1_ref, wk_ref, wv_ref, k_ref, vt_ref):
    bb, m_ctx, d = x_ref.shape
    m = mod_ref[0]
    h = _rms_mod(x_ref[...].reshape(bb * m_ctx, d), n1_ref[...], m[0:1], m[1:2]).astype(BF16)
    k_ref[...] = _dot(h, wk_ref[...]).astype(BF16).reshape(k_ref.shape)
    v = _dot(h, wv_ref[...])
    for j in range(bb):
        vt_ref[j] = v[j * m_ctx:(j + 1) * m_ctx].T.astype(BF16)


def _context_kv(ctx, modc, norm1, w_in_bf):
    b, m, d = ctx.shape
    bb = next(k for k in (4, 2, 1) if b % k == 0)
    out = jax.ShapeDtypeStruct((b, m, Q_W), BF16)
    return pl.pallas_call(
        _ctxkv_kernel,
        grid=(b // bb,),
        in_specs=[
            pl.BlockSpec((bb, m, d), lambda bi: (bi, 0, 0)),
            pl.BlockSpec((1, 6, d), lambda bi: (0, 0, 0)),
            pl.BlockSpec((1, d), lambda bi: (0, 0)),
            pl.BlockSpec((d, Q_W), lambda bi: (0, 1)),
            pl.BlockSpec((d, Q_W), lambda bi: (0, 2)),
        ],
        out_specs=[pl.BlockSpec((bb, m, Q_W), lambda bi: (bi, 0, 0)), pl.BlockSpec((bb, Q_W, m), lambda bi: (bi, 0, 0))],
        out_shape=[out, jax.ShapeDtypeStruct((b, Q_W, m), BF16)],
        compiler_params=_params("arbitrary"),
        name="context_kv",
    )(ctx, modc, norm1, w_in_bf, w_in_bf)


def _attn_kernel(lam_ref, q_ref, qall_ref, kc_ref, k_ref, vct_ref, vt_ref, sub_ref, o_ref,
                 kx_ref, vxt_ref, shift_ref, safe_ref, s_ref, e_ref, ox_ref, *, lam_init, m_ctx):
    tq = q_ref.shape[1]

    @pl.when(pl.program_id(2) == 0)
    def _():
        kx_ref[:m_ctx, :] = kc_ref[0]
        kx_ref[m_ctx:, :] = k_ref[0]
        vxt_ref[:, :m_ctx] = vct_ref[0]
        vxt_ref[:, m_ctx:] = vt_ref[0]

        dim = lax.broadcasted_iota(jnp.int32, (LANES, LANES), 0)
        col = lax.broadcasted_iota(jnp.int32, (LANES, LANES), 1)
        ind = jnp.where(col == jnp.where(dim < QK_DIM, 0, 1), 1.0, 0.0).astype(BF16)

        def sq_norms(x):
            xf = x.astype(F32)
            return _dot((xf * xf).astype(BF16), ind)

        kmax = jnp.max(sq_norms(kx_ref[...]), axis=0, keepdims=True)
        qn = sq_norms(qall_ref[0]).T
        bound = jnp.sqrt(qn[0:SUBLANES] * jnp.concatenate(
            [kmax[:, 0:1], kmax[:, 1:2], jnp.zeros((SUBLANES - 2, 1), F32)], axis=0)) * 1.05
        shift_ref[...] = bound
        safe_ref[0] = (jnp.max(bound) <= SAFE_SHIFT).astype(jnp.int32)

    lp = lam_ref[...]
    lam = (jnp.exp(jnp.sum(lp[0:1] * lp[1:2], keepdims=True))
           - jnp.exp(jnp.sum(lp[2:3] * lp[3:4], keepdims=True)) + lam_init)
    q = q_ref[0]
    lane = lax.broadcasted_iota(jnp.int32, q.shape, 1)
    zero = jnp.zeros_like(q)
    qms = [jnp.where(in_map, q, zero) for in_map in (lane < QK_DIM, lane >= QK_DIM)]
    q0 = pl.multiple_of(pl.program_id(2) * tq, tq)
    shifts = [shift_ref[mi:mi + 1, pl.ds(q0, tq)] for mi in range(2)]
    bound_is_safe = safe_ref[0] == 1

    def weighted_values(mi, e32):
        e_ref[mi] = e32.astype(BF16)
        ox_ref[mi, :V_DIM, :] = _dot(vxt_ref[...], e_ref[mi])
        ox_ref[mi, V_DIM:, :] = jnp.broadcast_to(jnp.sum(e32, axis=0, keepdims=True), (SUBLANES, tq))

    @pl.when(bound_is_safe)
    def _():
        for mi in range(2):
            st = lax.dot_general(kx_ref[...], qms[mi], NT_DIMS, preferred_element_type=F32)
            weighted_values(mi, jnp.exp2(st - shifts[mi]))

    @pl.when(jnp.logical_not(bound_is_safe))
    def _():
        for mi in range(2):
            s_ref[mi] = lax.dot_general(kx_ref[...], qms[mi], NT_DIMS, preferred_element_type=F32)
        for mi in range(2):
            weighted_values(mi, jnp.exp2(s_ref[mi] - jnp.max(s_ref[mi], axis=0, keepdims=True)))

    o1, l1 = ox_ref[0, :V_DIM, :], ox_ref[0, V_DIM:V_DIM + 1, :]
    o2, l2 = ox_ref[1, :V_DIM, :], ox_ref[1, V_DIM:V_DIM + 1, :]
    ot = o1 * (1.0 / l1) - o2 * (lam / l2)
    ot = ot * lax.rsqrt(jnp.mean(ot * ot, axis=0, keepdims=True) + NORM_EPS)
    o_ref[0] = (ot * sub_ref[...] * (1.0 - lam_init)).T.astype(BF16)


def _attention(q, kc, k, vct, vt, lam_p, subln, lam_init, tq):
    b, n, _ = q.shape
    m_ctx = kc.shape[1]
    head_rows = lambda rows: pl.BlockSpec((1, rows, V_DIM), lambda bi, h, i: (bi, 0, h))
    head_cols = lambda cols: pl.BlockSpec((1, V_DIM, cols), lambda bi, h, i: (bi, h, 0))
    return pl.pallas_call(
        functools.partial(_attn_kernel, lam_init=lam_init, m_ctx=m_ctx),
        grid=(b, ATTN_HEADS, n // tq),
        in_specs=[
            pl.BlockSpec((4, QK_DIM), lambda bi, h, i: (0, 0)),
            pl.BlockSpec((1, tq, V_DIM), lambda bi, h, i: (bi, i, h)),
            head_rows(n), head_rows(m_ctx), head_rows(n), head_cols(m_ctx), head_cols(n),
            pl.BlockSpec((V_DIM, 1), lambda bi, h, i: (0, 0)),
        ],
        out_specs=pl.BlockSpec((1, tq, V_DIM), lambda bi, h, i: (bi, i, h)),
        out_shape=jax.ShapeDtypeStruct((b, n, Q_W), BF16),
        scratch_shapes=[pltpu.VMEM((m_ctx + n, V_DIM), BF16), pltpu.VMEM((V_DIM, m_ctx + n), BF16),
                        pltpu.VMEM((SUBLANES, n), F32), pltpu.SMEM((1,), jnp.int32),
                        pltpu.VMEM((2, m_ctx + n, tq), F32), pltpu.VMEM((2, m_ctx + n, tq), BF16),
                        pltpu.VMEM((2, V_DIM + SUBLANES, tq), F32)],
        compiler_params=_params("arbitrary", "arbitrary", "arbitrary"),
        name="diff_attention",
    )(lam_p, q, q, kc, k, vct, vt, subln)


def _tail(y, x, m, n2, wr_ref, x1_ref, h2_ref, pt_ref):
    x1 = x + m[2:3] * y
    x1_ref[0] = x1
    h2 = _rms_mod(x1, n2, m[3:4], m[4:5])
    h_hi, h_lo = _split_bf16(h2)
    h2_ref[0] = h_hi
    hcat = jnp.concatenate([h_hi, h_lo], axis=1)
    half = hcat.shape[0] // 2
    parts = [_dot(hcat[r0:r0 + half], wr_ref[...]) for r0 in (0, half)]
    prod = jnp.concatenate(parts, axis=0)
    logits = (prod[:, :LANES] + prod[:, LANES:]).T[:N_EXPERTS]
    ex = jnp.exp(logits - jnp.max(logits, axis=0, keepdims=True))
    pt_ref[0] = ex / jnp.sum(ex, axis=0, keepdims=True)


def _tail_specs(b, n, d, tm):
    in_specs = [
        pl.BlockSpec((1, d), lambda bi, i, *_: (0, 0)),
        pl.BlockSpec((2 * d, 2 * LANES), lambda bi, i, *_: (0, 0)),
    ]
    out_specs = [
        pl.BlockSpec((1, tm, d), lambda bi, i, *_: (bi, i, 0)),
        pl.BlockSpec((1, tm, d), lambda bi, i, *_: (bi, i, 0)),
        pl.BlockSpec((1, N_EXPERTS, tm), lambda bi, i, *_: (bi, 0, i)),
    ]
    out_shape = [
        jax.ShapeDtypeStruct((b, n, d), F32),
        jax.ShapeDtypeStruct((b, n, d), BF16),
        jax.ShapeDtypeStruct((b, N_EXPERTS, n), F32),
    ]
    return in_specs, out_specs, out_shape


def _outproj_kernel(a_ref, c_ref, x_ref, mod_ref, wo_ref, n2_ref, wr_ref,
                    x1_ref, h2_ref, pt_ref, mix_ref):
    mix_ref[:, :Q_W] = a_ref[0]
    mix_ref[:, Q_W:] = c_ref[0]
    y = _dot(mix_ref[...], wo_ref[...])
    _tail(y, x_ref[0], mod_ref[0], n2_ref[...], wr_ref, x1_ref, h2_ref, pt_ref)


def _out_projection(attn, conv, x, mod0, w_out_bf, norm2, wr_cat, tm):
    b, n, d = x.shape
    tail_in, out_specs, out_shape = _tail_specs(b, n, d, tm)
    return pl.pallas_call(
        _outproj_kernel,
        grid=(b, n // tm),
        in_specs=[
            pl.BlockSpec((1, tm, Q_W), lambda bi, i: (bi, i, 0)),
            pl.BlockSpec((1, tm, CONV_W), lambda bi, i: (bi, i, 0)),
            pl.BlockSpec((1, tm, d), lambda bi, i: (bi, i, 0)),
            pl.BlockSpec((1, 6, d), lambda bi, i: (bi, 0, 0)),
            pl.BlockSpec((d, d), lambda bi, i: (0, 0)),
        ] + tail_in,
        out_specs=out_specs,
        out_shape=out_shape,
        scratch_shapes=[pltpu.VMEM((tm, d), BF16)],
        compiler_params=_params("arbitrary", "arbitrary"),
        name="out_projection",
    )(attn, conv, x, mod0, w_out_bf, norm2, wr_cat)


def _gelu(x):
    return 0.5 * x * (1.0 + lax.erf(x * np.float32(1.0 / math.sqrt(2.0))))


def _cmlp_body(x, m, n1_ref, w1_ref, vn_ref, ws_ref, bs_ref, wo_ref, n2_ref, wr_ref,
               x1_ref, h2_ref, pt_ref, mix_ref):
    tm, d = x.shape
    h = _rms_mod(x, n1_ref[...], m[0:1], m[1:2]).astype(BF16)
    p = _gelu(_dot(h, w1_ref[...]))
    u, v = p[:, :d], p[:, d:]
    v = v * lax.rsqrt(jnp.mean(v * v, axis=-1, keepdims=True) + NORM_EPS) * vn_ref[...]
    vb = v.astype(BF16)
    gw = d // CMLP_GROUPS
    bs = bs_ref[...]
    for c in range(tm // CHUNK):
        r0 = c * CHUNK
        for g in range(CMLP_GROUPS):
            s = _dot(ws_ref[g], vb[r0:r0 + CHUNK, g * gw:(g + 1) * gw]) + bs[:, g:g + 1]
            mix_ref[r0:r0 + CHUNK, g * gw:(g + 1) * gw] = (u[r0:r0 + CHUNK, g * gw:(g + 1) * gw] * s).astype(BF16)
    y = _dot(mix_ref[...], wo_ref[...])
    _tail(y, x, m, n2_ref[...], wr_ref, x1_ref, h2_ref, pt_ref)


def _route_kernel(p_ref, sel_ref, start_ref, fits_ref, *, cap, pchunk, window):
    rows, n = p_ref.shape

    def bit_step(it, t):
        cand = t | jnp.left_shift(jnp.int32(1), 30 - it)
        cf = lax.bitcast_convert_type(cand, F32)
        cnt = jnp.sum(jnp.where(p_ref[...] >= cf, 1.0, 0.0), axis=1, keepdims=True)
        return jnp.where(cnt >= cap, cand, t)

    t = lax.fori_loop(0, 31, bit_step, jnp.zeros((rows, 1), jnp.int32))
    tf = lax.bitcast_convert_type(t, F32)
    p = p_ref[...]
    gt = p > tf
    eq = jnp.logical_and(p >= tf, jnp.logical_not(gt))
    need = cap - jnp.sum(jnp.where(gt, 1.0, 0.0), axis=1, keepdims=True)

    ri = lax.broadcasted_iota(jnp.int32, (pchunk, pchunk), 0)
    ci = lax.broadcasted_iota(jnp.int32, (pchunk, pchunk), 1)
    upper = jnp.where(ri < ci, 1.0, 0.0).astype(BF16)

    def prefix(mask):
        mf = jnp.where(mask, 1.0, 0.0)
        carry = jnp.zeros((rows, 1), F32)
        parts, before, inside = [], [], []
        for c in range(n // pchunk):
            blk = mf[:, c * pchunk:(c + 1) * pchunk]
            parts.append(_dot(blk.astype(BF16), upper) + carry)
            total = jnp.sum(blk, axis=1, keepdims=True)
            before.append(carry)
            inside.append(total)
            carry = carry + total
        return jnp.concatenate(parts, axis=1), jnp.concatenate(before, axis=1), jnp.concatenate(inside, axis=1)

    chosen = jnp.logical_or(gt, jnp.logical_and(eq, prefix(eq)[0] < need))
    rank, before, inside = prefix(chosen)
    sel_ref[...] = jnp.where(chosen, rank, -1.0)
    start = jnp.minimum(jnp.floor(before * (1.0 / ALIGN_ROWS)) * ALIGN_ROWS, float(cap - window))
    start_ref[...] = start.astype(jnp.int32)
    fits_ref[...] = jnp.where(before + inside <= start + window, 1, 0).astype(jnp.int32)


def _route(pt, cap, window):
    b, e, n = pt.shape
    rows = b * e
    pchunk = min(SUB_TOKENS, n)
    n_chunks = n // pchunk
    full = lambda cols: pl.BlockSpec((rows, cols), lambda i: (0, 0))
    sel, start, fits = pl.pallas_call(
        functools.partial(_route_kernel, cap=cap, pchunk=pchunk, window=window),
        grid=(1,),
        in_specs=[full(n)],
        out_specs=[full(n), full(n_chunks), full(n_chunks)],
        out_shape=[jax.ShapeDtypeStruct((rows, n), F32), jax.ShapeDtypeStruct((rows, n_chunks), jnp.int32),
                   jax.ShapeDtypeStruct((rows, n_chunks), jnp.int32)],
        compiler_params=_params("arbitrary"),
        name="route",
    )(pt.reshape(rows, n))
    return sel.reshape(b, e, n), start.reshape(-1), fits.reshape(-1)


def _gather_kernel(start_ref, fits_ref, sel_ref, pt_ref, h_ref, xs_ref, vals_ref, *, cap, window, sub, n_chunks):
    bi = pl.program_id(0)
    entry = lambda e, c: (bi * N_EXPERTS + e) * n_chunks + c
    lane = lax.broadcasted_iota(jnp.int32, (1, LANES), 1)

    all_fit = fits_ref[entry(0, 0)]
    for e in range(N_EXPERTS):
        for c in range(n_chunks):
            all_fit = jnp.minimum(all_fit, fits_ref[entry(e, c)])

    vals_ref[...] = jnp.zeros(vals_ref.shape, F32)

    @pl.when(all_fit == 1)
    def _():
        xs_ref[...] = jnp.zeros(xs_ref.shape, BF16)
        offs = lax.broadcasted_iota(jnp.int32, (window, sub), 0).astype(F32)
        for c in range(n_chunks):
            sel = sel_ref[0, :, c * sub:(c + 1) * sub]
            prob = pt_ref[0, :, c * sub:(c + 1) * sub]
            hc = h_ref[0, c * sub:(c + 1) * sub, :]
            starts = [pl.multiple_of(start_ref[entry(e, c)], ALIGN_ROWS) for e in range(N_EXPERTS)]
            hits = [sel[e:e + 1, :] - a.astype(F32) == offs for e, a in enumerate(starts)]
            hots = [jnp.where(hit, 1.0, 0.0).astype(BF16) for hit in hits]
            z = _dot(jnp.concatenate(hots, axis=0), hc).astype(BF16)
            for e, a in enumerate(starts):
                xs_ref[0, e, pl.ds(a, window), :] += z[e * window:(e + 1) * window]
                w = jnp.sum(jnp.where(hits[e], prob[e:e + 1, :], 0.0), axis=1, keepdims=True)
                vals_ref[0, pl.ds(a, window), :] += jnp.where(lane == e, w, 0.0)

    @pl.when(all_fit != 1)
    def _():
        n = sel_ref.shape[-1]
        rank = lax.broadcasted_iota(jnp.int32, (cap, n), 0).astype(F32)
        for e in range(N_EXPERTS):
            hit = sel_ref[0, e:e + 1, :] == rank
            xs_ref[0, e] = _dot(jnp.where(hit, 1.0, 0.0).astype(BF16), h_ref[0]).astype(BF16)
            w = jnp.sum(jnp.where(hit, pt_ref[0, e:e + 1, :], 0.0), axis=1, keepdims=True)
            vals_ref[0] += jnp.where(lane == e, w, 0.0)


def _gather(route, pt, h2):
    b, n, d = h2.shape
    cap = route.geom["cap"]
    experts_by_tokens = pl.BlockSpec((1, N_EXPERTS, n), lambda bi, *_: (bi, 0, 0))
    return pl.pallas_call(
        functools.partial(_gather_kernel, **route.geom),
        grid_spec=pltpu.PrefetchScalarGridSpec(
            num_scalar_prefetch=2,
            grid=(b,),
            in_specs=[experts_by_tokens, experts_by_tokens, pl.BlockSpec((1, n, d), lambda bi, *_: (bi, 0, 0))],
            out_specs=[pl.BlockSpec((1, N_EXPERTS, cap, d), lambda bi, *_: (bi, 0, 0, 0)),
                       pl.BlockSpec((1, cap, LANES), lambda bi, *_: (bi, 0, 0))],
        ),
        out_shape=[jax.ShapeDtypeStruct((b, N_EXPERTS, cap, d), BF16), jax.ShapeDtypeStruct((b, cap, LANES), F32)],
        compiler_params=_params("arbitrary"),
        name="moe_gather",
    )(route.start, route.fits, route.sel, pt, h2)


def _expert_kernel(vals_ref, xs_ref, wg_ref, wu_ref, wd_ref, y_ref, wgu_s, wd_s):
    e = pl.program_id(0)
    hdim = wd_s.shape[0]

    @pl.when(pl.program_id(1) == 0)
    def _():
        wgu_s[:, :hdim] = wg_ref[0, 0].astype(BF16)
        wgu_s[:, hdim:] = wu_ref[0, 0].astype(BF16)
        wd_s[...] = wd_ref[0, 0].astype(BF16)

    bb, cap, _ = vals_ref.shape
    d = xs_ref.shape[-1]
    lane = lax.broadcasted_iota(jnp.int32, (1, 1, LANES), 2)
    vals = jnp.sum(jnp.where(lane == e, vals_ref[...], 0.0), axis=2, keepdims=True).reshape(bb * cap, 1)
    xs = xs_ref[...].reshape(bb * cap, d)
    ab = _dot(xs, wgu_s[...])
    a, bm = ab[:, :hdim], ab[:, hdim:]
    hid = (a * jax.nn.sigmoid(a) * bm * vals).astype(BF16)
    y_ref[...] = _dot(hid, wd_s[...]).astype(BF16).reshape(y_ref.shape)


def _expert_ffn(vals, xs, w_gate, w_up, w_down, layer):
    b, _, cap, d = xs.shape
    hdim = w_gate.shape[-1]
    bb = next(k for k in (4, 2, 1) if b % k == 0)
    w_spec = lambda rows, cols: pl.BlockSpec((1, 1, rows, cols), lambda e, bi: (layer, e, 0, 0))
    return pl.pallas_call(
        _expert_kernel,
        grid=(N_EXPERTS, b // bb),
        in_specs=[
            pl.BlockSpec((bb, cap, LANES), lambda e, bi: (bi, 0, 0)),
            pl.BlockSpec((bb, 1, cap, d), lambda e, bi: (bi, e, 0, 0)),
            w_spec(d, hdim), w_spec(d, hdim), w_spec(hdim, d),
        ],
        out_specs=pl.BlockSpec((bb, 1, cap, d), lambda e, bi: (bi, e, 0, 0)),
        out_shape=jax.ShapeDtypeStruct((b, N_EXPERTS, cap, d), BF16),
        scratch_shapes=[pltpu.VMEM((d, 2 * hdim), BF16), pltpu.VMEM((hdim, d), BF16)],
        compiler_params=_params("arbitrary", "arbitrary"),
        name="expert_ffn",
    )(vals, xs, w_gate, w_up, w_down)


def _scatter_add(start_ref, fits_ref, sel_ref, y_ref, acc_ref, *, cap, window, sub, n_chunks):
    bi, i = pl.program_id(0), pl.program_id(1)
    tm = sel_ref.shape[-1]
    n_sub = tm // sub
    entry = lambda e, j: (bi * N_EXPERTS + e) * n_chunks + i * n_sub + j

    all_fit = fits_ref[entry(0, 0)]
    for e in range(N_EXPERTS):
        for j in range(n_sub):
            all_fit = jnp.minimum(all_fit, fits_ref[entry(e, j)])

    @pl.when(all_fit == 1)
    def _():
        offs = lax.broadcasted_iota(jnp.int32, (window, sub), 0).astype(F32)
        for j in range(n_sub):
            sel = sel_ref[0, :, j * sub:(j + 1) * sub]
            hots, ys = [], []
            for e in range(N_EXPERTS):
                a = pl.multiple_of(start_ref[entry(e, j)], ALIGN_ROWS)
                hots.append(jnp.where(sel[e:e + 1, :] - a.astype(F32) == offs, 1.0, 0.0).astype(BF16))
                ys.append(y_ref[0, e, pl.ds(a, window), :])
            acc_ref[j * sub:(j + 1) * sub, :] = lax.dot_general(
                jnp.concatenate(hots, axis=0), jnp.concatenate(ys, axis=0), TN_DIMS, preferred_element_type=F32)

    @pl.when(all_fit != 1)
    def _():
        sel = sel_ref[0]
        rank = lax.broadcasted_iota(jnp.int32, (cap, tm), 0).astype(F32)
        acc = jnp.zeros(acc_ref.shape, F32)
        for e in range(N_EXPERTS):
            onehot = jnp.where(sel[e:e + 1, :] == rank, 1.0, 0.0).astype(BF16)
            acc = acc + lax.dot_general(onehot, y_ref[0, e], TN_DIMS, preferred_element_type=F32)
        acc_ref[...] = acc


def _combine_final_kernel(start_ref, fits_ref, sel_ref, y_ref, x_ref, mod_ref, fn_ref, o_ref, acc_ref, **geom):
    _scatter_add(start_ref, fits_ref, sel_ref, y_ref, acc_ref, **geom)
    x2 = x_ref[0] + mod_ref[0][5:6] * acc_ref[...]
    o_ref[0] = x2 * lax.rsqrt(jnp.mean(x2 * x2, axis=-1, keepdims=True) + NORM_EPS) * fn_ref[...]


def _combine_cmlp_kernel(start_ref, fits_ref, sel_ref, y_ref, x_ref, mod0_ref, mod1_ref, n1_ref, w1_ref, vn_ref,
                         ws_ref, bs_ref, wo_ref, n2_ref, wr_ref, x1_ref, h2_ref, pt_ref, acc_ref, mix_ref, **geom):
    _scatter_add(start_ref, fits_ref, sel_ref, y_ref, acc_ref, **geom)
    x2 = x_ref[0] + mod0_ref[0][5:6] * acc_ref[...]
    _cmlp_body(x2, mod1_ref[0], n1_ref, w1_ref, vn_ref, ws_ref, bs_ref, wo_ref, n2_ref, wr_ref,
               x1_ref, h2_ref, pt_ref, mix_ref)


def _combine_specs(b, n, d, cap, tm):
    return [
        pl.BlockSpec((1, N_EXPERTS, tm), lambda bi, i, *_: (bi, 0, i)),
        pl.BlockSpec((1, N_EXPERTS, cap, d), lambda bi, i, *_: (bi, 0, 0, 0)),
        pl.BlockSpec((1, tm, d), lambda bi, i, *_: (bi, i, 0)),
        pl.BlockSpec((1, 6, d), lambda bi, i, *_: (bi, 0, 0)),
    ]


def _combine_final(route, y, x1, mod_l, final_norm, tm):
    b, n, d = x1.shape
    return pl.pallas_call(
        functools.partial(_combine_final_kernel, **route.geom),
        grid_spec=pltpu.PrefetchScalarGridSpec(
            num_scalar_prefetch=2,
            grid=(b, n // tm),
            in_specs=_combine_specs(b, n, d, route.geom["cap"], tm) + [pl.BlockSpec((1, d), lambda bi, i, *_: (0, 0))],
            out_specs=pl.BlockSpec((1, tm, d), lambda bi, i, *_: (bi, i, 0)),
            scratch_shapes=[pltpu.VMEM((tm, d), F32)],
        ),
        out_shape=jax.ShapeDtypeStruct((b, n, d), F32),
        compiler_params=_params("arbitrary", "arbitrary"),
        name="moe_combine_final",
    )(route.start, route.fits, route.sel, y, x1, mod_l, final_norm)


def _combine_chunk_mlp(route, y, x1, mod0, mod1, norm1, w1_bf, v_norm, ws_bf, bs_t, w_out_bf, norm2, wr_cat, tm):
    b, n, d = x1.shape
    const = lambda *shape: pl.BlockSpec(shape, lambda bi, i, *_: (0,) * len(shape))
    tail_in, out_specs, out_shape = _tail_specs(b, n, d, tm)
    return pl.pallas_call(
        functools.partial(_combine_cmlp_kernel, **route.geom),
        grid_spec=pltpu.PrefetchScalarGridSpec(
            num_scalar_prefetch=2,
            grid=(b, n // tm),
            in_specs=_combine_specs(b, n, d, route.geom["cap"], tm) + [
                pl.BlockSpec((1, 6, d), lambda bi, i, *_: (bi, 0, 0)),
                const(1, d), const(d, 2 * d), const(1, d), const(CMLP_GROUPS, CHUNK, CHUNK),
                const(CHUNK, CMLP_GROUPS), const(d, d),
            ] + tail_in,
            out_specs=out_specs,
            scratch_shapes=[pltpu.VMEM((tm, d), F32), pltpu.VMEM((tm, d), BF16)],
        ),
        out_shape=out_shape,
        compiler_params=_params("arbitrary", "arbitrary"),
        name="combine_chunk_mlp",
    )(route.start, route.fits, route.sel, y, x1, mod0, mod1, norm1, w1_bf, v_norm, ws_bf, bs_t, w_out_bf,
      norm2, wr_cat)


class _Routing(NamedTuple):
    sel: jax.Array
    start: jax.Array
    fits: jax.Array
    geom: dict


def _moe_experts(h2, pt, w_gate, w_up, w_down, layer):
    n = h2.shape[1]
    cap = CAPACITY_FACTOR * n // N_EXPERTS
    window = min(SLOT_WINDOW, cap)
    sub = min(SUB_TOKENS, n)
    sel, start, fits = _route(pt, cap, window)
    route = _Routing(sel, start, fits, dict(cap=cap, window=window, sub=sub, n_chunks=n // sub))
    xs, vals = _gather(route, pt, h2)
    return route, _expert_ffn(vals, xs, w_gate, w_up, w_down, layer)


def kernel(x, c, ctx, c_ctx, w_mod, b_mod, norm1, norm2, even_w_in, even_lambda, even_subln, even_conv_w,
           odd_w_in, odd_v_norm, odd_w_s, odd_b_s, w_out, w_router, w_gate, w_up, w_down, final_norm):
    b, n, d = x.shape
    depth = w_mod.shape[0]
    assert d == D_MODEL and depth == 2 and n % 256 == 0
    t_proj = min(1024, n)
    t_attn = min(1024, n)
    t_moe = min(512, n)
    t_final = min(1024, n)

    cc = jnp.concatenate([c, c_ctx[None, :]], axis=0)
    mod = _modulation(cc, w_mod, b_mod).reshape(depth, b + 1, 6, d)
    row = lambda a, l: a[l].reshape(1, -1)
    wr_pad = jnp.pad(w_router, ((0, 0), (0, 0), (0, LANES - N_EXPERTS)))
    wr_hi = wr_pad.astype(BF16)
    wr_lo = (wr_pad - wr_hi.astype(F32)).astype(BF16)
    wr_cat = jnp.concatenate([jnp.concatenate([wr_hi, wr_lo], axis=2),
                              jnp.concatenate([wr_hi, jnp.zeros_like(wr_lo)], axis=2)], axis=1)
    w_out_bf = w_out.astype(BF16)
    fnorm = final_norm.reshape(1, d)

    mod0, modc = mod[0, :b], mod[0, b:]
    w_in_bf = even_w_in[0].astype(BF16)
    q, k, vt, conv = _in_projection(x, mod0, row(norm1, 0), w_in_bf, even_conv_w[0], t_proj)
    kc, vct = _context_kv(ctx, modc, row(norm1, 0), w_in_bf)
    lam_init = 0.8 - 0.6 * math.exp(-0.3 * 0)
    attn = _attention(q, kc, k, vct, vt, even_lambda[0], even_subln[0].reshape(-1, 1), lam_init, t_attn)
    x1, h2, pt = _out_projection(attn, conv, x, mod0, w_out_bf[0], row(norm2, 0), wr_cat[0], t_proj)
    route, y = _moe_experts(h2, pt, w_gate, w_up, w_down, 0)

    mod1 = mod[1, :b]
    x1, h2, pt = _combine_chunk_mlp(route, y, x1, mod0, mod1, row(norm1, 1), odd_w_in[0].astype(BF16),
                                    row(odd_v_norm, 0), odd_w_s[0].astype(BF16), odd_b_s[0].T, w_out_bf[1],
                                    row(norm2, 1), wr_cat[1], t_moe)
    route, y = _moe_experts(h2, pt, w_gate, w_up, w_down, 1)
    return _combine_final(route, y, x1, mod1, fnorm, t_final)
```

```python
import functools
import math
from typing import NamedTuple

import numpy as np
import jax
import jax.numpy as jnp
from jax import lax
from jax.experimental import pallas as pl
from jax.experimental.pallas import tpu as pltpu

F32 = jnp.float32
BF16 = jnp.bfloat16

D_MODEL = 1024
ATTN_HEADS = 4
QK_DIM = 64
V_DIM = 128
Q_W = ATTN_HEADS * 2 * QK_DIM
CONV_W = 512
EVEN_IN = 3072
N_EXPERTS = 16
CAPACITY_FACTOR = 2
GRID_W = 64
CHUNK = 128
CMLP_GROUPS = 4
ROPE_THETA = 10000.0
NORM_EPS = 1e-6
LANES = 128
SUBLANES = 8
HALO_ROWS = SUBLANES
VMEM_LIMIT = 56 * 1024 * 1024
SAFE_SHIFT = 60.0
ALIGN_ROWS = 16
SUB_TOKENS = 256
SLOT_WINDOW = 64

NT_DIMS = (((1,), (1,)), ((), ()))
TN_DIMS = (((0,), (0,)), ((), ()))


def _dot(a, b):
    return jnp.dot(a, b, preferred_element_type=F32)


def _params(*sem):
    return pltpu.CompilerParams(dimension_semantics=sem, vmem_limit_bytes=VMEM_LIMIT)


def _rms_mod(x, g, shift, scale):
    y = x * lax.rsqrt(jnp.mean(x * x, axis=-1, keepdims=True) + NORM_EPS)
    return y * (g * (1.0 + scale)) + shift


def _split_bf16(x):
    hi = x.astype(BF16)
    lo = (x - hi.astype(F32)).astype(BF16)
    return hi, lo


def _mod_kernel(c_ref, w_ref, b_ref, o_ref):
    c = c_ref[...]
    a_hi, a_lo = _split_bf16(c * jax.nn.sigmoid(c))
    w_hi, w_lo = _split_bf16(w_ref[0])
    o_ref[0] = _dot(a_hi, w_hi) + _dot(a_lo, w_hi) + _dot(a_hi, w_lo) + b_ref[0]


def _modulation(cc, w_mod, b_mod):
    depth, d, six_d = w_mod.shape
    rows = cc.shape[0]
    tn = 1536
    return pl.pallas_call(
        _mod_kernel,
        grid=(depth, six_d // tn),
        in_specs=[
            pl.BlockSpec((rows, d), lambda l, j: (0, 0)),
            pl.BlockSpec((1, d, tn), lambda l, j: (l, 0, j)),
            pl.BlockSpec((1, 1, tn), lambda l, j: (l, 0, j)),
        ],
        out_specs=pl.BlockSpec((1, rows, tn), lambda l, j: (l, 0, j)),
        out_shape=jax.ShapeDtypeStruct((depth, rows, six_d), F32),
        compiler_params=_params("arbitrary", "arbitrary"),
        name="modulation",
    )(cc, w_mod, b_mod.reshape(depth, 1, six_d))


def _inproj_kernel(x_ref, xp_ref, xn_ref, mod_ref, n1_ref, w_ref, cos_ref, sa_ref, sb_ref, cw_ref,
                   q_ref, k_ref, vt_ref, c_ref, *, tm, n_tiles):
    i = pl.program_id(1)
    m = mod_ref[0]
    shift, scale = m[0:1], m[1:2]
    g = n1_ref[...]
    h = _rms_mod(x_ref[0], g, shift, scale).astype(BF16)

    cos, sa, sb = cos_ref[...], sa_ref[...], sb_ref[...]
    for col0, out_ref, qscale in ((0, q_ref, QK_DIM ** -0.5 * math.log2(math.e)), (Q_W, k_ref, 1.0)):
        p = _dot(h, w_ref[:, col0:col0 + Q_W])
        for j in range(Q_W // LANES):
            pj = p[:, j * LANES:(j + 1) * LANES]
            r = pj * cos + pltpu.roll(pj, LANES - 16, 1) * sa + pltpu.roll(pj, 16, 1) * sb
            out_ref[0, :, j * LANES:(j + 1) * LANES] = (r * qscale).astype(BF16)

    vt_ref[0] = _dot(h, w_ref[:, 2 * Q_W:2 * Q_W + 512]).T.astype(BF16)

    pc = _dot(h, w_ref[:, 2 * Q_W + 512:])
    gb = pc[:, :CONV_W]
    u = pc[:, CONV_W:2 * CONV_W] * pc[:, 2 * CONV_W:]
    xh = jnp.concatenate([xp_ref[0], xn_ref[0]], axis=0)
    hh = _rms_mod(xh, g, shift, scale).astype(BF16)
    ph = _dot(hh, w_ref[:, 2 * Q_W + 512 + CONV_W:])
    uh = ph[:, :CONV_W] * ph[:, CONV_W:]
    u_before = jnp.where(i > 0, uh[HALO_ROWS - 1:HALO_ROWS], 0.0)
    u_after = jnp.where(i < n_tiles - 1, uh[HALO_ROWS:HALO_ROWS + 1], 0.0)
    row = lax.broadcasted_iota(jnp.int32, (tm, 1), 0)
    u_prev = jnp.where(row == 0, u_before, pltpu.roll(u, 1, 0))
    u_next = jnp.where(row == tm - 1, u_after, pltpu.roll(u, tm - 1, 0))
    cw = cw_ref[...]
    conv = cw[0:1] * u_prev + cw[1:2] * u + cw[2:3] * u_next
    c_ref[0] = (gb * conv).astype(BF16)


def _rope_tables(n):
    rows = n // GRID_W
    row = jnp.repeat(jnp.arange(rows), GRID_W).astype(F32)
    col = jnp.tile(jnp.arange(GRID_W), rows).astype(F32)
    half = QK_DIM // 2
    inv = 1.0 / (ROPE_THETA ** (jnp.arange(0, half, 2, dtype=F32) / half))
    ang_r = row[:, None] * inv
    ang_c = col[:, None] * inv
    ang = jnp.concatenate([ang_r, ang_r, ang_c, ang_c], axis=-1)
    cos, sin = jnp.cos(ang), jnp.sin(ang)
    first_half = (jnp.arange(QK_DIM) % 32) < 16
    sin_a = jnp.where(first_half, -sin, 0.0)
    sin_b = jnp.where(first_half, 0.0, sin)
    tile2 = lambda t: jnp.concatenate([t, t], axis=-1)
    return tile2(cos), tile2(sin_a), tile2(sin_b)


def _in_projection(x, mod0, norm1, w_in_bf, conv_w, tm):
    b, n, d = x.shape
    n_tiles = n // tm
    hb = tm // HALO_ROWS
    n_hblocks = n // HALO_ROWS
    cos, sa, sb = _rope_tables(n)
    out = jax.ShapeDtypeStruct((b, n, Q_W), BF16)
    row_spec = pl.BlockSpec((1, tm, Q_W), lambda bi, i: (bi, i, 0))
    tab_spec = pl.BlockSpec((tm, LANES), lambda bi, i: (i, 0))
    return pl.pallas_call(
        functools.partial(_inproj_kernel, tm=tm, n_tiles=n_tiles),
        grid=(b, n_tiles),
        in_specs=[
            pl.BlockSpec((1, tm, d), lambda bi, i: (bi, i, 0)),
            pl.BlockSpec((1, HALO_ROWS, d), lambda bi, i: (bi, jnp.maximum(i * hb - 1, 0), 0)),
            pl.BlockSpec((1, HALO_ROWS, d), lambda bi, i: (bi, jnp.minimum((i + 1) * hb, n_hblocks - 1), 0)),
            pl.BlockSpec((1, 6, d), lambda bi, i: (bi, 0, 0)),
            pl.BlockSpec((1, d), lambda bi, i: (0, 0)),
            pl.BlockSpec((d, EVEN_IN), lambda bi, i: (0, 0)),
            tab_spec, tab_spec, tab_spec,
            pl.BlockSpec((3, CONV_W), lambda bi, i: (0, 0)),
        ],
        out_specs=[row_spec, row_spec, pl.BlockSpec((1, Q_W, tm), lambda bi, i: (bi, 0, i)), row_spec],
        out_shape=[out, out, jax.ShapeDtypeStruct((b, Q_W, n), BF16), out],
        compiler_params=_params("arbitrary", "arbitrary"),
        name="in_projection",
    )(x, x, x, mod0, norm1, w_in_bf, cos, sa, sb, conv_w)


def _ctxkv_kernel(x_ref, mod_ref, n1_ref, wk_ref, wv_ref, k_ref, vt_ref):
    bb, m_ctx, d = x_ref.shape
    m = mod_ref[0]
    h = _rms_mod(x_ref[...].reshape(bb * m_ctx, d), n1_ref[...], m[0:1], m[1:2]).astype(BF16)
    k_ref[...] = _dot(h, wk_ref[...]).astype(BF16).reshape(k_ref.shape)
    v = _dot(h, wv_ref[...])
    for j in range(bb):
        vt_ref[j] = v[j * m_ctx:(j + 1) * m_ctx].T.astype(BF16)


def _context_kv(ctx, modc, norm1, w_in_bf):
    b, m, d = ctx.shape
    bb = next(k for k in (4, 2, 1) if b % k == 0)
    out = jax.ShapeDtypeStruct((b, m, Q_W), BF16)
    return pl.pallas_call(
        _ctxkv_kernel,
        grid=(b // bb,),
        in_specs=[
            pl.BlockSpec((bb, m, d), lambda bi: (bi, 0, 0)),
            pl.BlockSpec((1, 6, d), lambda bi: (0, 0, 0)),
            pl.BlockSpec((1, d), lambda bi: (0, 0)),
            pl.BlockSpec((d, Q_W), lambda bi: (0, 1)),
            pl.BlockSpec((d, Q_W), lambda bi: (0, 2)),
        ],
        out_specs=[pl.BlockSpec((bb, m, Q_W), lambda bi: (bi, 0, 0)), pl.BlockSpec((bb, Q_W, m), lambda bi: (bi, 0, 0))],
        out_shape=[out, jax.ShapeDtypeStruct((b, Q_W, m), BF16)],
        compiler_params=_params("arbitrary"),
        name="context_kv",
    )(ctx, modc, norm1, w_in_bf, w_in_bf)


def _attn_kernel(lam_ref, q_ref, qall_ref, kc_ref, k_ref, vct_ref, vt_ref, sub_ref, o_ref,
                 kx_ref, vxt_ref, shift_ref, safe_ref, s_ref, e_ref, ox_ref, *, lam_init, m_ctx):
    tq = q_ref.shape[1]

    @pl.when(pl.program_id(2) == 0)
    def _():
        kx_ref[:m_ctx, :] = kc_ref[0]
        kx_ref[m_ctx:, :] = k_ref[0]
        vxt_ref[:, :m_ctx] = vct_ref[0]
        vxt_ref[:, m_ctx:] = vt_ref[0]

        dim = lax.broadcasted_iota(jnp.int32, (LANES, LANES), 0)
        col = lax.broadcasted_iota(jnp.int32, (LANES, LANES), 1)
        ind = jnp.where(col == jnp.where(dim < QK_DIM, 0, 1), 1.0, 0.0).astype(BF16)

        def sq_norms(x):
            xf = x.astype(F32)
            return _dot((xf * xf).astype(BF16), ind)

        kmax = jnp.max(sq_norms(kx_ref[...]), axis=0, keepdims=True)
        qn = sq_norms(qall_ref[0]).T
        bound = jnp.sqrt(qn[0:SUBLANES] * jnp.concatenate(
            [kmax[:, 0:1], kmax[:, 1:2], jnp.zeros((SUBLANES - 2, 1), F32)], axis=0)) * 1.05
        shift_ref[...] = bound
        safe_ref[0] = (jnp.max(bound) <= SAFE_SHIFT).astype(jnp.int32)

    lp = lam_ref[...]
    lam = (jnp.exp(jnp.sum(lp[0:1] * lp[1:2], keepdims=True))
           - jnp.exp(jnp.sum(lp[2:3] * lp[3:4], keepdims=True)) + lam_init)
    q = q_ref[0]
    lane = lax.broadcasted_iota(jnp.int32, q.shape, 1)
    zero = jnp.zeros_like(q)
    qms = [jnp.where(in_map, q, zero) for in_map in (lane < QK_DIM, lane >= QK_DIM)]
    q0 = pl.multiple_of(pl.program_id(2) * tq, tq)
    shifts = [shift_ref[mi:mi + 1, pl.ds(q0, tq)] for mi in range(2)]
    bound_is_safe = safe_ref[0] == 1

    def weighted_values(mi, e32):
        e_ref[mi] = e32.astype(BF16)
        ox_ref[mi, :V_DIM, :] = _dot(vxt_ref[...], e_ref[mi])
        ox_ref[mi, V_DIM:, :] = jnp.broadcast_to(jnp.sum(e32, axis=0, keepdims=True), (SUBLANES, tq))

    @pl.when(bound_is_safe)
    def _():
        for mi in range(2):
            st = lax.dot_general(kx_ref[...], qms[mi], NT_DIMS, preferred_element_type=F32)
            weighted_values(mi, jnp.exp2(st - shifts[mi]))

    @pl.when(jnp.logical_not(bound_is_safe))
    def _():
        for mi in range(2):
            s_ref[mi] = lax.dot_general(kx_ref[...], qms[mi], NT_DIMS, preferred_element_type=F32)
        for mi in range(2):
            weighted_values(mi, jnp.exp2(s_ref[mi] - jnp.max(s_ref[mi], axis=0, keepdims=True)))

    o1, l1 = ox_ref[0, :V_DIM, :], ox_ref[0, V_DIM:V_DIM + 1, :]
    o2, l2 = ox_ref[1, :V_DIM, :], ox_ref[1, V_DIM:V_DIM + 1, :]
    ot = o1 * (1.0 / l1) - o2 * (lam / l2)
    ot = ot * lax.rsqrt(jnp.mean(ot * ot, axis=0, keepdims=True) + NORM_EPS)
    o_ref[0] = (ot * sub_ref[...] * (1.0 - lam_init)).T.astype(BF16)


def _attention(q, kc, k, vct, vt, lam_p, subln, lam_init, tq):
    b, n, _ = q.shape
    m_ctx = kc.shape[1]
    head_rows = lambda rows: pl.BlockSpec((1, rows, V_DIM), lambda bi, h, i: (bi, 0, h))
    head_cols = lambda cols: pl.BlockSpec((1, V_DIM, cols), lambda bi, h, i: (bi, h, 0))
    return pl.pallas_call(
        functools.partial(_attn_kernel, lam_init=lam_init, m_ctx=m_ctx),
        grid=(b, ATTN_HEADS, n // tq),
        in_specs=[
            pl.BlockSpec((4, QK_DIM), lambda bi, h, i: (0, 0)),
            pl.BlockSpec((1, tq, V_DIM), lambda bi, h, i: (bi, i, h)),
            head_rows(n), head_rows(m_ctx), head_rows(n), head_cols(m_ctx), head_cols(n),
            pl.BlockSpec((V_DIM, 1), lambda bi, h, i: (0, 0)),
        ],
        out_specs=pl.BlockSpec((1, tq, V_DIM), lambda bi, h, i: (bi, i, h)),
        out_shape=jax.ShapeDtypeStruct((b, n, Q_W), BF16),
        scratch_shapes=[pltpu.VMEM((m_ctx + n, V_DIM), BF16), pltpu.VMEM((V_DIM, m_ctx + n), BF16),
                        pltpu.VMEM((SUBLANES, n), F32), pltpu.SMEM((1,), jnp.int32),
                        pltpu.VMEM((2, m_ctx + n, tq), F32), pltpu.VMEM((2, m_ctx + n, tq), BF16),
                        pltpu.VMEM((2, V_DIM + SUBLANES, tq), F32)],
        compiler_params=_params("arbitrary", "arbitrary", "arbitrary"),
        name="diff_attention",
    )(lam_p, q, q, kc, k, vct, vt, subln)


def _tail(y, x, m, n2, wr_ref, x1_ref, h2_ref, pt_ref):
    x1 = x + m[2:3] * y
    x1_ref[0] = x1
    h2 = _rms_mod(x1, n2, m[3:4], m[4:5])
    h_hi, h_lo = _split_bf16(h2)
    h2_ref[0] = h_hi
    hcat = jnp.concatenate([h_hi, h_lo], axis=1)
    half = hcat.shape[0] // 2
    parts = [_dot(hcat[r0:r0 + half], wr_ref[...]) for r0 in (0, half)]
    prod = jnp.concatenate(parts, axis=0)
    logits = (prod[:, :LANES] + prod[:, LANES:]).T[:N_EXPERTS]
    ex = jnp.exp(logits - jnp.max(logits, axis=0, keepdims=True))
    pt_ref[0] = ex / jnp.sum(ex, axis=0, keepdims=True)


def _tail_specs(b, n, d, tm):
    in_specs = [
        pl.BlockSpec((1, d), lambda bi, i, *_: (0, 0)),
        pl.BlockSpec((2 * d, 2 * LANES), lambda bi, i, *_: (0, 0)),
    ]
    out_specs = [
        pl.BlockSpec((1, tm, d), lambda bi, i, *_: (bi, i, 0)),
        pl.BlockSpec((1, tm, d), lambda bi, i, *_: (bi, i, 0)),
        pl.BlockSpec((1, N_EXPERTS, tm), lambda bi, i, *_: (bi, 0, i)),
    ]
    out_shape = [
        jax.ShapeDtypeStruct((b, n, d), F32),
        jax.ShapeDtypeStruct((b, n, d), BF16),
        jax.ShapeDtypeStruct((b, N_EXPERTS, n), F32),
    ]
    return in_specs, out_specs, out_shape


def _outproj_kernel(a_ref, c_ref, x_ref, mod_ref, wo_ref, n2_ref, wr_ref,
                    x1_ref, h2_ref, pt_ref, mix_ref):
    mix_ref[:, :Q_W] = a_ref[0]
    mix_ref[:, Q_W:] = c_ref[0]
    y = _dot(mix_ref[...], wo_ref[...])
    _tail(y, x_ref[0], mod_ref[0], n2_ref[...], wr_ref, x1_ref, h2_ref, pt_ref)


def _out_projection(attn, conv, x, mod0, w_out_bf, norm2, wr_cat, tm):
    b, n, d = x.shape
    tail_in, out_specs, out_shape = _tail_specs(b, n, d, tm)
    return pl.pallas_call(
        _outproj_kernel,
        grid=(b, n // tm),
        in_specs=[
            pl.BlockSpec((1, tm, Q_W), lambda bi, i: (bi, i, 0)),
            pl.BlockSpec((1, tm, CONV_W), lambda bi, i: (bi, i, 0)),
            pl.BlockSpec((1, tm, d), lambda bi, i: (bi, i, 0)),
            pl.BlockSpec((1, 6, d), lambda bi, i: (bi, 0, 0)),
            pl.BlockSpec((d, d), lambda bi, i: (0, 0)),
        ] + tail_in,
        out_specs=out_specs,
        out_shape=out_shape,
        scratch_shapes=[pltpu.VMEM((tm, d), BF16)],
        compiler_params=_params("arbitrary", "arbitrary"),
        name="out_projection",
    )(attn, conv, x, mod0, w_out_bf, norm2, wr_cat)


def _gelu(x):
    return 0.5 * x * (1.0 + lax.erf(x * np.float32(1.0 / math.sqrt(2.0))))


def _cmlp_body(x, m, n1_ref, w1_ref, vn_ref, ws_ref, bs_ref, wo_ref, n2_ref, wr_ref,
               x1_ref, h2_ref, pt_ref, mix_ref):
    tm, d = x.shape
    h = _rms_mod(x, n1_ref[...], m[0:1], m[1:2]).astype(BF16)
    p = _gelu(_dot(h, w1_ref[...]))
    u, v = p[:, :d], p[:, d:]
    v = v * lax.rsqrt(jnp.mean(v * v, axis=-1, keepdims=True) + NORM_EPS) * vn_ref[...]
    vb = v.astype(BF16)
    gw = d // CMLP_GROUPS
    bs = bs_ref[...]
    for c in range(tm // CHUNK):
        r0 = c * CHUNK
        for g in range(CMLP_GROUPS):
            s = _dot(ws_ref[g], vb[r0:r0 + CHUNK, g * gw:(g + 1) * gw]) + bs[:, g:g + 1]
            mix_ref[r0:r0 + CHUNK, g * gw:(g + 1) * gw] = (u[r0:r0 + CHUNK, g * gw:(g + 1) * gw] * s).astype(BF16)
    y = _dot(mix_ref[...], wo_ref[...])
    _tail(y, x, m, n2_ref[...], wr_ref, x1_ref, h2_ref, pt_ref)


def _route_kernel(p_ref, sel_ref, start_ref, fits_ref, *, cap, pchunk, window):
    rows, n = p_ref.shape

    def bit_step(it, t):
        cand = t | jnp.left_shift(jnp.int32(1), 30 - it)
        cf = lax.bitcast_convert_type(cand, F32)
        cnt = jnp.sum(jnp.where(p_ref[...] >= cf, 1.0, 0.0), axis=1, keepdims=True)
        return jnp.where(cnt >= cap, cand, t)

    t = lax.fori_loop(0, 31, bit_step, jnp.zeros((rows, 1), jnp.int32))
    tf = lax.bitcast_convert_type(t, F32)
    p = p_ref[...]
    gt = p > tf
    eq = jnp.logical_and(p >= tf, jnp.logical_not(gt))
    need = cap - jnp.sum(jnp.where(gt, 1.0, 0.0), axis=1, keepdims=True)

    ri = lax.broadcasted_iota(jnp.int32, (pchunk, pchunk), 0)
    ci = lax.broadcasted_iota(jnp.int32, (pchunk, pchunk), 1)
    upper = jnp.where(ri < ci, 1.0, 0.0).astype(BF16)

    def prefix(mask):
        mf = jnp.where(mask, 1.0, 0.0)
        carry = jnp.zeros((rows, 1), F32)
        parts, before, inside = [], [], []
        for c in range(n // pchunk):
            blk = mf[:, c * pchunk:(c + 1) * pchunk]
            parts.append(_dot(blk.astype(BF16), upper) + carry)
            total = jnp.sum(blk, axis=1, keepdims=True)
            before.append(carry)
            inside.append(total)
            carry = carry + total
        return jnp.concatenate(parts, axis=1), jnp.concatenate(before, axis=1), jnp.concatenate(inside, axis=1)

    chosen = jnp.logical_or(gt, jnp.logical_and(eq, prefix(eq)[0] < need))
    rank, before, inside = prefix(chosen)
    sel_ref[...] = jnp.where(chosen, rank, -1.0)
    start = jnp.minimum(jnp.floor(before * (1.0 / ALIGN_ROWS)) * ALIGN_ROWS, float(cap - window))
    start_ref[...] = start.astype(jnp.int32)
    fits_ref[...] = jnp.where(before + inside <= start + window, 1, 0).astype(jnp.int32)


def _route(pt, cap, window):
    b, e, n = pt.shape
    rows = b * e
    pchunk = min(SUB_TOKENS, n)
    n_chunks = n // pchunk
    full = lambda cols: pl.BlockSpec((rows, cols), lambda i: (0, 0))
    sel, start, fits = pl.pallas_call(
        functools.partial(_route_kernel, cap=cap, pchunk=pchunk, window=window),
        grid=(1,),
        in_specs=[full(n)],
        out_specs=[full(n), full(n_chunks), full(n_chunks)],
        out_shape=[jax.ShapeDtypeStruct((rows, n), F32), jax.ShapeDtypeStruct((rows, n_chunks), jnp.int32),
                   jax.ShapeDtypeStruct((rows, n_chunks), jnp.int32)],
        compiler_params=_params("arbitrary"),
        name="route",
    )(pt.reshape(rows, n))
    return sel.reshape(b, e, n), start.reshape(-1), fits.reshape(-1)


def _gather_kernel(start_ref, fits_ref, sel_ref, pt_ref, h_ref, xs_ref, vals_ref, *, cap, window, sub, n_chunks):
    bi = pl.program_id(0)
    entry = lambda e, c: (bi * N_EXPERTS + e) * n_chunks + c
    lane = lax.broadcasted_iota(jnp.int32, (1, LANES), 1)

    all_fit = fits_ref[entry(0, 0)]
    for e in range(N_EXPERTS):
        for c in range(n_chunks):
            all_fit = jnp.minimum(all_fit, fits_ref[entry(e, c)])

    vals_ref[...] = jnp.zeros(vals_ref.shape, F32)

    @pl.when(all_fit == 1)
    def _():
        xs_ref[...] = jnp.zeros(xs_ref.shape, BF16)
        offs = lax.broadcasted_iota(jnp.int32, (window, sub), 0).astype(F32)
        for c in range(n_chunks):
            sel = sel_ref[0, :, c * sub:(c + 1) * sub]
            prob = pt_ref[0, :, c * sub:(c + 1) * sub]
            hc = h_ref[0, c * sub:(c + 1) * sub, :]
            starts = [pl.multiple_of(start_ref[entry(e, c)], ALIGN_ROWS) for e in range(N_EXPERTS)]
            hits = [sel[e:e + 1, :] - a.astype(F32) == offs for e, a in enumerate(starts)]
            hots = [jnp.where(hit, 1.0, 0.0).astype(BF16) for hit in hits]
            z = _dot(jnp.concatenate(hots, axis=0), hc).astype(BF16)
            for e, a in enumerate(starts):
                xs_ref[0, e, pl.ds(a, window), :] += z[e * window:(e + 1) * window]
                w = jnp.sum(jnp.where(hits[e], prob[e:e + 1, :], 0.0), axis=1, keepdims=True)
                vals_ref[0, pl.ds(a, window), :] += jnp.where(lane == e, w, 0.0)

    @pl.when(all_fit != 1)
    def _():
        n = sel_ref.shape[-1]
        rank = lax.broadcasted_iota(jnp.int32, (cap, n), 0).astype(F32)
        for e in range(N_EXPERTS):
            hit = sel_ref[0, e:e + 1, :] == rank
            xs_ref[0, e] = _dot(jnp.where(hit, 1.0, 0.0).astype(BF16), h_ref[0]).astype(BF16)
            w = jnp.sum(jnp.where(hit, pt_ref[0, e:e + 1, :], 0.0), axis=1, keepdims=True)
            vals_ref[0] += jnp.where(lane == e, w, 0.0)


def _gather(route, pt, h2):
    b, n, d = h2.shape
    cap = route.geom["cap"]
    experts_by_tokens = pl.BlockSpec((1, N_EXPERTS, n), lambda bi, *_: (bi, 0, 0))
    return pl.pallas_call(
        functools.partial(_gather_kernel, **route.geom),
        grid_spec=pltpu.PrefetchScalarGridSpec(
            num_scalar_prefetch=2,
            grid=(b,),
            in_specs=[experts_by_tokens, experts_by_tokens, pl.BlockSpec((1, n, d), lambda bi, *_: (bi, 0, 0))],
            out_specs=[pl.BlockSpec((1, N_EXPERTS, cap, d), lambda bi, *_: (bi, 0, 0, 0)),
                       pl.BlockSpec((1, cap, LANES), lambda bi, *_: (bi, 0, 0))],
        ),
        out_shape=[jax.ShapeDtypeStruct((b, N_EXPERTS, cap, d), BF16), jax.ShapeDtypeStruct((b, cap, LANES), F32)],
        compiler_params=_params("arbitrary"),
        name="moe_gather",
    )(route.start, route.fits, route.sel, pt, h2)


def _expert_kernel(vals_ref, xs_ref, wg_ref, wu_ref, wd_ref, y_ref, wgu_s, wd_s):
    e = pl.program_id(0)
    hdim = wd_s.shape[0]

    @pl.when(pl.program_id(1) == 0)
    def _():
        wgu_s[:, :hdim] = wg_ref[0, 0].astype(BF16)
        wgu_s[:, hdim:] = wu_ref[0, 0].astype(BF16)
        wd_s[...] = wd_ref[0, 0].astype(BF16)

    bb, cap, _ = vals_ref.shape
    d = xs_ref.shape[-1]
    lane = lax.broadcasted_iota(jnp.int32, (1, 1, LANES), 2)
    vals = jnp.sum(jnp.where(lane == e, vals_ref[...], 0.0), axis=2, keepdims=True).reshape(bb * cap, 1)
    xs = xs_ref[...].reshape(bb * cap, d)
    ab = _dot(xs, wgu_s[...])
    a, bm = ab[:, :hdim], ab[:, hdim:]
    hid = (a * jax.nn.sigmoid(a) * bm * vals).astype(BF16)
    y_ref[...] = _dot(hid, wd_s[...]).astype(BF16).reshape(y_ref.shape)


def _expert_ffn(vals, xs, w_gate, w_up, w_down, layer):
    b, _, cap, d = xs.shape
    hdim = w_gate.shape[-1]
    bb = next(k for k in (4, 2, 1) if b % k == 0)
    w_spec = lambda rows, cols: pl.BlockSpec((1, 1, rows, cols), lambda e, bi: (layer, e, 0, 0))
    return pl.pallas_call(
        _expert_kernel,
        grid=(N_EXPERTS, b // bb),
        in_specs=[
            pl.BlockSpec((bb, cap, LANES), lambda e, bi: (bi, 0, 0)),
            pl.BlockSpec((bb, 1, cap, d), lambda e, bi: (bi, e, 0, 0)),
            w_spec(d, hdim), w_spec(d, hdim), w_spec(hdim, d),
        ],
        out_specs=pl.BlockSpec((bb, 1, cap, d), lambda e, bi: (bi, e, 0, 0)),
        out_shape=jax.ShapeDtypeStruct((b, N_EXPERTS, cap, d), BF16),
        scratch_shapes=[pltpu.VMEM((d, 2 * hdim), BF16), pltpu.VMEM((hdim, d), BF16)],
        compiler_params=_params("arbitrary", "arbitrary"),
        name="expert_ffn",
    )(vals, xs, w_gate, w_up, w_down)


def _scatter_add(start_ref, fits_ref, sel_ref, y_ref, acc_ref, *, cap, window, sub, n_chunks, y_slot=0):
    bi, i = pl.program_id(0), pl.program_id(1)
    tm = sel_ref.shape[-1]
    n_sub = tm // sub
    entry = lambda e, j: (bi * N_EXPERTS + e) * n_chunks + i * n_sub + j

    all_fit = fits_ref[entry(0, 0)]
    for e in range(N_EXPERTS):
        for j in range(n_sub):
            all_fit = jnp.minimum(all_fit, fits_ref[entry(e, j)])

    @pl.when(all_fit == 1)
    def _():
        offs = lax.broadcasted_iota(jnp.int32, (window, sub), 0).astype(F32)
        for j in range(n_sub):
            sel = sel_ref[0, :, j * sub:(j + 1) * sub]
            hots, ys = [], []
            for e in range(N_EXPERTS):
                a = pl.multiple_of(start_ref[entry(e, j)], ALIGN_ROWS)
                hots.append(jnp.where(sel[e:e + 1, :] - a.astype(F32) == offs, 1.0, 0.0).astype(BF16))
                ys.append(y_ref[y_slot, e, pl.ds(a, window), :])
            acc_ref[j * sub:(j + 1) * sub, :] = lax.dot_general(
                jnp.concatenate(hots, axis=0), jnp.concatenate(ys, axis=0), TN_DIMS, preferred_element_type=F32)

    @pl.when(all_fit != 1)
    def _():
        sel = sel_ref[0]
        rank = lax.broadcasted_iota(jnp.int32, (cap, tm), 0).astype(F32)
        acc = jnp.zeros(acc_ref.shape, F32)
        for e in range(N_EXPERTS):
            onehot = jnp.where(sel[e:e + 1, :] == rank, 1.0, 0.0).astype(BF16)
            acc = acc + lax.dot_general(onehot, y_ref[y_slot, e], TN_DIMS, preferred_element_type=F32)
        acc_ref[...] = acc


def _combine_final_kernel(start_ref, fits_ref, sel_ref, y_hbm, x_ref, mod_ref, fn_ref, o_ref, acc_ref, ybuf, sem,
                          **geom):
    bi, i = pl.program_id(0), pl.program_id(1)
    n_b, n_i = pl.num_programs(0), pl.num_programs(1)
    piece = N_EXPERTS // n_i
    slot = lax.rem(bi, 2)

    def fetch(batch, to_slot, p):
        return pltpu.make_async_copy(y_hbm.at[batch, pl.ds(p * piece, piece)],
                                     ybuf.at[to_slot, pl.ds(p * piece, piece)], sem.at[to_slot, p])

    @pl.when(jnp.logical_and(bi == 0, i == 0))
    def _():
        for p in range(n_i):
            fetch(0, 0, p).start()

    @pl.when(i == 0)
    def _():
        for p in range(n_i):
            fetch(bi, slot, p).wait()

    @pl.when(bi + 1 < n_b)
    def _():
        for p in range(n_i):
            @pl.when(i == p)
            def _():
                fetch(bi + 1, 1 - slot, p).start()

    _scatter_add(start_ref, fits_ref, sel_ref, ybuf, acc_ref, y_slot=slot, **geom)
    x2 = x_ref[0] + mod_ref[0][5:6] * acc_ref[...]
    o_ref[0] = x2 * lax.rsqrt(jnp.mean(x2 * x2, axis=-1, keepdims=True) + NORM_EPS) * fn_ref[...]


def _combine_cmlp_kernel(start_ref, fits_ref, sel_ref, y_ref, x_ref, mod0_ref, mod1_ref, n1_ref, w1_ref, vn_ref,
                         ws_ref, bs_ref, wo_ref, n2_ref, wr_ref, x1_ref, h2_ref, pt_ref, acc_ref, mix_ref, **geom):
    _scatter_add(start_ref, fits_ref, sel_ref, y_ref, acc_ref, **geom)
    x2 = x_ref[0] + mod0_ref[0][5:6] * acc_ref[...]
    _cmlp_body(x2, mod1_ref[0], n1_ref, w1_ref, vn_ref, ws_ref, bs_ref, wo_ref, n2_ref, wr_ref,
               x1_ref, h2_ref, pt_ref, mix_ref)


def _combine_specs(b, n, d, cap, tm):
    return [
        pl.BlockSpec((1, N_EXPERTS, tm), lambda bi, i, *_: (bi, 0, i)),
        pl.BlockSpec((1, N_EXPERTS, cap, d), lambda bi, i, *_: (bi, 0, 0, 0)),
        pl.BlockSpec((1, tm, d), lambda bi, i, *_: (bi, i, 0)),
        pl.BlockSpec((1, 6, d), lambda bi, i, *_: (bi, 0, 0)),
    ]


def _combine_final(route, y, x1, mod_l, final_norm, tm):
    b, n, d = x1.shape
    cap = route.geom["cap"]
    n_tiles = n // tm
    assert N_EXPERTS % n_tiles == 0
    specs = _combine_specs(b, n, d, cap, tm)
    specs[1] = pl.BlockSpec(memory_space=pl.ANY)
    return pl.pallas_call(
        functools.partial(_combine_final_kernel, **route.geom),
        grid_spec=pltpu.PrefetchScalarGridSpec(
            num_scalar_prefetch=2,
            grid=(b, n_tiles),
            in_specs=specs + [pl.BlockSpec((1, d), lambda bi, i, *_: (0, 0))],
            out_specs=pl.BlockSpec((1, tm, d), lambda bi, i, *_: (bi, i, 0)),
            scratch_shapes=[pltpu.VMEM((tm, d), F32), pltpu.VMEM((2, N_EXPERTS, cap, d), BF16),
                            pltpu.SemaphoreType.DMA((2, n_tiles))],
        ),
        out_shape=jax.ShapeDtypeStruct((b, n, d), F32),
        compiler_params=_params("arbitrary", "arbitrary"),
        name="moe_combine_final",
    )(route.start, route.fits, route.sel, y, x1, mod_l, final_norm)


def _combine_chunk_mlp(route, y, x1, mod0, mod1, norm1, w1_bf, v_norm, ws_bf, bs_t, w_out_bf, norm2, wr_cat, tm):
    b, n, d = x1.shape
    const = lambda *shape: pl.BlockSpec(shape, lambda bi, i, *_: (0,) * len(shape))
    tail_in, out_specs, out_shape = _tail_specs(b, n, d, tm)
    return pl.pallas_call(
        functools.partial(_combine_cmlp_kernel, **route.geom),
        grid_spec=pltpu.PrefetchScalarGridSpec(
            num_scalar_prefetch=2,
            grid=(b, n // tm),
            in_specs=_combine_specs(b, n, d, route.geom["cap"], tm) + [
                pl.BlockSpec((1, 6, d), lambda bi, i, *_: (bi, 0, 0)),
                const(1, d), const(d, 2 * d), const(1, d), const(CMLP_GROUPS, CHUNK, CHUNK),
                const(CHUNK, CMLP_GROUPS), const(d, d),
            ] + tail_in,
            out_specs=out_specs,
            scratch_shapes=[pltpu.VMEM((tm, d), F32), pltpu.VMEM((tm, d), BF16)],
        ),
        out_shape=out_shape,
        compiler_params=_params("arbitrary", "arbitrary"),
        name="combine_chunk_mlp",
    )(route.start, route.fits, route.sel, y, x1, mod0, mod1, norm1, w1_bf, v_norm, ws_bf, bs_t, w_out_bf,
      norm2, wr_cat)


class _Routing(NamedTuple):
    sel: jax.Array
    start: jax.Array
    fits: jax.Array
    geom: dict


def _moe_experts(h2, pt, w_gate, w_up, w_down, layer):
    n = h2.shape[1]
    cap = CAPACITY_FACTOR * n // N_EXPERTS
    window = min(SLOT_WINDOW, cap)
    sub = min(SUB_TOKENS, n)
    sel, start, fits = _route(pt, cap, window)
    route = _Routing(sel, start, fits, dict(cap=cap, window=window, sub=sub, n_chunks=n // sub))
    xs, vals = _gather(route, pt, h2)
    return route, _expert_ffn(vals, xs, w_gate, w_up, w_down, layer)


def kernel(x, c, ctx, c_ctx, w_mod, b_mod, norm1, norm2, even_w_in, even_lambda, even_subln, even_conv_w,
           odd_w_in, odd_v_norm, odd_w_s, odd_b_s, w_out, w_router, w_gate, w_up, w_down, final_norm):
    b, n, d = x.shape
    depth = w_mod.shape[0]
    assert d == D_MODEL and depth == 2 and n % 256 == 0
    t_proj = min(1024, n)
    t_attn = min(1024, n)
    t_moe = min(512, n)
    t_final = min(1024, n)

    cc = jnp.concatenate([c, c_ctx[None, :]], axis=0)
    mod = _modulation(cc, w_mod, b_mod).reshape(depth, b + 1, 6, d)
    row = lambda a, l: a[l].reshape(1, -1)
    wr_pad = jnp.pad(w_router, ((0, 0), (0, 0), (0, LANES - N_EXPERTS)))
    wr_hi = wr_pad.astype(BF16)
    wr_lo = (wr_pad - wr_hi.astype(F32)).astype(BF16)
    wr_cat = jnp.concatenate([jnp.concatenate([wr_hi, wr_lo], axis=2),
                              jnp.concatenate([wr_hi, jnp.zeros_like(wr_lo)], axis=2)], axis=1)
    w_out_bf = w_out.astype(BF16)
    fnorm = final_norm.reshape(1, d)

    mod0, modc = mod[0, :b], mod[0, b:]
    w_in_bf = even_w_in[0].astype(BF16)
    q, k, vt, conv = _in_projection(x, mod0, row(norm1, 0), w_in_bf, even_conv_w[0], t_proj)
    kc, vct = _context_kv(ctx, modc, row(norm1, 0), w_in_bf)
    lam_init = 0.8 - 0.6 * math.exp(-0.3 * 0)
    attn = _attention(q, kc, k, vct, vt, even_lambda[0], even_subln[0].reshape(-1, 1), lam_init, t_attn)
    x1, h2, pt = _out_projection(attn, conv, x, mod0, w_out_bf[0], row(norm2, 0), wr_cat[0], t_proj)
    route, y = _moe_experts(h2, pt, w_gate, w_up, w_down, 0)

    mod1 = mod[1, :b]
    x1, h2, pt = _combine_chunk_mlp(route, y, x1, mod0, mod1, row(norm1, 1), odd_w_in[0].astype(BF16),
                                    row(odd_v_norm, 0), odd_w_s[0].astype(BF16), odd_b_s[0].T, w_out_bf[1],
                                    row(norm2, 1), wr_cat[1], t_moe)
    route, y = _moe_experts(h2, pt, w_gate, w_up, w_down, 1)
    return _combine_final(route, y, x1, mod1, fnorm, t_final)
```

```python
import functools
import math
from typing import NamedTuple

import numpy as np
import jax
import jax.numpy as jnp
from jax import lax
from jax.experimental import pallas as pl
from jax.experimental.pallas import tpu as pltpu

F32 = jnp.float32
BF16 = jnp.bfloat16

D_MODEL = 1024
ATTN_HEADS = 4
QK_DIM = 64
V_DIM = 128
Q_W = ATTN_HEADS * 2 * QK_DIM
CONV_W = 512
EVEN_IN = 3072
N_EXPERTS = 16
CAPACITY_FACTOR = 2
GRID_W = 64
CHUNK = 128
CMLP_GROUPS = 4
ROPE_THETA = 10000.0
NORM_EPS = 1e-6
LANES = 128
SUBLANES = 8
HALO_ROWS = SUBLANES
VMEM_LIMIT = 56 * 1024 * 1024
SAFE_SHIFT = 60.0
ALIGN_ROWS = 16
SUB_TOKENS = 256
SLOT_WINDOW = 64

NT_DIMS = (((1,), (1,)), ((), ()))
TN_DIMS = (((0,), (0,)), ((), ()))


def _dot(a, b):
    return jnp.dot(a, b, preferred_element_type=F32)


def _params(*sem):
    return pltpu.CompilerParams(dimension_semantics=sem, vmem_limit_bytes=VMEM_LIMIT)


def _rms_mod(x, g, shift, scale):
    y = x * lax.rsqrt(jnp.mean(x * x, axis=-1, keepdims=True) + NORM_EPS)
    return y * (g * (1.0 + scale)) + shift


def _split_bf16(x):
    hi = x.astype(BF16)
    lo = (x - hi.astype(F32)).astype(BF16)
    return hi, lo


def _mod_kernel(c_ref, w_ref, b_ref, o_ref):
    c = c_ref[...]
    a_hi, a_lo = _split_bf16(c * jax.nn.sigmoid(c))
    w_hi, w_lo = _split_bf16(w_ref[0])
    o_ref[0] = _dot(a_hi, w_hi) + _dot(a_lo, w_hi) + _dot(a_hi, w_lo) + b_ref[0]


def _modulation(cc, w_mod, b_mod):
    depth, d, six_d = w_mod.shape
    rows = cc.shape[0]
    tn = 1536
    return pl.pallas_call(
        _mod_kernel,
        grid=(depth, six_d // tn),
        in_specs=[
            pl.BlockSpec((rows, d), lambda l, j: (0, 0)),
            pl.BlockSpec((1, d, tn), lambda l, j: (l, 0, j)),
            pl.BlockSpec((1, 1, tn), lambda l, j: (l, 0, j)),
        ],
        out_specs=pl.BlockSpec((1, rows, tn), lambda l, j: (l, 0, j)),
        out_shape=jax.ShapeDtypeStruct((depth, rows, six_d), F32),
        compiler_params=_params("arbitrary", "arbitrary"),
        name="modulation",
    )(cc, w_mod, b_mod.reshape(depth, 1, six_d))


def _inproj_kernel(x_ref, xp_ref, xn_ref, mod_ref, n1_ref, w_ref, cos_ref, sa_ref, sb_ref, cw_ref,
                   q_ref, k_ref, vt_ref, c_ref, *, tm, n_tiles):
    i = pl.program_id(1)
    m = mod_ref[0]
    shift, scale = m[0:1], m[1:2]
    g = n1_ref[...]
    h = _rms_mod(x_ref[0], g, shift, scale).astype(BF16)

    cos, sa, sb = cos_ref[...], sa_ref[...], sb_ref[...]
    for col0, out_ref, qscale in ((0, q_ref, QK_DIM ** -0.5 * math.log2(math.e)), (Q_W, k_ref, 1.0)):
        p = _dot(h, w_ref[:, col0:col0 + Q_W])
        for j in range(Q_W // LANES):
            pj = p[:, j * LANES:(j + 1) * LANES]
            r = pj * cos + pltpu.roll(pj, LANES - 16, 1) * sa + pltpu.roll(pj, 16, 1) * sb
            out_ref[0, :, j * LANES:(j + 1) * LANES] = (r * qscale).astype(BF16)

    vt_ref[0] = _dot(h, w_ref[:, 2 * Q_W:2 * Q_W + 512]).T.astype(BF16)

    pc = _dot(h, w_ref[:, 2 * Q_W + 512:])
    gb = pc[:, :CONV_W]
    u = pc[:, CONV_W:2 * CONV_W] * pc[:, 2 * CONV_W:]
    xh = jnp.concatenate([xp_ref[0], xn_ref[0]], axis=0)
    hh = _rms_mod(xh, g, shift, scale).astype(BF16)
    ph = _dot(hh, w_ref[:, 2 * Q_W + 512 + CONV_W:])
    uh = ph[:, :CONV_W] * ph[:, CONV_W:]
    u_before = jnp.where(i > 0, uh[HALO_ROWS - 1:HALO_ROWS], 0.0)
    u_after = jnp.where(i < n_tiles - 1, uh[HALO_ROWS:HALO_ROWS + 1], 0.0)
    row = lax.broadcasted_iota(jnp.int32, (tm, 1), 0)
    u_prev = jnp.where(row == 0, u_before, pltpu.roll(u, 1, 0))
    u_next = jnp.where(row == tm - 1, u_after, pltpu.roll(u, tm - 1, 0))
    cw = cw_ref[...]
    conv = cw[0:1] * u_prev + cw[1:2] * u + cw[2:3] * u_next
    c_ref[0] = (gb * conv).astype(BF16)


def _rope_tables(n):
    rows = n // GRID_W
    row = jnp.repeat(jnp.arange(rows), GRID_W).astype(F32)
    col = jnp.tile(jnp.arange(GRID_W), rows).astype(F32)
    half = QK_DIM // 2
    inv = 1.0 / (ROPE_THETA ** (jnp.arange(0, half, 2, dtype=F32) / half))
    ang_r = row[:, None] * inv
    ang_c = col[:, None] * inv
    ang = jnp.concatenate([ang_r, ang_r, ang_c, ang_c], axis=-1)
    cos, sin = jnp.cos(ang), jnp.sin(ang)
    first_half = (jnp.arange(QK_DIM) % 32) < 16
    sin_a = jnp.where(first_half, -sin, 0.0)
    sin_b = jnp.where(first_half, 0.0, sin)
    tile2 = lambda t: jnp.concatenate([t, t], axis=-1)
    return tile2(cos), tile2(sin_a), tile2(sin_b)


def _in_projection(x, mod0, norm1, w_in_bf, conv_w, tm):
    b, n, d = x.shape
    n_tiles = n // tm
    hb = tm // HALO_ROWS
    n_hblocks = n // HALO_ROWS
    cos, sa, sb = _rope_tables(n)
    out = jax.ShapeDtypeStruct((b, n, Q_W), BF16)
    row_spec = pl.BlockSpec((1, tm, Q_W), lambda bi, i: (bi, i, 0))
    tab_spec = pl.BlockSpec((tm, LANES), lambda bi, i: (i, 0))
    return pl.pallas_call(
        functools.partial(_inproj_kernel, tm=tm, n_tiles=n_tiles),
        grid=(b, n_tiles),
        in_specs=[
            pl.BlockSpec((1, tm, d), lambda bi, i: (bi, i, 0)),
            pl.BlockSpec((1, HALO_ROWS, d), lambda bi, i: (bi, jnp.maximum(i * hb - 1, 0), 0)),
            pl.BlockSpec((1, HALO_ROWS, d), lambda bi, i: (bi, jnp.minimum((i + 1) * hb, n_hblocks - 1), 0)),
            pl.BlockSpec((1, 6, d), lambda bi, i: (bi, 0, 0)),
            pl.BlockSpec((1, d), lambda bi, i: (0, 0)),
            pl.BlockSpec((d, EVEN_IN), lambda bi, i: (0, 0)),
            tab_spec, tab_spec, tab_spec,
            pl.BlockSpec((3, CONV_W), lambda bi, i: (0, 0)),
        ],
        out_specs=[row_spec, row_spec, pl.BlockSpec((1, Q_W, tm), lambda bi, i: (bi, 0, i)), row_spec],
        out_shape=[out, out, jax.ShapeDtypeStruct((b, Q_W, n), BF16), out],
        compiler_params=_params("arbitrary", "arbitrary"),
        name="in_projection",
    )(x, x, x, mod0, norm1, w_in_bf, cos, sa, sb, conv_w)


def _ctxkv_kernel(x_ref, mod_ref, n1_ref, wk_ref, wv_ref, k_ref, vt_ref):
    bb, m_ctx, d = x_ref.shape
    m = mod_ref[0]
    h = _rms_mod(x_ref[...].reshape(bb * m_ctx, d), n1_ref[...], m[0:1], m[1:2]).astype(BF16)
    k_ref[...] = _dot(h, wk_ref[...]).astype(BF16).reshape(k_ref.shape)
    v = _dot(h, wv_ref[...])
    for j in range(bb):
        vt_ref[j] = v[j * m_ctx:(j + 1) * m_ctx].T.astype(BF16)


def _context_kv(ctx, modc, norm1, w_in_bf):
    b, m, d = ctx.shape
    bb = next(k for k in (4, 2, 1) if b % k == 0)
    out = jax.ShapeDtypeStruct((b, m, Q_W), BF16)
    return pl.pallas_call(
        _ctxkv_kernel,
        grid=(b // bb,),
        in_specs=[
            pl.BlockSpec((bb, m, d), lambda bi: (bi, 0, 0)),
            pl.BlockSpec((1, 6, d), lambda bi: (0, 0, 0)),
            pl.BlockSpec((1, d), lambda bi: (0, 0)),
            pl.BlockSpec((d, Q_W), lambda bi: (0, 1)),
            pl.BlockSpec((d, Q_W), lambda bi: (0, 2)),
        ],
        out_specs=[pl.BlockSpec((bb, m, Q_W), lambda bi: (bi, 0, 0)), pl.BlockSpec((bb, Q_W, m), lambda bi: (bi, 0, 0))],
        out_shape=[out, jax.ShapeDtypeStruct((b, Q_W, m), BF16)],
        compiler_params=_params("arbitrary"),
        name="context_kv",
    )(ctx, modc, norm1, w_in_bf, w_in_bf)


def _attn_kernel(lam_ref, q_ref, qall_ref, kc_ref, k_ref, vct_ref, vt_ref, sub_ref, o_ref,
                 kx_ref, vxt_ref, shift_ref, safe_ref, s_ref, e_ref, ox_ref, *, lam_init, m_ctx):
    tq = q_ref.shape[1]

    @pl.when(pl.program_id(2) == 0)
    def _():
        kx_ref[:m_ctx, :] = kc_ref[0]
        kx_ref[m_ctx:, :] = k_ref[0]
        vxt_ref[:, :m_ctx] = vct_ref[0]
        vxt_ref[:, m_ctx:] = vt_ref[0]

        dim = lax.broadcasted_iota(jnp.int32, (LANES, LANES), 0)
        col = lax.broadcasted_iota(jnp.int32, (LANES, LANES), 1)
        ind = jnp.where(col == jnp.where(dim < QK_DIM, 0, 1), 1.0, 0.0).astype(BF16)

        def sq_norms(x):
            xf = x.astype(F32)
            return _dot((xf * xf).astype(BF16), ind)

        kmax = jnp.max(sq_norms(kx_ref[...]), axis=0, keepdims=True)
        qn = sq_norms(qall_ref[0]).T
        bound = jnp.sqrt(qn[0:SUBLANES] * jnp.concatenate(
            [kmax[:, 0:1], kmax[:, 1:2], jnp.zeros((SUBLANES - 2, 1), F32)], axis=0)) * 1.05
        shift_ref[...] = bound
        safe_ref[0] = (jnp.max(bound) <= SAFE_SHIFT).astype(jnp.int32)

    lp = lam_ref[...]
    lam = (jnp.exp(jnp.sum(lp[0:1] * lp[1:2], keepdims=True))
           - jnp.exp(jnp.sum(lp[2:3] * lp[3:4], keepdims=True)) + lam_init)
    q = q_ref[0]
    lane = lax.broadcasted_iota(jnp.int32, q.shape, 1)
    zero = jnp.zeros_like(q)
    qms = [jnp.where(in_map, q, zero) for in_map in (lane < QK_DIM, lane >= QK_DIM)]
    q0 = pl.multiple_of(pl.program_id(2) * tq, tq)
    shifts = [shift_ref[mi:mi + 1, pl.ds(q0, tq)] for mi in range(2)]
    bound_is_safe = safe_ref[0] == 1

    def weighted_values(mi, e32):
        e_ref[mi] = e32.astype(BF16)
        ox_ref[mi, :V_DIM, :] = _dot(vxt_ref[...], e_ref[mi])
        ox_ref[mi, V_DIM:, :] = jnp.broadcast_to(jnp.sum(e32, axis=0, keepdims=True), (SUBLANES, tq))

    @pl.when(bound_is_safe)
    def _():
        for mi in range(2):
            st = lax.dot_general(kx_ref[...], qms[mi], NT_DIMS, preferred_element_type=F32)
            weighted_values(mi, jnp.exp2(st - shifts[mi]))

    @pl.when(jnp.logical_not(bound_is_safe))
    def _():
        for mi in range(2):
            s_ref[mi] = lax.dot_general(kx_ref[...], qms[mi], NT_DIMS, preferred_element_type=F32)
        for mi in range(2):
            weighted_values(mi, jnp.exp2(s_ref[mi] - jnp.max(s_ref[mi], axis=0, keepdims=True)))

    o1, l1 = ox_ref[0, :V_DIM, :], ox_ref[0, V_DIM:V_DIM + 1, :]
    o2, l2 = ox_ref[1, :V_DIM, :], ox_ref[1, V_DIM:V_DIM + 1, :]
    ot = o1 * (1.0 / l1) - o2 * (lam / l2)
    ot = ot * lax.rsqrt(jnp.mean(ot * ot, axis=0, keepdims=True) + NORM_EPS)
    o_ref[0] = (ot * sub_ref[...] * (1.0 - lam_init)).T.astype(BF16)


def _attention(q, kc, k, vct, vt, lam_p, subln, lam_init, tq):
    b, n, _ = q.shape
    m_ctx = kc.shape[1]
    head_rows = lambda rows: pl.BlockSpec((1, rows, V_DIM), lambda bi, h, i: (bi, 0, h))
    head_cols = lambda cols: pl.BlockSpec((1, V_DIM, cols), lambda bi, h, i: (bi, h, 0))
    return pl.pallas_call(
        functools.partial(_attn_kernel, lam_init=lam_init, m_ctx=m_ctx),
        grid=(b, ATTN_HEADS, n // tq),
        in_specs=[
            pl.BlockSpec((4, QK_DIM), lambda bi, h, i: (0, 0)),
            pl.BlockSpec((1, tq, V_DIM), lambda bi, h, i: (bi, i, h)),
            head_rows(n), head_rows(m_ctx), head_rows(n), head_cols(m_ctx), head_cols(n),
            pl.BlockSpec((V_DIM, 1), lambda bi, h, i: (0, 0)),
        ],
        out_specs=pl.BlockSpec((1, tq, V_DIM), lambda bi, h, i: (bi, i, h)),
        out_shape=jax.ShapeDtypeStruct((b, n, Q_W), BF16),
        scratch_shapes=[pltpu.VMEM((m_ctx + n, V_DIM), BF16), pltpu.VMEM((V_DIM, m_ctx + n), BF16),
                        pltpu.VMEM((SUBLANES, n), F32), pltpu.SMEM((1,), jnp.int32),
                        pltpu.VMEM((2, m_ctx + n, tq), F32), pltpu.VMEM((2, m_ctx + n, tq), BF16),
                        pltpu.VMEM((2, V_DIM + SUBLANES, tq), F32)],
        compiler_params=_params("arbitrary", "arbitrary", "arbitrary"),
        name="diff_attention",
    )(lam_p, q, q, kc, k, vct, vt, subln)


def _tail(y, x, m, n2, wr_ref, x1_ref, h2_ref, pt_ref):
    x1 = x + m[2:3] * y
    x1_ref[0] = x1
    h2 = _rms_mod(x1, n2, m[3:4], m[4:5])
    h_hi, h_lo = _split_bf16(h2)
    h2_ref[0] = h_hi
    hcat = jnp.concatenate([h_hi, h_lo], axis=1)
    half = hcat.shape[0] // 2
    parts = [_dot(hcat[r0:r0 + half], wr_ref[...]) for r0 in (0, half)]
    prod = jnp.concatenate(parts, axis=0)
    logits = (prod[:, :LANES] + prod[:, LANES:]).T[:N_EXPERTS]
    ex = jnp.exp(logits - jnp.max(logits, axis=0, keepdims=True))
    pt_ref[0] = ex / jnp.sum(ex, axis=0, keepdims=True)


def _tail_specs(b, n, d, tm):
    in_specs = [
        pl.BlockSpec((1, d), lambda bi, i, *_: (0, 0)),
        pl.BlockSpec((2 * d, 2 * LANES), lambda bi, i, *_: (0, 0)),
    ]
    out_specs = [
        pl.BlockSpec((1, tm, d), lambda bi, i, *_: (bi, i, 0)),
        pl.BlockSpec((1, tm, d), lambda bi, i, *_: (bi, i, 0)),
        pl.BlockSpec((1, N_EXPERTS, tm), lambda bi, i, *_: (bi, 0, i)),
    ]
    out_shape = [
        jax.ShapeDtypeStruct((b, n, d), F32),
        jax.ShapeDtypeStruct((b, n, d), BF16),
        jax.ShapeDtypeStruct((b, N_EXPERTS, n), F32),
    ]
    return in_specs, out_specs, out_shape


def _outproj_kernel(a_ref, c_ref, x_ref, mod_ref, wo_ref, n2_ref, wr_ref,
                    x1_ref, h2_ref, pt_ref, mix_ref):
    mix_ref[:, :Q_W] = a_ref[0]
    mix_ref[:, Q_W:] = c_ref[0]
    y = _dot(mix_ref[...], wo_ref[...])
    _tail(y, x_ref[0], mod_ref[0], n2_ref[...], wr_ref, x1_ref, h2_ref, pt_ref)


def _out_projection(attn, conv, x, mod0, w_out_bf, norm2, wr_cat, tm):
    b, n, d = x.shape
    tail_in, out_specs, out_shape = _tail_specs(b, n, d, tm)
    return pl.pallas_call(
        _outproj_kernel,
        grid=(b, n // tm),
        in_specs=[
            pl.BlockSpec((1, tm, Q_W), lambda bi, i: (bi, i, 0)),
            pl.BlockSpec((1, tm, CONV_W), lambda bi, i: (bi, i, 0)),
            pl.BlockSpec((1, tm, d), lambda bi, i: (bi, i, 0)),
            pl.BlockSpec((1, 6, d), lambda bi, i: (bi, 0, 0)),
            pl.BlockSpec((d, d), lambda bi, i: (0, 0)),
        ] + tail_in,
        out_specs=out_specs,
        out_shape=out_shape,
        scratch_shapes=[pltpu.VMEM((tm, d), BF16)],
        compiler_params=_params("arbitrary", "arbitrary"),
        name="out_projection",
    )(attn, conv, x, mod0, w_out_bf, norm2, wr_cat)


def _gelu(x):
    return 0.5 * x * (1.0 + lax.erf(x * np.float32(1.0 / math.sqrt(2.0))))


def _cmlp_body(x, m, n1_ref, w1_ref, vn_ref, ws_ref, bs_ref, wo_ref, n2_ref, wr_ref,
               x1_ref, h2_ref, pt_ref, mix_ref):
    tm, d = x.shape
    h = _rms_mod(x, n1_ref[...], m[0:1], m[1:2]).astype(BF16)
    p = _gelu(_dot(h, w1_ref[...]))
    u, v = p[:, :d], p[:, d:]
    v = v * lax.rsqrt(jnp.mean(v * v, axis=-1, keepdims=True) + NORM_EPS) * vn_ref[...]
    vb = v.astype(BF16)
    gw = d // CMLP_GROUPS
    bs = bs_ref[...]
    for c in range(tm // CHUNK):
        r0 = c * CHUNK
        for g in range(CMLP_GROUPS):
            s = _dot(ws_ref[g], vb[r0:r0 + CHUNK, g * gw:(g + 1) * gw]) + bs[:, g:g + 1]
            mix_ref[r0:r0 + CHUNK, g * gw:(g + 1) * gw] = (u[r0:r0 + CHUNK, g * gw:(g + 1) * gw] * s).astype(BF16)
    y = _dot(mix_ref[...], wo_ref[...])
    _tail(y, x, m, n2_ref[...], wr_ref, x1_ref, h2_ref, pt_ref)


def _route_kernel(p_ref, sel_ref, start_ref, fits_ref, *, cap, pchunk, window):
    rows, n = p_ref.shape

    def bit_step(it, t):
        cand = t | jnp.left_shift(jnp.int32(1), 30 - it)
        cf = lax.bitcast_convert_type(cand, F32)
        cnt = jnp.sum(jnp.where(p_ref[...] >= cf, 1.0, 0.0), axis=1, keepdims=True)
        return jnp.where(cnt >= cap, cand, t)

    t = lax.fori_loop(0, 31, bit_step, jnp.zeros((rows, 1), jnp.int32))
    tf = lax.bitcast_convert_type(t, F32)
    p = p_ref[...]
    gt = p > tf
    eq = jnp.logical_and(p >= tf, jnp.logical_not(gt))
    need = cap - jnp.sum(jnp.where(gt, 1.0, 0.0), axis=1, keepdims=True)

    ri = lax.broadcasted_iota(jnp.int32, (pchunk, pchunk), 0)
    ci = lax.broadcasted_iota(jnp.int32, (pchunk, pchunk), 1)
    upper = jnp.where(ri < ci, 1.0, 0.0).astype(BF16)

    def prefix(mask):
        mf = jnp.where(mask, 1.0, 0.0)
        carry = jnp.zeros((rows, 1), F32)
        parts, before, inside = [], [], []
        for c in range(n // pchunk):
            blk = mf[:, c * pchunk:(c + 1) * pchunk]
            parts.append(_dot(blk.astype(BF16), upper) + carry)
            total = jnp.sum(blk, axis=1, keepdims=True)
            before.append(carry)
            inside.append(total)
            carry = carry + total
        return jnp.concatenate(parts, axis=1), jnp.concatenate(before, axis=1), jnp.concatenate(inside, axis=1)

    chosen = jnp.logical_or(gt, jnp.logical_and(eq, prefix(eq)[0] < need))
    rank, before, inside = prefix(chosen)
    sel_ref[...] = jnp.where(chosen, rank, -1.0)
    start = jnp.minimum(jnp.floor(before * (1.0 / ALIGN_ROWS)) * ALIGN_ROWS, float(cap - window))
    start_ref[...] = start.astype(jnp.int32)
    fits_ref[...] = jnp.where(before + inside <= start + window, 1, 0).astype(jnp.int32)


def _route(pt, cap, window):
    b, e, n = pt.shape
    rows = b * e
    pchunk = min(SUB_TOKENS, n)
    n_chunks = n // pchunk
    full = lambda cols: pl.BlockSpec((rows, cols), lambda i: (0, 0))
    sel, start, fits = pl.pallas_call(
        functools.partial(_route_kernel, cap=cap, pchunk=pchunk, window=window),
        grid=(1,),
        in_specs=[full(n)],
        out_specs=[full(n), full(n_chunks), full(n_chunks)],
        out_shape=[jax.ShapeDtypeStruct((rows, n), F32), jax.ShapeDtypeStruct((rows, n_chunks), jnp.int32),
                   jax.ShapeDtypeStruct((rows, n_chunks), jnp.int32)],
        compiler_params=_params("arbitrary"),
        name="route",
    )(pt.reshape(rows, n))
    return sel.reshape(b, e, n), start.reshape(-1), fits.reshape(-1)


def _gather_kernel(start_ref, fits_ref, sel_ref, pt_ref, h_ref, xs_ref, vals_ref, *, cap, window, sub, n_chunks):
    bi = pl.program_id(0)
    entry = lambda e, c: (bi * N_EXPERTS + e) * n_chunks + c
    lane = lax.broadcasted_iota(jnp.int32, (1, LANES), 1)

    all_fit = fits_ref[entry(0, 0)]
    for e in range(N_EXPERTS):
        for c in range(n_chunks):
            all_fit = jnp.minimum(all_fit, fits_ref[entry(e, c)])

    vals_ref[...] = jnp.zeros(vals_ref.shape, F32)

    @pl.when(all_fit == 1)
    def _():
        xs_ref[...] = jnp.zeros(xs_ref.shape, BF16)
        offs = lax.broadcasted_iota(jnp.int32, (window, sub), 0).astype(F32)
        for c in range(n_chunks):
            sel = sel_ref[0, :, c * sub:(c + 1) * sub]
            prob = pt_ref[0, :, c * sub:(c + 1) * sub]
            hc = h_ref[0, c * sub:(c + 1) * sub, :]
            starts = [pl.multiple_of(start_ref[entry(e, c)], ALIGN_ROWS) for e in range(N_EXPERTS)]
            hits = [sel[e:e + 1, :] - a.astype(F32) == offs for e, a in enumerate(starts)]
            hots = [jnp.where(hit, 1.0, 0.0).astype(BF16) for hit in hits]
            z = _dot(jnp.concatenate(hots, axis=0), hc).astype(BF16)
            for e, a in enumerate(starts):
                xs_ref[0, e, pl.ds(a, window), :] += z[e * window:(e + 1) * window]
                w = jnp.sum(jnp.where(hits[e], prob[e:e + 1, :], 0.0), axis=1, keepdims=True)
                vals_ref[0, pl.ds(a, window), :] += jnp.where(lane == e, w, 0.0)

    @pl.when(all_fit != 1)
    def _():
        n = sel_ref.shape[-1]
        rank = lax.broadcasted_iota(jnp.int32, (cap, n), 0).astype(F32)
        for e in range(N_EXPERTS):
            hit = sel_ref[0, e:e + 1, :] == rank
            xs_ref[0, e] = _dot(jnp.where(hit, 1.0, 0.0).astype(BF16), h_ref[0]).astype(BF16)
            w = jnp.sum(jnp.where(hit, pt_ref[0, e:e + 1, :], 0.0), axis=1, keepdims=True)
            vals_ref[0] += jnp.where(lane == e, w, 0.0)


def _gather(route, pt, h2):
    b, n, d = h2.shape
    cap = route.geom["cap"]
    experts_by_tokens = pl.BlockSpec((1, N_EXPERTS, n), lambda bi, *_: (bi, 0, 0))
    return pl.pallas_call(
        functools.partial(_gather_kernel, **route.geom),
        grid_spec=pltpu.PrefetchScalarGridSpec(
            num_scalar_prefetch=2,
            grid=(b,),
            in_specs=[experts_by_tokens, experts_by_tokens, pl.BlockSpec((1, n, d), lambda bi, *_: (bi, 0, 0))],
            out_specs=[pl.BlockSpec((1, N_EXPERTS, cap, d), lambda bi, *_: (bi, 0, 0, 0)),
                       pl.BlockSpec((1, cap, LANES), lambda bi, *_: (bi, 0, 0))],
        ),
        out_shape=[jax.ShapeDtypeStruct((b, N_EXPERTS, cap, d), BF16), jax.ShapeDtypeStruct((b, cap, LANES), F32)],
        compiler_params=_params("arbitrary"),
        name="moe_gather",
    )(route.start, route.fits, route.sel, pt, h2)


def _expert_kernel(vals_ref, xs_ref, wg_ref, wu_ref, wd_ref, y_ref, wgu_s, wd_s):
    e = pl.program_id(0)
    hdim = wd_s.shape[0]

    @pl.when(pl.program_id(1) == 0)
    def _():
        wgu_s[:, :hdim] = wg_ref[0, 0].astype(BF16)
        wgu_s[:, hdim:] = wu_ref[0, 0].astype(BF16)
        wd_s[...] = wd_ref[0, 0].astype(BF16)

    bb, cap, _ = vals_ref.shape
    d = xs_ref.shape[-1]
    lane = lax.broadcasted_iota(jnp.int32, (1, 1, LANES), 2)
    vals = jnp.sum(jnp.where(lane == e, vals_ref[...], 0.0), axis=2, keepdims=True).reshape(bb * cap, 1)
    xs = xs_ref[...].reshape(bb * cap, d)
    ab = _dot(xs, wgu_s[...])
    a, bm = ab[:, :hdim], ab[:, hdim:]
    hid = (a * jax.nn.sigmoid(a) * bm * vals).astype(BF16)
    y_ref[...] = _dot(hid, wd_s[...]).astype(BF16).reshape(y_ref.shape)


def _expert_ffn(vals, xs, w_gate, w_up, w_down, layer):
    b, _, cap, d = xs.shape
    hdim = w_gate.shape[-1]
    bb = next(k for k in (4, 2, 1) if b % k == 0)
    w_spec = lambda rows, cols: pl.BlockSpec((1, 1, rows, cols), lambda e, bi: (layer, e, 0, 0))
    return pl.pallas_call(
        _expert_kernel,
        grid=(N_EXPERTS, b // bb),
        in_specs=[
            pl.BlockSpec((bb, cap, LANES), lambda e, bi: (bi, 0, 0)),
            pl.BlockSpec((bb, 1, cap, d), lambda e, bi: (bi, e, 0, 0)),
            w_spec(d, hdim), w_spec(d, hdim), w_spec(hdim, d),
        ],
        out_specs=pl.BlockSpec((bb, 1, cap, d), lambda e, bi: (bi, e, 0, 0)),
        out_shape=jax.ShapeDtypeStruct((b, N_EXPERTS, cap, d), BF16),
        scratch_shapes=[pltpu.VMEM((d, 2 * hdim), BF16), pltpu.VMEM((hdim, d), BF16)],
        compiler_params=_params("arbitrary", "arbitrary"),
        name="expert_ffn",
    )(vals, xs, w_gate, w_up, w_down)


def _scatter_add(start_ref, fits_ref, sel_ref, y_ref, acc_ref, *, cap, window, sub, n_chunks, y_slot=0):
    bi, i = pl.program_id(0), pl.program_id(1)
    tm = sel_ref.shape[-1]
    n_sub = tm // sub
    entry = lambda e, j: (bi * N_EXPERTS + e) * n_chunks + i * n_sub + j

    all_fit = fits_ref[entry(0, 0)]
    for e in range(N_EXPERTS):
        for j in range(n_sub):
            all_fit = jnp.minimum(all_fit, fits_ref[entry(e, j)])

    @pl.when(all_fit == 1)
    def _():
        offs = lax.broadcasted_iota(jnp.int32, (window, sub), 0).astype(F32)
        for j in range(n_sub):
            sel = sel_ref[0, :, j * sub:(j + 1) * sub]
            hots, ys = [], []
            for e in range(N_EXPERTS):
                a = pl.multiple_of(start_ref[entry(e, j)], ALIGN_ROWS)
                hots.append(jnp.where(sel[e:e + 1, :] - a.astype(F32) == offs, 1.0, 0.0).astype(BF16))
                ys.append(y_ref[y_slot, e, pl.ds(a, window), :])
            acc_ref[j * sub:(j + 1) * sub, :] = lax.dot_general(
                jnp.concatenate(hots, axis=0), jnp.concatenate(ys, axis=0), TN_DIMS, preferred_element_type=F32)

    @pl.when(all_fit != 1)
    def _():
        sel = sel_ref[0]
        rank = lax.broadcasted_iota(jnp.int32, (cap, tm), 0).astype(F32)
        acc = jnp.zeros(acc_ref.shape, F32)
        for e in range(N_EXPERTS):
            onehot = jnp.where(sel[e:e + 1, :] == rank, 1.0, 0.0).astype(BF16)
            acc = acc + lax.dot_general(onehot, y_ref[y_slot, e], TN_DIMS, preferred_element_type=F32)
        acc_ref[...] = acc


def _ring_expert_outputs(y_hbm, ybuf, sem):
    bi, i = pl.program_id(0), pl.program_id(1)
    n_b, n_i = pl.num_programs(0), pl.num_programs(1)
    piece = N_EXPERTS // n_i
    slot = lax.rem(bi, 2)

    def fetch(batch, to_slot, p):
        return pltpu.make_async_copy(y_hbm.at[batch, pl.ds(p * piece, piece)],
                                     ybuf.at[to_slot, pl.ds(p * piece, piece)], sem.at[to_slot, p])

    @pl.when(jnp.logical_and(bi == 0, i == 0))
    def _():
        for p in range(n_i):
            fetch(0, 0, p).start()

    @pl.when(i == 0)
    def _():
        for p in range(n_i):
            fetch(bi, slot, p).wait()

    @pl.when(bi + 1 < n_b)
    def _():
        for p in range(n_i):
            @pl.when(i == p)
            def _():
                fetch(bi + 1, 1 - slot, p).start()

    return slot


def _combine_final_kernel(start_ref, fits_ref, sel_ref, y_hbm, x_ref, mod_ref, fn_ref, o_ref, acc_ref, ybuf, sem,
                          **geom):
    slot = _ring_expert_outputs(y_hbm, ybuf, sem)
    _scatter_add(start_ref, fits_ref, sel_ref, ybuf, acc_ref, y_slot=slot, **geom)
    x2 = x_ref[0] + mod_ref[0][5:6] * acc_ref[...]
    o_ref[0] = x2 * lax.rsqrt(jnp.mean(x2 * x2, axis=-1, keepdims=True) + NORM_EPS) * fn_ref[...]


def _combine_cmlp_kernel(start_ref, fits_ref, sel_ref, y_hbm, x_ref, mod0_ref, mod1_ref, n1_ref, w1_ref, vn_ref,
                         ws_ref, bs_ref, wo_ref, n2_ref, wr_ref, x1_ref, h2_ref, pt_ref, acc_ref, mix_ref, ybuf, sem,
                         **geom):
    slot = _ring_expert_outputs(y_hbm, ybuf, sem)
    _scatter_add(start_ref, fits_ref, sel_ref, ybuf, acc_ref, y_slot=slot, **geom)
    x2 = x_ref[0] + mod0_ref[0][5:6] * acc_ref[...]
    _cmlp_body(x2, mod1_ref[0], n1_ref, w1_ref, vn_ref, ws_ref, bs_ref, wo_ref, n2_ref, wr_ref,
               x1_ref, h2_ref, pt_ref, mix_ref)


def _combine_specs(b, n, d, cap, tm):
    return [
        pl.BlockSpec((1, N_EXPERTS, tm), lambda bi, i, *_: (bi, 0, i)),
        pl.BlockSpec((1, N_EXPERTS, cap, d), lambda bi, i, *_: (bi, 0, 0, 0)),
        pl.BlockSpec((1, tm, d), lambda bi, i, *_: (bi, i, 0)),
        pl.BlockSpec((1, 6, d), lambda bi, i, *_: (bi, 0, 0)),
    ]


def _combine_final(route, y, x1, mod_l, final_norm, tm):
    b, n, d = x1.shape
    cap = route.geom["cap"]
    n_tiles = n // tm
    assert N_EXPERTS % n_tiles == 0
    specs = _combine_specs(b, n, d, cap, tm)
    specs[1] = pl.BlockSpec(memory_space=pl.ANY)
    return pl.pallas_call(
        functools.partial(_combine_final_kernel, **route.geom),
        grid_spec=pltpu.PrefetchScalarGridSpec(
            num_scalar_prefetch=2,
            grid=(b, n_tiles),
            in_specs=specs + [pl.BlockSpec((1, d), lambda bi, i, *_: (0, 0))],
            out_specs=pl.BlockSpec((1, tm, d), lambda bi, i, *_: (bi, i, 0)),
            scratch_shapes=[pltpu.VMEM((tm, d), F32), pltpu.VMEM((2, N_EXPERTS, cap, d), BF16),
                            pltpu.SemaphoreType.DMA((2, n_tiles))],
        ),
        out_shape=jax.ShapeDtypeStruct((b, n, d), F32),
        compiler_params=_params("arbitrary", "arbitrary"),
        name="moe_combine_final",
    )(route.start, route.fits, route.sel, y, x1, mod_l, final_norm)


def _combine_chunk_mlp(route, y, x1, mod0, mod1, norm1, w1_bf, v_norm, ws_bf, bs_t, w_out_bf, norm2, wr_cat, tm):
    b, n, d = x1.shape
    const = lambda *shape: pl.BlockSpec(shape, lambda bi, i, *_: (0,) * len(shape))
    tail_in, out_specs, out_shape = _tail_specs(b, n, d, tm)
    cap = route.geom["cap"]
    n_tiles = n // tm
    assert N_EXPERTS % n_tiles == 0
    specs = _combine_specs(b, n, d, cap, tm)
    specs[1] = pl.BlockSpec(memory_space=pl.ANY)
    return pl.pallas_call(
        functools.partial(_combine_cmlp_kernel, **route.geom),
        grid_spec=pltpu.PrefetchScalarGridSpec(
            num_scalar_prefetch=2,
            grid=(b, n_tiles),
            in_specs=specs + [
                pl.BlockSpec((1, 6, d), lambda bi, i, *_: (bi, 0, 0)),
                const(1, d), const(d, 2 * d), const(1, d), const(CMLP_GROUPS, CHUNK, CHUNK),
                const(CHUNK, CMLP_GROUPS), const(d, d),
            ] + tail_in,
            out_specs=out_specs,
            scratch_shapes=[pltpu.VMEM((tm, d), F32), pltpu.VMEM((tm, d), BF16),
                            pltpu.VMEM((2, N_EXPERTS, cap, d), BF16), pltpu.SemaphoreType.DMA((2, n_tiles))],
        ),
        out_shape=out_shape,
        compiler_params=_params("arbitrary", "arbitrary"),
        name="combine_chunk_mlp",
    )(route.start, route.fits, route.sel, y, x1, mod0, mod1, norm1, w1_bf, v_norm, ws_bf, bs_t, w_out_bf,
      norm2, wr_cat)


class _Routing(NamedTuple):
    sel: jax.Array
    start: jax.Array
    fits: jax.Array
    geom: dict


def _moe_experts(h2, pt, w_gate, w_up, w_down, layer):
    n = h2.shape[1]
    cap = CAPACITY_FACTOR * n // N_EXPERTS
    window = min(SLOT_WINDOW, cap)
    sub = min(SUB_TOKENS, n)
    sel, start, fits = _route(pt, cap, window)
    route = _Routing(sel, start, fits, dict(cap=cap, window=window, sub=sub, n_chunks=n // sub))
    xs, vals = _gather(route, pt, h2)
    return route, _expert_ffn(vals, xs, w_gate, w_up, w_down, layer)


def kernel(x, c, ctx, c_ctx, w_mod, b_mod, norm1, norm2, even_w_in, even_lambda, even_subln, even_conv_w,
           odd_w_in, odd_v_norm, odd_w_s, odd_b_s, w_out, w_router, w_gate, w_up, w_down, final_norm):
    b, n, d = x.shape
    depth = w_mod.shape[0]
    assert d == D_MODEL and depth == 2 and n % 256 == 0
    t_proj = min(1024, n)
    t_attn = min(1024, n)
    t_moe = min(512, n)
    t_final = min(1024, n)

    cc = jnp.concatenate([c, c_ctx[None, :]], axis=0)
    mod = _modulation(cc, w_mod, b_mod).reshape(depth, b + 1, 6, d)
    row = lambda a, l: a[l].reshape(1, -1)
    wr_pad = jnp.pad(w_router, ((0, 0), (0, 0), (0, LANES - N_EXPERTS)))
    wr_hi = wr_pad.astype(BF16)
    wr_lo = (wr_pad - wr_hi.astype(F32)).astype(BF16)
    wr_cat = jnp.concatenate([jnp.concatenate([wr_hi, wr_lo], axis=2),
                              jnp.concatenate([wr_hi, jnp.zeros_like(wr_lo)], axis=2)], axis=1)
    w_out_bf = w_out.astype(BF16)
    fnorm = final_norm.reshape(1, d)

    mod0, modc = mod[0, :b], mod[0, b:]
    w_in_bf = even_w_in[0].astype(BF16)
    q, k, vt, conv = _in_projection(x, mod0, row(norm1, 0), w_in_bf, even_conv_w[0], t_proj)
    kc, vct = _context_kv(ctx, modc, row(norm1, 0), w_in_bf)
    lam_init = 0.8 - 0.6 * math.exp(-0.3 * 0)
    attn = _attention(q, kc, k, vct, vt, even_lambda[0], even_subln[0].reshape(-1, 1), lam_init, t_attn)
    x1, h2, pt = _out_projection(attn, conv, x, mod0, w_out_bf[0], row(norm2, 0), wr_cat[0], t_proj)
    route, y = _moe_experts(h2, pt, w_gate, w_up, w_down, 0)

    mod1 = mod[1, :b]
    x1, h2, pt = _combine_chunk_mlp(route, y, x1, mod0, mod1, row(norm1, 1), odd_w_in[0].astype(BF16),
                                    row(odd_v_norm, 0), odd_w_s[0].astype(BF16), odd_b_s[0].T, w_out_bf[1],
                                    row(norm2, 1), wr_cat[1], t_moe)
    route, y = _moe_experts(h2, pt, w_gate, w_up, w_down, 1)
    return _combine_final(route, y, x1, mod1, fnorm, t_final)
```
